```python
import math
import jax, jax.numpy as jnp
from jax import lax
import numpy as np

D_MODEL = 2048
BATCH = 1
SEQ = 16384
DEPTH = 1

EPS = 1e-6
SSM_D_INNER = 2 * D_MODEL
SSM_HEAD_DIM = 64
SSM_N_HEADS = SSM_D_INNER // SSM_HEAD_DIM
SSM_N_GROUPS = 8
SSM_HEADS_PER_GROUP = SSM_N_HEADS // SSM_N_GROUPS
SSM_D_STATE = 128
SSM_CONV = 4
SSM_CHUNK = 128
SSM_CONV_DIM = SSM_D_INNER + 2 * SSM_N_GROUPS * SSM_D_STATE
SSM_PROJ = SSM_D_INNER + SSM_CONV_DIM + SSM_N_HEADS
GDN_N_HEADS = 16
GDN_HEAD_K = 128
GDN_HEAD_V = 256
GDN_KEY_DIM = GDN_N_HEADS * GDN_HEAD_K
GDN_VAL_DIM = GDN_N_HEADS * GDN_HEAD_V
GDN_CONV = 4
GDN_CHUNK = 64
GDN_CONV_DIM = 2 * GDN_KEY_DIM + GDN_VAL_DIM
GDN_PROJ = GDN_CONV_DIM + GDN_VAL_DIM + 2 * GDN_N_HEADS
N_BRANCHES = 2
GATE_PROJ = N_BRANCHES * D_MODEL
IN_PROJ = SSM_PROJ + GDN_PROJ + GATE_PROJ
PEER_HEADS = 8
PEER_N_KEYS = 128
PEER_N_EXPERTS = PEER_N_KEYS * PEER_N_KEYS
PEER_D_KEY = 256
PEER_HALF = PEER_D_KEY // 2
PEER_TOPK = 16
PEER_TOKEN_BLOCK = 128

kernel_name = "hybrid_ssd_gdn_peer_block"


def _rms_norm(x, w):
    xf = x.astype(jnp.float32)
    y = xf * lax.rsqrt(jnp.mean(xf * xf, axis=-1, keepdims=True) + EPS)
    return (y * w.astype(jnp.float32)).astype(x.dtype)


def _l2norm(x):
    return x * lax.rsqrt(jnp.sum(x * x, axis=-1, keepdims=True) + EPS)


def _causal_dwconv(x, w):
    k_w, t = w.shape[0], x.shape[1]
    xp = jnp.pad(x, ((0, 0), (k_w - 1, 0), (0, 0)))
    y = xp[:, 0:t] * w[0]
    for k in range(1, k_w):
        y = y + xp[:, k:k + t] * w[k]
    return y


def _ssd_branch(p, conv_w, conv_b, dt_bias, a_log, d_skip, norm_w):
    f32 = jnp.float32
    b_, t, _ = p.shape
    L, G, R, P, N = SSM_CHUNK, SSM_N_GROUPS, SSM_HEADS_PER_GROUP, SSM_HEAD_DIM, SSM_D_STATE
    nc = t // L
    z, xbc, dt = jnp.split(p, [SSM_D_INNER, SSM_D_INNER + SSM_CONV_DIM], axis=-1)
    xbc = jax.nn.silu(_causal_dwconv(xbc, conv_w) + conv_b).astype(f32)
    xs, bm, cm = jnp.split(xbc, [SSM_D_INNER, SSM_D_INNER + G * N], axis=-1)
    x = xs.reshape(b_, nc, L, G, R, P)
    bm = bm.reshape(b_, nc, L, G, N)
    cm = cm.reshape(b_, nc, L, G, N)
    dt = jax.nn.softplus(dt.astype(f32) + dt_bias.astype(f32)).reshape(b_, nc, L, G, R)
    a = -jnp.exp(a_log.astype(f32)).reshape(G, R)
    a_cs = jnp.cumsum(dt * a, axis=2)
    xdt = x * dt[..., None]
    causal = jnp.tril(jnp.ones((L, L), dtype=bool))
    seg = a_cs[:, :, :, None] - a_cs[:, :, None, :]
    decay = jnp.exp(jnp.where(causal[:, :, None, None], seg, -jnp.inf))
    cb = jnp.einsum('bclgn,bcsgn->bclsg', cm, bm)
    scores = cb[..., None] * decay
    y_diag = jnp.einsum('bclsgr,bcsgrp->bclgrp', scores, xdt)
    to_end = jnp.exp(a_cs[:, :, -1:] - a_cs)
    states = jnp.einsum('bcsgn,bcsgrp->bcgrpn', bm, xdt * to_end[..., None])
    c_dec = jnp.exp(a_cs[:, :, -1])

    def step(hs, inp):
        s_c, d_c = inp
        return hs * d_c[..., None, None] + s_c, hs

    h0 = jnp.zeros((b_, G, R, P, N), f32)
    _, h_in = lax.scan(step, h0, (jnp.moveaxis(states, 1, 0), jnp.moveaxis(c_dec, 1, 0)))
    h_in = jnp.moveaxis(h_in, 0, 1)
    y_off = jnp.einsum('bclgn,bcgrpn->bclgrp', cm, h_in) * jnp.exp(a_cs)[..., None]
    y = y_diag + y_off + x * d_skip.astype(f32).reshape(G, R)[:, :, None]
    y = y.reshape(b_, t, SSM_D_INNER) * jax.nn.silu(z.astype(f32))
    yg = y.reshape(b_, t, G, SSM_D_INNER // G)
    yg = yg * lax.rsqrt(jnp.mean(yg * yg, axis=-1, keepdims=True) + EPS)
    y = yg.reshape(b_, t, SSM_D_INNER) * norm_w.astype(f32)
    return y.astype(p.dtype)


def _gdn_branch(p, conv_w, dt_bias, a_log, norm_w):
    f32 = jnp.float32
    b_, t, _ = p.shape
    H, DK, DV, C = GDN_N_HEADS, GDN_HEAD_K, GDN_HEAD_V, GDN_CHUNK
    nc = t // C
    qkv, z, bt, al = jnp.split(
        p, [GDN_CONV_DIM, GDN_CONV_DIM + GDN_VAL_DIM, GDN_CONV_DIM + GDN_VAL_DIM + H], axis=-1)
    qkv = jax.nn.silu(_causal_dwconv(qkv, conv_w)).astype(f32)
    q, k, v = jnp.split(qkv, [GDN_KEY_DIM, 2 * GDN_KEY_DIM], axis=-1)
    q = _l2norm(q.reshape(b_, t, H, DK)) * (DK ** -0.5)
    k = _l2norm(k.reshape(b_, t, H, DK))
    v = v.reshape(b_, t, H, DV)
    beta = jax.nn.sigmoid(bt.astype(f32))
    g = -jnp.exp(a_log.astype(f32)) * jax.nn.softplus(al.astype(f32) + dt_bias.astype(f32))

    def chunked(a):
        return jnp.moveaxis(a.reshape((b_, nc, C) + a.shape[2:]), 2, 3)

    q, k, v, beta, g = chunked(q), chunked(k), chunked(v), chunked(beta), chunked(g)
    gc = jnp.cumsum(g, axis=-1)
    causal = jnp.tril(jnp.ones((C, C), dtype=bool))
    strict = jnp.tril(jnp.ones((C, C), dtype=bool), k=-1)
    decay = jnp.exp(jnp.where(causal, gc[..., :, None] - gc[..., None, :], -jnp.inf))
    kk = jnp.einsum('bnhid,bnhjd->bnhij', k, k)
    lmat = jnp.where(strict, kk * beta[..., :, None] * decay, 0.0)
    eye = jnp.eye(C, dtype=f32)
    t_inv = lax.linalg.triangular_solve(eye + lmat, jnp.broadcast_to(eye, lmat.shape),
                                        left_side=True, lower=True, unit_diagonal=True)
    u = t_inv @ (v * beta[..., None])
    w = t_inv @ (k * (beta * jnp.exp(gc))[..., None])
    attn = jnp.where(causal, jnp.einsum('bnhid,bnhjd->bnhij', q, k) * decay, 0.0)
    q_dec = q * jnp.exp(gc)[..., None]
    k_dec = k * jnp.exp(gc[..., -1:] - gc)[..., None]
    c_dec = jnp.exp(gc[..., -1])

    def step(s, inp):
        u_c, w_c, a_c, q_c, k_c, d_c = inp
        v_new = u_c - w_c @ s
        o_c = q_c @ s + a_c @ v_new
        s = s * d_c[..., None, None] + jnp.swapaxes(k_c, -1, -2) @ v_new
        return s, o_c

    xs = (jnp.moveaxis(u, 1, 0), jnp.moveaxis(w, 1, 0), jnp.moveaxis(attn, 1, 0),
          jnp.moveaxis(q_dec, 1, 0), jnp.moveaxis(k_dec, 1, 0), jnp.moveaxis(c_dec, 1, 0))
    s0 = jnp.zeros((b_, H, DK, DV), f32)
    _, o = lax.scan(step, s0, xs)
    o = jnp.moveaxis(jnp.moveaxis(o, 0, 1), 2, 3).reshape(b_, t, H, DV)
    o = o * lax.rsqrt(jnp.mean(o * o, axis=-1, keepdims=True) + EPS) * norm_w.astype(f32)
    o = o * jax.nn.silu(z.astype(f32).reshape(b_, t, H, DV))
    return o.reshape(b_, t, GDN_VAL_DIM).astype(p.dtype)


def _peer_ffn(xn, w_q, sub_keys, u_tab, v_tab):
    f32 = jnp.float32
    b_, t, d = xn.shape
    q = (xn @ w_q).astype(f32).reshape(b_, t, PEER_HEADS, 2, PEER_HALF)
    s = jnp.einsum('bthpd,hpkd->bthpk', q, sub_keys.astype(f32))
    s1, i1 = lax.top_k(s[..., 0, :], PEER_TOPK)
    s2, i2 = lax.top_k(s[..., 1, :], PEER_TOPK)
    cand = (s1[..., :, None] + s2[..., None, :]).reshape(b_, t, PEER_HEADS, PEER_TOPK * PEER_TOPK)
    cand_idx = (i1[..., :, None] * PEER_N_KEYS + i2[..., None, :]).reshape(
        b_, t, PEER_HEADS, PEER_TOPK * PEER_TOPK)
    top_s, pos = lax.top_k(cand, PEER_TOPK)
    idx = jnp.take_along_axis(cand_idx, pos, axis=-1)
    gate = jax.nn.softmax(top_s, axis=-1).astype(xn.dtype)
    nb = (b_ * t) // PEER_TOKEN_BLOCK
    xb = xn.reshape(nb, PEER_TOKEN_BLOCK, d)
    idx_b = idx.reshape(nb, PEER_TOKEN_BLOCK, PEER_HEADS, PEER_TOPK)
    g_b = gate.reshape(nb, PEER_TOKEN_BLOCK, PEER_HEADS, PEER_TOPK)

    def block(args):
        xt, it, gt = args
        u = u_tab[it]
        act = jax.nn.gelu(jnp.einsum('thkd,td->thk', u, xt), approximate=False)
        return jnp.einsum('thk,thkd->td', gt * act, v_tab[it])

    out = lax.map(block, (xb, idx_b, g_b))
    return out.reshape(b_, t, d)


def setup_inputs(seed: int = 0) -> dict:
    key = jax.random.key(seed)
    ks = jax.random.split(key, 23)
    f32 = jnp.float32

    def nrm(k, shape, scale):
        return jax.random.normal(k, shape, f32) * scale

    def gain(k, shape):
        return 1.0 + 0.02 * jax.random.normal(k, shape, f32)

    def dt_bias(k, shape):
        dt = jnp.exp(jax.random.uniform(k, shape, f32, math.log(1e-3), math.log(1e-1)))
        return dt + jnp.log(-jnp.expm1(-dt))

    def a_log(k, shape):
        return jnp.log(jax.random.uniform(k, shape, f32, 1.0, 16.0))

    return {
        "x": jax.random.normal(ks[0], (BATCH, SEQ, D_MODEL), f32),
        "mix_norm_w": gain(ks[1], (DEPTH, D_MODEL)),
        "w_in": nrm(ks[2], (DEPTH, D_MODEL, IN_PROJ), D_MODEL ** -0.5),
        "gate_b": nrm(ks[3], (DEPTH, N_BRANCHES, D_MODEL), 0.02),
        "ssm_conv_w": nrm(ks[4], (DEPTH, SSM_CONV, SSM_CONV_DIM), SSM_CONV ** -0.5),
        "ssm_conv_b": nrm(ks[5], (DEPTH, SSM_CONV_DIM), 0.02),
        "ssm_dt_bias": dt_bias(ks[6], (DEPTH, SSM_N_HEADS)),
        "ssm_a_log": a_log(ks[7], (DEPTH, SSM_N_HEADS)),
        "ssm_d": 1.0 + 0.1 * jax.random.normal(ks[8], (DEPTH, SSM_N_HEADS), f32),
        "ssm_norm_w": gain(ks[9], (DEPTH, SSM_D_INNER)),
        "gdn_conv_w": nrm(ks[10], (DEPTH, GDN_CONV, GDN_CONV_DIM), GDN_CONV ** -0.5),
        "gdn_dt_bias": dt_bias(ks[11], (DEPTH, GDN_N_HEADS)),
        "gdn_a_log": a_log(ks[12], (DEPTH, GDN_N_HEADS)),
        "gdn_norm_w": gain(ks[13], (DEPTH, GDN_HEAD_V)),
        "w_branch_ssm": nrm(ks[14], (DEPTH, SSM_D_INNER, D_MODEL), SSM_D_INNER ** -0.5),
        "w_branch_gdn": nrm(ks[15], (DEPTH, GDN_VAL_DIM, D_MODEL), GDN_VAL_DIM ** -0.5),
        "w_out": nrm(ks[16], (DEPTH, D_MODEL, D_MODEL), D_MODEL ** -0.5),
        "ffn_norm_w": gain(ks[17], (DEPTH, D_MODEL)),
        "peer_w_q": nrm(ks[18], (DEPTH, D_MODEL, PEER_HEADS * PEER_D_KEY), D_MODEL ** -0.5),
        "peer_sub_keys": nrm(ks[19], (DEPTH, PEER_HEADS, 2, PEER_N_KEYS, PEER_HALF), PEER_HALF ** -0.5),
        "peer_u": nrm(ks[20], (DEPTH, PEER_N_EXPERTS, D_MODEL), D_MODEL ** -0.5),
        "peer_v": nrm(ks[21], (DEPTH, PEER_N_EXPERTS, D_MODEL), PEER_HEADS ** -0.5),
        "final_norm_w": gain(ks[22], (D_MODEL,)),
    }


def reference(x, mix_norm_w, w_in, gate_b, ssm_conv_w, ssm_conv_b, ssm_dt_bias, ssm_a_log,
              ssm_d, ssm_norm_w, gdn_conv_w, gdn_dt_bias, gdn_a_log, gdn_norm_w,
              w_branch_ssm, w_branch_gdn, w_out, ffn_norm_w, peer_w_q, peer_sub_keys,
              peer_u, peer_v, final_norm_w):
    b_, t, _ = x.shape
    h = x
    for l in range(DEPTH):
        xn = _rms_norm(h, mix_norm_w[l])
        proj = xn @ w_in[l]
        p_ssm, p_gdn, p_gate = jnp.split(proj, [SSM_PROJ, SSM_PROJ + GDN_PROJ], axis=-1)
        y_ssm = _ssd_branch(p_ssm, ssm_conv_w[l], ssm_conv_b[l], ssm_dt_bias[l], ssm_a_log[l],
                            ssm_d[l], ssm_norm_w[l]) @ w_branch_ssm[l]
        y_gdn = _gdn_branch(p_gdn, gdn_conv_w[l], gdn_dt_bias[l], gdn_a_log[l],
                            gdn_norm_w[l]) @ w_branch_gdn[l]
        gates = jax.nn.sigmoid(
            (p_gate.reshape(b_, t, N_BRANCHES, D_MODEL) + gate_b[l]).astype(jnp.float32)
        ).astype(x.dtype)
        mixed = gates[:, :, 0] * y_ssm + gates[:, :, 1] * y_gdn
        h = h + mixed @ w_out[l]
        hn = _rms_norm(h, ffn_norm_w[l])
        h = h + _peer_ffn(hn, peer_w_q[l], peer_sub_keys[l], peer_u[l], peer_v[l])
    return _rms_norm(h, final_norm_w)
```

```python
import functools

import jax
import jax.numpy as jnp
from jax import lax
from jax.experimental import pallas as pl
from jax.experimental.pallas import tpu as pltpu

F32 = jnp.float32
BF16 = jnp.bfloat16
HIGHEST = lax.Precision.HIGHEST

EPS = 1e-6
D_MODEL = 2048
LANES = 128
SSM_D_INNER = 4096
SSM_HEAD_DIM = 64
SSM_N_HEADS = 64
SSM_N_GROUPS = 8
SSM_HPG = 8
SSM_D_STATE = 128
SSM_CHUNK = 128
SSM_GROUP_W = SSM_HPG * SSM_HEAD_DIM
GDN_N_HEADS = 16
GDN_HEAD_K = 128
GDN_HEAD_V = 256
GDN_CHUNK = 64
GDN_KEY_DIM = GDN_N_HEADS * GDN_HEAD_K
GDN_VAL_DIM = GDN_N_HEADS * GDN_HEAD_V
CONV_K = 4
PEER_HEADS = 8
PEER_N_KEYS = 128
PEER_TOPK = 16
PEER_N_EXPERTS = PEER_N_KEYS * PEER_N_KEYS

COL_SSM_Z = 0
COL_SSM_XBC = 4096
COL_GDN_QKV = 10240
COL_GDN_Z = 18432
COL_GATE = 22528
N_MAIN = 26624
SMALL_BETA = 64
SMALL_ALPHA = 80

VMEM_LIMIT = 56 * 1024 * 1024


def _softplus(x):
    return jnp.maximum(x, 0.0) + jnp.log1p(jnp.exp(-jnp.abs(x)))


def _silu(x):
    return x * jax.nn.sigmoid(x)


def _dot(a, b, **kw):
    return jnp.dot(a, b, preferred_element_type=F32, **kw)


def _dot_nt(a, b, **kw):
    return lax.dot_general(a, b, (((1,), (1,)), ((), ())), preferred_element_type=F32, **kw)


def _dot_tn(a, b, **kw):
    return lax.dot_general(a, b, (((0,), (0,)), ((), ())), preferred_element_type=F32, **kw)


def _params(n_grid):
    return pltpu.CompilerParams(dimension_semantics=("arbitrary",) * n_grid,
                                vmem_limit_bytes=VMEM_LIMIT)


def _inproj_kernel(x_ref, nw_ref, w_ref, ws_ref, o_ref, os_ref, xn_ref):
    @pl.when(pl.program_id(1) == 0)
    def _():
        x = x_ref[...]
        ms = jnp.mean(x * x, axis=-1, keepdims=True)
        xn = (x * lax.rsqrt(ms + EPS) * nw_ref[...]).astype(BF16)
        xn_ref[...] = xn
        os_ref[...] = _dot(xn, ws_ref[...])

    o_ref[...] = _dot(xn_ref[...], w_ref[...]).astype(BF16)


def _inproj(x, norm_w, w_main, w_small, tm=1024, tn=1024):
    t, d = x.shape
    n = w_main.shape[1]
    tm = min(tm, t)
    return pl.pallas_call(
        _inproj_kernel,
        grid=(t // tm, n // tn),
        in_specs=[
            pl.BlockSpec((tm, d), lambda i, j: (i, 0)),
            pl.BlockSpec((1, d), lambda i, j: (0, 0)),
            pl.BlockSpec((d, tn), lambda i, j: (0, j)),
            pl.BlockSpec((d, LANES), lambda i, j: (0, 0)),
        ],
        out_specs=[
            pl.BlockSpec((tm, tn), lambda i, j: (i, j)),
            pl.BlockSpec((tm, LANES), lambda i, j: (i, 0)),
        ],
        out_shape=[jax.ShapeDtypeStruct((t, n), BF16), jax.ShapeDtypeStruct((t, LANES), F32)],
        scratch_shapes=[pltpu.VMEM((tm, d), BF16)],
        compiler_params=_params(2),
        name="inproj",
    )(x, norm_w, w_main, w_small)


def _conv_block(raw_ref, buf_ref, w_ref, tc):
    buf_ref[pl.ds(8, tc), :] = raw_ref[...].astype(F32)
    y = buf_ref[pl.ds(8 - (CONV_K - 1), tc), :] * w_ref[0:1, :]
    for k in range(1, CONV_K):
        y = y + buf_ref[pl.ds(8 - (CONV_K - 1) + k, tc), :] * w_ref[k:k + 1, :]
    buf_ref[pl.ds(0, 8), :] = buf_ref[pl.ds(tc, 8), :]
    return y


def _ssd_kernel(xs_ref, b_ref, c_ref, z_ref, sm_ref, cwx_ref, cwb_ref, cwc_ref, cbx_ref, cbb_ref,
                cbc_ref, dtb_ref, alog_ref, dsk_ref, nw_ref, o_ref,
                xbuf, bbuf, cbuf, state_ref, acst_ref, *, tc):
    g = pl.program_id(0)
    L = SSM_CHUNK

    @pl.when(pl.program_id(1) == 0)
    def _():
        xbuf[pl.ds(0, 8), :] = jnp.zeros((8, SSM_GROUP_W), F32)
        bbuf[pl.ds(0, 8), :] = jnp.zeros((8, SSM_D_STATE), F32)
        cbuf[pl.ds(0, 8), :] = jnp.zeros((8, SSM_D_STATE), F32)
        state_ref[...] = jnp.zeros_like(state_ref)

    xs_all = _silu(_conv_block(xs_ref, xbuf, cwx_ref, tc) + cbx_ref[...])
    b_all = _silu(_conv_block(b_ref, bbuf, cwb_ref, tc) + cbb_ref[...])
    c_all = _silu(_conv_block(c_ref, cbuf, cwc_ref, tc) + cbc_ref[...])

    row = lax.broadcasted_iota(jnp.int32, (L, L), 0)
    col = lax.broadcasted_iota(jnp.int32, (L, L), 1)
    causal = row >= col
    tri = causal.astype(F32)
    lo_half = col < SSM_HEAD_DIM
    a_row = -jnp.exp(alog_ref[...])

    for c in range(tc // L):
        sl = slice(c * L, (c + 1) * L)
        xs, bm, cm = xs_all[sl], b_all[sl], c_all[sl]
        dt_full = _softplus(sm_ref[sl, :] + dtb_ref[...])
        acs_full = _dot(tri, dt_full * a_row, precision=HIGHEST)
        acst_ref[...] = acs_full.T
        bm16, cm16 = bm.astype(BF16), cm.astype(BF16)
        cb = _dot_nt(cm16, bm16)
        ys = []
        for p in range(SSM_HPG // 2):
            acs_bc, dt_bc, scores = [], [], []
            for r in (2 * p, 2 * p + 1):
                j = g * SSM_HPG + r
                sel = col == j
                a_c = jnp.sum(jnp.where(sel, acs_full, 0.0), axis=1, keepdims=True)
                d_c = jnp.sum(jnp.where(sel, dt_full, 0.0), axis=1, keepdims=True)
                a_b = jnp.broadcast_to(a_c, (L, L))
                acs_bc.append(a_b)
                dt_bc.append(jnp.broadcast_to(d_c, (L, L)))
                a_r = acst_ref[pl.ds(j, 1), :]
                seg = jnp.where(causal, a_b - a_r, -jnp.inf)
                scores.append((cb * jnp.exp(seg)).astype(BF16))
            acs_e = jnp.where(lo_half, acs_bc[0], acs_bc[1])
            dt_e = jnp.where(lo_half, dt_bc[0], dt_bc[1])
            lanes = slice(p * LANES, (p + 1) * LANES)
            x_p = xs[:, lanes]
            xdt = x_p * dt_e
            last = acs_e[L - 1:L, :]
            xdtw = (xdt * jnp.exp(last - acs_e)).astype(BF16)
            s2 = jnp.concatenate(scores, axis=1)
            x2 = jnp.concatenate([jnp.where(lo_half, xdt, 0.0), jnp.where(lo_half, 0.0, xdt)],
                                 axis=0).astype(BF16)
            y_diag = _dot(s2, x2)
            st = state_ref[:, lanes]
            y_off = _dot(cm16, st.astype(BF16)) * jnp.exp(acs_e)
            state_ref[:, lanes] = st * jnp.exp(last) + _dot_tn(bm16, xdtw)
            ys.append(y_diag + y_off + x_p * dsk_ref[:, lanes])
        y = jnp.concatenate(ys, axis=1) * _silu(z_ref[sl, :].astype(F32))
        y = y * lax.rsqrt(jnp.mean(y * y, axis=-1, keepdims=True) + EPS)
        o_ref[sl, :] = (y * nw_ref[...]).astype(BF16)


def _ssd(proj, small, cwx, cwb, cwc, cbx, cbb, cbc, dtb_row, alog_row, dsk_row, nw_row, tc=256):
    t = proj.shape[0]
    tc = min(tc, t)
    gw, ns = SSM_GROUP_W, SSM_D_STATE
    xs_blk = COL_SSM_XBC // gw
    b_blk = (COL_SSM_XBC + SSM_D_INNER) // ns
    c_blk = b_blk + SSM_N_GROUPS
    z_blk = COL_SSM_Z // gw
    kern = functools.partial(_ssd_kernel, tc=tc)
    return pl.pallas_call(
        kern,
        grid=(SSM_N_GROUPS, t // tc),
        in_specs=[
            pl.BlockSpec((tc, gw), lambda g, i: (i, xs_blk + g)),
            pl.BlockSpec((tc, ns), lambda g, i: (i, b_blk + g)),
            pl.BlockSpec((tc, ns), lambda g, i: (i, c_blk + g)),
            pl.BlockSpec((tc, gw), lambda g, i: (i, z_blk + g)),
            pl.BlockSpec((tc, LANES), lambda g, i: (i, 0)),
            pl.BlockSpec((CONV_K, gw), lambda g, i: (0, g)),
            pl.BlockSpec((CONV_K, ns), lambda g, i: (0, g)),
            pl.BlockSpec((CONV_K, ns), lambda g, i: (0, g)),
            pl.BlockSpec((1, gw), lambda g, i: (0, g)),
            pl.BlockSpec((1, ns), lambda g, i: (0, g)),
            pl.BlockSpec((1, ns), lambda g, i: (0, g)),
            pl.BlockSpec((1, LANES), lambda g, i: (0, 0)),
            pl.BlockSpec((1, LANES), lambda g, i: (0, 0)),
            pl.BlockSpec((1, gw), lambda g, i: (0, g)),
            pl.BlockSpec((1, gw), lambda g, i: (0, g)),
        ],
        out_specs=pl.BlockSpec((tc, gw), lambda g, i: (i, g)),
        out_shape=jax.ShapeDtypeStruct((t, SSM_D_INNER), BF16),
        scratch_shapes=[
            pltpu.VMEM((tc + 8, gw), F32),
            pltpu.VMEM((tc + 8, ns), F32),
            pltpu.VMEM((tc + 8, ns), F32),
            pltpu.VMEM((ns, gw), F32),
            pltpu.VMEM((LANES, SSM_CHUNK), F32),
        ],
        compiler_params=_params(2),
        name="ssd",
    )(proj, proj, proj, proj, small, cwx, cwb, cwc, cbx, cbb, cbc, dtb_row, alog_row, dsk_row, nw_row)


def _unit_lower_inverse(lmat, eye):
    n = -lmat
    x = eye + n
    p = n
    steps = GDN_CHUNK.bit_length() - 2
    for _ in range(steps):
        p = _dot(p, p, precision=HIGHEST)
        x = x + _dot(x, p, precision=HIGHEST)
    return x


def _gdn_kernel(q_ref, k_ref, v_ref, z_ref, sm_ref, cwq_ref, cwk_ref, cwv_ref, dtb_ref, alog_ref,
                nw_ref, o_ref, qbuf, kbuf, vbuf, s_ref, *, tc):
    h = pl.program_id(0)
    C, DK, DV = GDN_CHUNK, GDN_HEAD_K, GDN_HEAD_V

    @pl.when(pl.program_id(1) == 0)
    def _():
        qbuf[pl.ds(0, 8), :] = jnp.zeros((8, DK), F32)
        kbuf[pl.ds(0, 8), :] = jnp.zeros((8, DK), F32)
        vbuf[pl.ds(0, 8), :] = jnp.zeros((8, DV), F32)
        s_ref[...] = jnp.zeros_like(s_ref)

    q_all = _silu(_conv_block(q_ref, qbuf, cwq_ref, tc))
    k_all = _silu(_conv_block(k_ref, kbuf, cwk_ref, tc))
    v_all = _silu(_conv_block(v_ref, vbuf, cwv_ref, tc))
    q_all = q_all * lax.rsqrt(jnp.sum(q_all * q_all, axis=-1, keepdims=True) + EPS) * (DK ** -0.5)
    k_all = k_all * lax.rsqrt(jnp.sum(k_all * k_all, axis=-1, keepdims=True) + EPS)

    sm = sm_ref[...]
    lane = lax.broadcasted_iota(jnp.int32, sm.shape, 1)
    beta_all = jnp.sum(jnp.where(lane == SMALL_BETA + h, jax.nn.sigmoid(sm), 0.0),
                       axis=1, keepdims=True)
    g_full = -jnp.exp(alog_ref[...]) * _softplus(sm + dtb_ref[...])
    g_all = jnp.sum(jnp.where(lane == SMALL_ALPHA + h, g_full, 0.0), axis=1, keepdims=True)

    row = lax.broadcasted_iota(jnp.int32, (C, C), 0)
    col = lax.broadcasted_iota(jnp.int32, (C, C), 1)
    causal = row >= col
    strict = row > col
    tri = causal.astype(F32)
    eye = (row == col).astype(F32)

    pre = []
    for c in range(tc // C):
        sl = slice(c * C, (c + 1) * C)
        q, k, v = q_all[sl], k_all[sl], v_all[sl]
        beta, gg = beta_all[sl], g_all[sl]
        gc_bc = _dot(tri, jnp.broadcast_to(gg, (C, LANES)), precision=HIGHEST)
        gc_i = gc_bc[:, :C]
        gc_j = gc_bc.T[:C, :]
        decay = jnp.exp(jnp.where(causal, gc_i - gc_j, -jnp.inf))
        k16, q16 = k.astype(BF16), q.astype(BF16)
        kk = _dot_nt(k16, k16)
        lmat = jnp.where(strict, kk * beta * decay, 0.0)
        t_inv = _unit_lower_inverse(lmat, eye).astype(BF16)
        egc = jnp.exp(gc_bc)
        u = _dot(t_inv, (v * beta).astype(BF16))
        w = _dot(t_inv, (k * (beta * egc)).astype(BF16))
        attn = jnp.where(causal, _dot_nt(q16, k16) * decay, 0.0).astype(BF16)
        q_dec = (q * egc).astype(BF16)
        gc_last = gc_bc[C - 1:C, :]
        k_dec = (k * jnp.exp(gc_last - gc_bc)).astype(BF16)
        c_dec = jnp.exp(gc_last[:, :1])
        pre.append((u, w.astype(BF16), attn, q_dec, k_dec, c_dec))

    s = s_ref[...]
    for c, (u, w, attn, q_dec, k_dec, c_dec) in enumerate(pre):
        sl = slice(c * C, (c + 1) * C)
        s16 = s.astype(BF16)
        v_new = u - _dot(w, s16)
        v16 = v_new.astype(BF16)
        o = _dot(q_dec, s16) + _dot(attn, v16)
        s = s * c_dec + _dot_tn(k_dec, v16)
        o = o * lax.rsqrt(jnp.mean(o * o, axis=-1, keepdims=True) + EPS) * nw_ref[...]
        o_ref[sl, :] = (o * _silu(z_ref[sl, :].astype(F32))).astype(BF16)
    s_ref[...] = s


def _gdn(proj, small, cwq, cwk, cwv, dtb_row, alog_row, nw_row, tc=256):
    t = proj.shape[0]
    tc = min(tc, t)
    dk, dv = GDN_HEAD_K, GDN_HEAD_V
    q_blk = COL_GDN_QKV // dk
    k_blk = q_blk + GDN_N_HEADS
    v_blk = (COL_GDN_QKV + 2 * GDN_KEY_DIM) // dv
    z_blk = COL_GDN_Z // dv
    kern = functools.partial(_gdn_kernel, tc=tc)
    return pl.pallas_call(
        kern,
        grid=(GDN_N_HEADS, t // tc),
        in_specs=[
            pl.BlockSpec((tc, dk), lambda h, i: (i, q_blk + h)),
            pl.BlockSpec((tc, dk), lambda h, i: (i, k_blk + h)),
            pl.BlockSpec((tc, dv), lambda h, i: (i, v_blk + h)),
            pl.BlockSpec((tc, dv), lambda h, i: (i, z_blk + h)),
            pl.BlockSpec((tc, LANES), lambda h, i: (i, 0)),
            pl.BlockSpec((CONV_K, dk), lambda h, i: (0, h)),
            pl.BlockSpec((CONV_K, dk), lambda h, i: (0, h)),
            pl.BlockSpec((CONV_K, dv), lambda h, i: (0, h)),
            pl.BlockSpec((1, LANES), lambda h, i: (0, 0)),
            pl.BlockSpec((1, LANES), lambda h, i: (0, 0)),
            pl.BlockSpec((1, dv), lambda h, i: (0, 0)),
        ],
        out_specs=pl.BlockSpec((tc, dv), lambda h, i: (i, h)),
        out_shape=jax.ShapeDtypeStruct((t, GDN_VAL_DIM), BF16),
        scratch_shapes=[
            pltpu.VMEM((tc + 8, dk), F32),
            pltpu.VMEM((tc + 8, dk), F32),
            pltpu.VMEM((tc + 8, dv), F32),
            pltpu.VMEM((dk, dv), F32),
        ],
        compiler_params=_params(2),
        name="gdn",
    )(proj, proj, proj, proj, small, cwq, cwk, cwv, dtb_row, alog_row, nw_row)


def _mix_kernel(ys_ref, yg_ref, g0_ref, g1_ref, gb_ref, ws_ref, wg_ref, o_ref):
    a = _dot(ys_ref[...], ws_ref[...])
    b = _dot(yg_ref[...], wg_ref[...])
    g0 = jax.nn.sigmoid(g0_ref[...].astype(F32) + gb_ref[0:1, :])
    g1 = jax.nn.sigmoid(g1_ref[...].astype(F32) + gb_ref[1:2, :])
    o_ref[...] = (g0 * a + g1 * b).astype(BF16)


def _mix(y_ssm, y_gdn, proj, gate_b, w_s, w_g, tm=512, tn=512):
    t = y_ssm.shape[0]
    tm = min(tm, t)
    g0_blk = COL_GATE // tn
    g1_blk = (COL_GATE + D_MODEL) // tn
    return pl.pallas_call(
        _mix_kernel,
        grid=(D_MODEL // tn, t // tm),
        in_specs=[
            pl.BlockSpec((tm, SSM_D_INNER), lambda n, i: (i, 0)),
            pl.BlockSpec((tm, GDN_VAL_DIM), lambda n, i: (i, 0)),
            pl.BlockSpec((tm, tn), lambda n, i: (i, g0_blk + n)),
            pl.BlockSpec((tm, tn), lambda n, i: (i, g1_blk + n)),
            pl.BlockSpec((2, tn), lambda n, i: (0, n)),
            pl.BlockSpec((SSM_D_INNER, tn), lambda n, i: (0, n)),
            pl.BlockSpec((GDN_VAL_DIM, tn), lambda n, i: (0, n)),
        ],
        out_specs=pl.BlockSpec((tm, tn), lambda n, i: (i, n)),
        out_shape=jax.ShapeDtypeStruct((t, D_MODEL), BF16),
        compiler_params=_params(2),
        name="mix",
    )(y_ssm, y_gdn, proj, proj, gate_b, w_s, w_g)


_N_RANK = PEER_TOPK + 1
_CAND_PAIRS = [(i, j) for i in range(_N_RANK) for j in range(_N_RANK) if (i + 1) * (j + 1) <= _N_RANK]
_CAND_ROWS = -(-len(_CAND_PAIRS) // 8) * 8


def _top_desc(s, n):
    out = []
    for _ in range(n):
        m = jnp.max(s, axis=0, keepdims=True)
        out.append(m)
        s = jnp.where(s == m, -jnp.inf, s)
    return out


def _post_kernel(x_ref, mix_ref, wo_ref, fw_ref, wq_ref, sk_ref,
                 h1_ref, hnt_ref, s2_ref, e2_ref, th_ref, e1_ref, cand_ref, *, tm):
    h1 = x_ref[...] + _dot(mix_ref[...], wo_ref[...])
    h1_ref[...] = h1
    hn = h1 * lax.rsqrt(jnp.mean(h1 * h1, axis=-1, keepdims=True) + EPS) * fw_ref[...]
    hnt_ref[...] = hn.T.astype(BF16)
    qv = _dot(hn.astype(BF16), wq_ref[...]).astype(BF16)
    nchunk = tm // LANES
    for h in range(PEER_HEADS):
        s1 = _dot_nt(sk_ref[2 * h], qv[:, (2 * h) * LANES:(2 * h + 1) * LANES])
        s2 = _dot_nt(sk_ref[2 * h + 1], qv[:, (2 * h + 1) * LANES:(2 * h + 2) * LANES])
        a1 = _top_desc(s1, _N_RANK)
        a2 = _top_desc(s2, _N_RANK)
        cand_ref[...] = jnp.full(cand_ref.shape, -jnp.inf, F32)
        for r, (i, j) in enumerate(_CAND_PAIRS):
            cand_ref[pl.ds(r, 1), :] = a1[i] + a2[j]
        cand = cand_ref[...]
        c = cand
        n_removed = jnp.zeros((1, tm), F32)
        v16 = jnp.full((1, tm), -jnp.inf, F32)
        v17 = jnp.full((1, tm), -jnp.inf, F32)
        for _ in range(_N_RANK):
            m = jnp.max(c, axis=0, keepdims=True)
            eq = c == m
            cnt = jnp.sum(eq.astype(F32), axis=0, keepdims=True)
            v16 = jnp.where(n_removed < PEER_TOPK, m, v16)
            v17 = jnp.where(n_removed < _N_RANK, m, v17)
            n_removed = n_removed + cnt
            c = jnp.where(eq, -jnp.inf, c)
        tau = 0.5 * (v16 + v17)
        tau = jnp.where(v17 == -jnp.inf, v16, tau)
        m_tot = a1[0] + a2[0]
        z = jnp.sum(jnp.where(cand >= tau, jnp.exp(cand - m_tot), 0.0), axis=0, keepdims=True)
        e2 = jnp.exp(s2 - a2[0]) / z
        e1 = jnp.exp(s1 - a1[0])
        th = tau - s1
        for cc in range(nchunk):
            ls = slice(cc * LANES, (cc + 1) * LANES)
            s2_ref[cc, h] = s2[:, ls]
            e2_ref[cc, h] = e2[:, ls]
            th_ref[cc, h] = th[:, ls]
            e1_ref[cc, h] = e1[:, ls]


def _post(x, mix, w_out, ffn_w, w_q, sub_keys, tm=256):
    t, d = x.shape
    tm = min(tm, t)
    nchunk = tm // LANES
    stat = jax.ShapeDtypeStruct((t // LANES, PEER_HEADS, PEER_N_KEYS, LANES), F32)
    stat_spec = pl.BlockSpec((nchunk, PEER_HEADS, PEER_N_KEYS, LANES), lambda i: (i, 0, 0, 0))
    kern = functools.partial(_post_kernel, tm=tm)
    return pl.pallas_call(
        kern,
        grid=(t // tm,),
        in_specs=[
            pl.BlockSpec((tm, d), lambda i: (i, 0)),
            pl.BlockSpec((tm, d), lambda i: (i, 0)),
            pl.BlockSpec((d, d), lambda i: (0, 0)),
            pl.BlockSpec((1, d), lambda i: (0, 0)),
            pl.BlockSpec((d, d), lambda i: (0, 0)),
            pl.BlockSpec((2 * PEER_HEADS, PEER_N_KEYS, LANES), lambda i: (0, 0, 0)),
        ],
        out_specs=[
            pl.BlockSpec((tm, d), lambda i: (i, 0)),
            pl.BlockSpec((d, tm), lambda i: (0, i)),
            stat_spec, stat_spec, stat_spec, stat_spec,
        ],
        out_shape=[
            jax.ShapeDtypeStruct((t, d), F32),
            jax.ShapeDtypeStruct((d, t), BF16),
            stat, stat, stat, stat,
        ],
        scratch_shapes=[pltpu.VMEM((_CAND_ROWS, tm), F32)],
        compiler_params=_params(1),
        name="post",
    )(x, mix, w_out, ffn_w, w_q, sub_keys)


_ROW_TILE = 64


def _gelu(x):
    return 0.5 * x * (1.0 + lax.erf(x * (2.0 ** -0.5)))


def _peer_kernel(hnt_ref, u_ref, vt_ref, s2_ref, e2_ref, th_ref, e1_ref, h1_ref, fw_ref, o_ref,
                 acc_ref, pt_ref, *, tb, eb):
    e = pl.program_id(1)
    n1 = eb // PEER_N_KEYS

    @pl.when(e == 0)
    def _():
        acc_ref[...] = jnp.zeros_like(acc_ref)

    at = _dot(u_ref[...], hnt_ref[...])
    for cc in range(tb // LANES):
        ls = slice(cc * LANES, (cc + 1) * LANES)
        for rt in range(PEER_N_KEYS // _ROW_TILE):
            rs = slice(rt * _ROW_TILE, (rt + 1) * _ROW_TILE)
            w = [jnp.zeros((_ROW_TILE, LANES), F32) for _ in range(n1)]
            for h in range(PEER_HEADS):
                s2 = s2_ref[cc, h, rs, :]
                e2 = e2_ref[cc, h, rs, :]
                for i in range(n1):
                    i1 = e * n1 + i
                    th = th_ref[cc, h, pl.ds(i1, 1), :]
                    e1 = e1_ref[cc, h, pl.ds(i1, 1), :]
                    w[i] = w[i] + jnp.where(s2 >= th, e2, 0.0) * e1
            for i in range(n1):
                r0 = i * PEER_N_KEYS + rt * _ROW_TILE
                a = at[r0:r0 + _ROW_TILE, ls]
                pt_ref[r0:r0 + _ROW_TILE, ls] = (w[i] * _gelu(a)).astype(BF16)
    acc_ref[...] += _dot(vt_ref[...], pt_ref[...])

    @pl.when(e == pl.num_programs(1) - 1)
    def _():
        hf = h1_ref[...] + acc_ref[...].T
        hf = hf * lax.rsqrt(jnp.mean(hf * hf, axis=-1, keepdims=True) + EPS)
        o_ref[...] = hf * fw_ref[...]


def _peer(hnt, u16, vt16, s2, e2, th, e1, h1, final_w, tb=512, eb=512):
    d, t = hnt.shape
    tb = min(tb, t)
    nchunk = tb // LANES
    stat_spec = pl.BlockSpec((nchunk, PEER_HEADS, PEER_N_KEYS, LANES), lambda i, e: (i, 0, 0, 0))
    kern = functools.partial(_peer_kernel, tb=tb, eb=eb)
    return pl.pallas_call(
        kern,
        grid=(t // tb, PEER_N_EXPERTS // eb),
        in_specs=[
            pl.BlockSpec((d, tb), lambda i, e: (0, i)),
            pl.BlockSpec((eb, d), lambda i, e: (e, 0)),
            pl.BlockSpec((d, eb), lambda i, e: (0, e)),
            stat_spec, stat_spec, stat_spec, stat_spec,
            pl.BlockSpec((tb, d), lambda i, e: (i, 0)),
            pl.BlockSpec((1, d), lambda i, e: (0, 0)),
        ],
        out_specs=pl.BlockSpec((tb, d), lambda i, e: (i, 0)),
        out_shape=jax.ShapeDtypeStruct((t, d), F32),
        scratch_shapes=[pltpu.VMEM((d, tb), F32), pltpu.VMEM((eb, tb), BF16)],
        compiler_params=_params(2),
        name="peer",
    )(hnt, u16, vt16, s2, e2, th, e1, h1, final_w)


def _row(v, width=None):
    v = v.astype(F32).reshape(1, -1)
    if width is not None and v.shape[1] < width:
        v = jnp.pad(v, ((0, 0), (0, width - v.shape[1])))
    return v


def _layer(h, mix_norm_w, w_in, gate_b, ssm_conv_w, ssm_conv_b, ssm_dt_bias, ssm_a_log, ssm_d,
           ssm_norm_w, gdn_conv_w, gdn_dt_bias, gdn_a_log, gdn_norm_w, w_branch_ssm, w_branch_gdn,
           w_out, ffn_norm_w, peer_w_q, peer_sub_keys, peer_u, peer_v, out_norm_w):
    ssm_proj = SSM_D_INNER + (SSM_D_INNER + 2 * SSM_N_GROUPS * SSM_D_STATE) + SSM_N_HEADS
    gdn_conv_dim = 2 * GDN_KEY_DIM + GDN_VAL_DIM
    o_dt = ssm_proj - SSM_N_HEADS
    o_qkv = ssm_proj
    o_gz = o_qkv + gdn_conv_dim
    o_beta = o_gz + GDN_VAL_DIM
    o_gate = o_beta + 2 * GDN_N_HEADS
    w_main = jnp.concatenate(
        [w_in[:, :o_dt], w_in[:, o_qkv:o_beta], w_in[:, o_gate:]], axis=1).astype(BF16)
    w_small = jnp.concatenate(
        [w_in[:, o_dt:o_qkv], w_in[:, o_beta:o_gate],
         jnp.zeros((D_MODEL, LANES - SSM_N_HEADS - 2 * GDN_N_HEADS), w_in.dtype)], axis=1).astype(BF16)

    proj, small = _inproj(h, _row(mix_norm_w), w_main, w_small)

    cw = ssm_conv_w.astype(F32)
    cbias = _row(ssm_conv_b)
    nx = SSM_D_INNER
    nb = SSM_N_GROUPS * SSM_D_STATE
    y_ssm = _ssd(proj, small,
                 cw[:, :nx], cw[:, nx:nx + nb], cw[:, nx + nb:],
                 cbias[:, :nx], cbias[:, nx:nx + nb], cbias[:, nx + nb:],
                 _row(ssm_dt_bias, LANES), _row(ssm_a_log, LANES),
                 _row(jnp.repeat(ssm_d, SSM_HEAD_DIM)), _row(ssm_norm_w))

    gw = gdn_conv_w.astype(F32)
    zeros_b = jnp.zeros((SMALL_ALPHA,), F32)
    y_gdn = _gdn(proj, small,
                 gw[:, :GDN_KEY_DIM], gw[:, GDN_KEY_DIM:2 * GDN_KEY_DIM], gw[:, 2 * GDN_KEY_DIM:],
                 _row(jnp.concatenate([zeros_b, gdn_dt_bias.astype(F32)]), LANES),
                 _row(jnp.concatenate([zeros_b, gdn_a_log.astype(F32)]), LANES),
                 _row(gdn_norm_w))

    mix = _mix(y_ssm, y_gdn, proj, gate_b.astype(F32),
               w_branch_ssm.astype(BF16), w_branch_gdn.astype(BF16))

    sk = peer_sub_keys.reshape(2 * PEER_HEADS, PEER_N_KEYS, LANES).astype(BF16)
    h1, hnt, s2, e2, th, e1 = _post(h, mix, w_out.astype(BF16), _row(ffn_norm_w),
                                    peer_w_q.astype(BF16), sk)

    return _peer(hnt, peer_u.astype(BF16), peer_v.T.astype(BF16), s2, e2, th, e1, h1,
                 _row(out_norm_w))


def kernel(x, mix_norm_w, w_in, gate_b, ssm_conv_w, ssm_conv_b, ssm_dt_bias, ssm_a_log, ssm_d,
           ssm_norm_w, gdn_conv_w, gdn_dt_bias, gdn_a_log, gdn_norm_w, w_branch_ssm, w_branch_gdn,
           w_out, ffn_norm_w, peer_w_q, peer_sub_keys, peer_u, peer_v, final_norm_w):
    b, t, d = x.shape
    assert b == 1 and mix_norm_w.shape[0] == 1, "single sequence, single layer"
    out = _layer(x[0], mix_norm_w[0], w_in[0], gate_b[0], ssm_conv_w[0], ssm_conv_b[0],
                 ssm_dt_bias[0], ssm_a_log[0], ssm_d[0], ssm_norm_w[0], gdn_conv_w[0],
                 gdn_dt_bias[0], gdn_a_log[0], gdn_norm_w[0], w_branch_ssm[0], w_branch_gdn[0],
                 w_out[0], ffn_norm_w[0], peer_w_q[0], peer_sub_keys[0], peer_u[0], peer_v[0],
                 final_norm_w)
    return out[None]
```

```python
import functools

import jax
import jax.numpy as jnp
from jax import lax
from jax.experimental import pallas as pl
from jax.experimental.pallas import tpu as pltpu

F32 = jnp.float32
BF16 = jnp.bfloat16
HIGHEST = lax.Precision.HIGHEST

EPS = 1e-6
D_MODEL = 2048
LANES = 128
SSM_D_INNER = 4096
SSM_HEAD_DIM = 64
SSM_N_HEADS = 64
SSM_N_GROUPS = 8
SSM_HPG = 8
SSM_D_STATE = 128
SSM_CHUNK = 128
SSM_GROUP_W = SSM_HPG * SSM_HEAD_DIM
GDN_N_HEADS = 16
GDN_HEAD_K = 128
GDN_HEAD_V = 256
GDN_CHUNK = 64
GDN_KEY_DIM = GDN_N_HEADS * GDN_HEAD_K
GDN_VAL_DIM = GDN_N_HEADS * GDN_HEAD_V
CONV_K = 4
PEER_HEADS = 8
PEER_N_KEYS = 128
PEER_TOPK = 16
PEER_N_EXPERTS = PEER_N_KEYS * PEER_N_KEYS

COL_SSM_Z = 0
COL_SSM_XBC = 4096
COL_GDN_QKV = 10240
COL_GDN_Z = 18432
COL_GATE = 22528
N_MAIN = 26624
SMALL_BETA = 64
SMALL_ALPHA = 80

VMEM_LIMIT = 56 * 1024 * 1024


def _softplus(x):
    return jnp.maximum(x, 0.0) + jnp.log1p(jnp.exp(-jnp.abs(x)))


def _silu(x):
    return x * jax.nn.sigmoid(x)


def _dot(a, b, **kw):
    return jnp.dot(a, b, preferred_element_type=F32, **kw)


def _dot_nt(a, b, **kw):
    return lax.dot_general(a, b, (((1,), (1,)), ((), ())), preferred_element_type=F32, **kw)


def _dot_tn(a, b, **kw):
    return lax.dot_general(a, b, (((0,), (0,)), ((), ())), preferred_element_type=F32, **kw)


def _params(n_grid):
    return pltpu.CompilerParams(dimension_semantics=("arbitrary",) * n_grid,
                                vmem_limit_bytes=VMEM_LIMIT)


def _inproj_kernel(x_ref, nw_ref, w_ref, ws_ref, o_ref, os_ref, xn_ref):
    @pl.when(pl.program_id(1) == 0)
    def _():
        x = x_ref[...]
        ms = jnp.mean(x * x, axis=-1, keepdims=True)
        xn = (x * lax.rsqrt(ms + EPS) * nw_ref[...]).astype(BF16)
        xn_ref[...] = xn
        os_ref[...] = _dot(xn, ws_ref[...])

    o_ref[...] = _dot(xn_ref[...], w_ref[...]).astype(BF16)


def _inproj(x, norm_w, w_main, w_small, tm=1024, tn=1024):
    t, d = x.shape
    n = w_main.shape[1]
    tm = min(tm, t)
    return pl.pallas_call(
        _inproj_kernel,
        grid=(t // tm, n // tn),
        in_specs=[
            pl.BlockSpec((tm, d), lambda i, j: (i, 0)),
            pl.BlockSpec((1, d), lambda i, j: (0, 0)),
            pl.BlockSpec((d, tn), lambda i, j: (0, j)),
            pl.BlockSpec((d, LANES), lambda i, j: (0, 0)),
        ],
        out_specs=[
            pl.BlockSpec((tm, tn), lambda i, j: (i, j)),
            pl.BlockSpec((tm, LANES), lambda i, j: (i, 0)),
        ],
        out_shape=[jax.ShapeDtypeStruct((t, n), BF16), jax.ShapeDtypeStruct((t, LANES), F32)],
        scratch_shapes=[pltpu.VMEM((tm, d), BF16)],
        compiler_params=_params(2),
        name="inproj",
    )(x, norm_w, w_main, w_small)


def _conv_block(raw_ref, buf_ref, w_ref, tc):
    buf_ref[pl.ds(8, tc), :] = raw_ref[...].astype(F32)
    y = buf_ref[pl.ds(8 - (CONV_K - 1), tc), :] * w_ref[0:1, :]
    for k in range(1, CONV_K):
        y = y + buf_ref[pl.ds(8 - (CONV_K - 1) + k, tc), :] * w_ref[k:k + 1, :]
    buf_ref[pl.ds(0, 8), :] = buf_ref[pl.ds(tc, 8), :]
    return y


def _ssd_kernel(xs_ref, b_ref, c_ref, z_ref, sm_ref, cwx_ref, cwb_ref, cwc_ref, cbx_ref, cbb_ref,
                cbc_ref, dtb_ref, alog_ref, dsk_ref, nw_ref, o_ref,
                xbuf, bbuf, cbuf, state_ref, acst_ref, *, tc):
    g = pl.program_id(0)
    L = SSM_CHUNK

    @pl.when(pl.program_id(1) == 0)
    def _():
        xbuf[pl.ds(0, 8), :] = jnp.zeros((8, SSM_GROUP_W), F32)
        bbuf[pl.ds(0, 8), :] = jnp.zeros((8, SSM_D_STATE), F32)
        cbuf[pl.ds(0, 8), :] = jnp.zeros((8, SSM_D_STATE), F32)
        state_ref[...] = jnp.zeros_like(state_ref)

    xs_all = _silu(_conv_block(xs_ref, xbuf, cwx_ref, tc) + cbx_ref[...])
    b_all = _silu(_conv_block(b_ref, bbuf, cwb_ref, tc) + cbb_ref[...])
    c_all = _silu(_conv_block(c_ref, cbuf, cwc_ref, tc) + cbc_ref[...])

    row = lax.broadcasted_iota(jnp.int32, (L, L), 0)
    col = lax.broadcasted_iota(jnp.int32, (L, L), 1)
    causal = row >= col
    tri = causal.astype(F32)
    lo_half = col < SSM_HEAD_DIM
    a_row = -jnp.exp(alog_ref[...])

    for c in range(tc // L):
        sl = slice(c * L, (c + 1) * L)
        xs, bm, cm = xs_all[sl], b_all[sl], c_all[sl]
        dt_full = _softplus(sm_ref[sl, :] + dtb_ref[...])
        acs_full = _dot(tri, dt_full * a_row, precision=HIGHEST)
        acst_ref[...] = acs_full.T
        bm16, cm16 = bm.astype(BF16), cm.astype(BF16)
        cb = _dot_nt(cm16, bm16)
        ys = []
        for p in range(SSM_HPG // 2):
            acs_bc, dt_bc, scores = [], [], []
            for r in (2 * p, 2 * p + 1):
                j = g * SSM_HPG + r
                sel = col == j
                a_c = jnp.sum(jnp.where(sel, acs_full, 0.0), axis=1, keepdims=True)
                d_c = jnp.sum(jnp.where(sel, dt_full, 0.0), axis=1, keepdims=True)
                a_b = jnp.broadcast_to(a_c, (L, L))
                acs_bc.append(a_b)
                dt_bc.append(jnp.broadcast_to(d_c, (L, L)))
                a_r = acst_ref[pl.ds(j, 1), :]
                seg = jnp.where(causal, a_b - a_r, -jnp.inf)
                scores.append((cb * jnp.exp(seg)).astype(BF16))
            acs_e = jnp.where(lo_half, acs_bc[0], acs_bc[1])
            dt_e = jnp.where(lo_half, dt_bc[0], dt_bc[1])
            lanes = slice(p * LANES, (p + 1) * LANES)
            x_p = xs[:, lanes]
            xdt = x_p * dt_e
            last = acs_e[L - 1:L, :]
            xdtw = (xdt * jnp.exp(last - acs_e)).astype(BF16)
            s2 = jnp.concatenate(scores, axis=1)
            x2 = jnp.concatenate([jnp.where(lo_half, xdt, 0.0), jnp.where(lo_half, 0.0, xdt)],
                                 axis=0).astype(BF16)
            y_diag = _dot(s2, x2)
            st = state_ref[:, lanes]
            y_off = _dot(cm16, st.astype(BF16)) * jnp.exp(acs_e)
            state_ref[:, lanes] = st * jnp.exp(last) + _dot_tn(bm16, xdtw)
            ys.append(y_diag + y_off + x_p * dsk_ref[:, lanes])
        y = jnp.concatenate(ys, axis=1) * _silu(z_ref[sl, :].astype(F32))
        y = y * lax.rsqrt(jnp.mean(y * y, axis=-1, keepdims=True) + EPS)
        o_ref[sl, :] = (y * nw_ref[...]).astype(BF16)


def _ssd(proj, small, cwx, cwb, cwc, cbx, cbb, cbc, dtb_row, alog_row, dsk_row, nw_row, tc=256):
    t = proj.shape[0]
    tc = min(tc, t)
    gw, ns = SSM_GROUP_W, SSM_D_STATE
    xs_blk = COL_SSM_XBC // gw
    b_blk = (COL_SSM_XBC + SSM_D_INNER) // ns
    c_blk = b_blk + SSM_N_GROUPS
    z_blk = COL_SSM_Z // gw
    kern = functools.partial(_ssd_kernel, tc=tc)
    return pl.pallas_call(
        kern,
        grid=(SSM_N_GROUPS, t // tc),
        in_specs=[
            pl.BlockSpec((tc, gw), lambda g, i: (i, xs_blk + g)),
            pl.BlockSpec((tc, ns), lambda g, i: (i, b_blk + g)),
            pl.BlockSpec((tc, ns), lambda g, i: (i, c_blk + g)),
            pl.BlockSpec((tc, gw), lambda g, i: (i, z_blk + g)),
            pl.BlockSpec((tc, LANES), lambda g, i: (i, 0)),
            pl.BlockSpec((CONV_K, gw), lambda g, i: (0, g)),
            pl.BlockSpec((CONV_K, ns), lambda g, i: (0, g)),
            pl.BlockSpec((CONV_K, ns), lambda g, i: (0, g)),
            pl.BlockSpec((1, gw), lambda g, i: (0, g)),
            pl.BlockSpec((1, ns), lambda g, i: (0, g)),
            pl.BlockSpec((1, ns), lambda g, i: (0, g)),
            pl.BlockSpec((1, LANES), lambda g, i: (0, 0)),
            pl.BlockSpec((1, LANES), lambda g, i: (0, 0)),
            pl.BlockSpec((1, gw), lambda g, i: (0, g)),
            pl.BlockSpec((1, gw), lambda g, i: (0, g)),
        ],
        out_specs=pl.BlockSpec((tc, gw), lambda g, i: (i, g)),
        out_shape=jax.ShapeDtypeStruct((t, SSM_D_INNER), BF16),
        scratch_shapes=[
            pltpu.VMEM((tc + 8, gw), F32),
            pltpu.VMEM((tc + 8, ns), F32),
            pltpu.VMEM((tc + 8, ns), F32),
            pltpu.VMEM((ns, gw), F32),
            pltpu.VMEM((LANES, SSM_CHUNK), F32),
        ],
        compiler_params=_params(2),
        name="ssd",
    )(proj, proj, proj, proj, small, cwx, cwb, cwc, cbx, cbb, cbc, dtb_row, alog_row, dsk_row, nw_row)


def _gdn_kernel(q_ref, k_ref, v_ref, z_ref, sm_ref, cwq_ref, cwk_ref, cwv_ref, dtb_ref, alog_ref,
                nw_ref, o_ref, qbuf, kbuf, vbuf, s_ref, *, tc, hps):
    h0 = pl.program_id(0) * hps
    C, DK, DV = GDN_CHUNK, GDN_HEAD_K, GDN_HEAD_V

    @pl.when(pl.program_id(1) == 0)
    def _():
        qbuf[pl.ds(0, 8), :] = jnp.zeros((8, hps * DK), F32)
        kbuf[pl.ds(0, 8), :] = jnp.zeros((8, hps * DK), F32)
        vbuf[pl.ds(0, 8), :] = jnp.zeros((8, hps * DV), F32)
        s_ref[...] = jnp.zeros_like(s_ref)

    q_cv = _silu(_conv_block(q_ref, qbuf, cwq_ref, tc))
    k_cv = _silu(_conv_block(k_ref, kbuf, cwk_ref, tc))
    v_cv = _silu(_conv_block(v_ref, vbuf, cwv_ref, tc))

    sm = sm_ref[...]
    lane = lax.broadcasted_iota(jnp.int32, sm.shape, 1)
    beta_full = jax.nn.sigmoid(sm)
    g_full = -jnp.exp(alog_ref[...]) * _softplus(sm + dtb_ref[...])
    q_hd, k_hd, v_hd, beta_hd, g_hd = [], [], [], [], []
    for hd in range(hps):
        q = q_cv[:, hd * DK:(hd + 1) * DK]
        k = k_cv[:, hd * DK:(hd + 1) * DK]
        q_hd.append(q * lax.rsqrt(jnp.sum(q * q, axis=-1, keepdims=True) + EPS) * (DK ** -0.5))
        k_hd.append(k * lax.rsqrt(jnp.sum(k * k, axis=-1, keepdims=True) + EPS))
        v_hd.append(v_cv[:, hd * DV:(hd + 1) * DV])
        beta_hd.append(jnp.sum(jnp.where(lane == SMALL_BETA + h0 + hd, beta_full, 0.0),
                               axis=1, keepdims=True))
        g_hd.append(jnp.sum(jnp.where(lane == SMALL_ALPHA + h0 + hd, g_full, 0.0),
                            axis=1, keepdims=True))

    row = lax.broadcasted_iota(jnp.int32, (C, C), 0)
    col = lax.broadcasted_iota(jnp.int32, (C, C), 1)
    causal = row >= col
    strict = row > col
    tri = causal.astype(F32)
    eye = (row == col).astype(F32)

    nc = tc // C
    items = [(hd, slice(c * C, (c + 1) * C)) for hd in range(hps) for c in range(nc)]
    rng = range(len(items))
    strict_w = (lax.broadcasted_iota(jnp.int32, (C, LANES), 0)
                > lax.broadcasted_iota(jnp.int32, (C, LANES), 1))
    qs = [q_hd[hd][sl] for hd, sl in items]
    ks = [k_hd[hd][sl] for hd, sl in items]
    vs = [v_hd[hd][sl] for hd, sl in items]
    betas = [beta_hd[hd][sl] for hd, sl in items]
    g_bc = [jnp.broadcast_to(g_hd[hd][sl], (C, LANES)) for hd, sl in items]
    cums = [_dot(tri, jnp.concatenate([g_bc[c], jnp.where(strict_w, g_bc[c], 0.0)], axis=1),
                 precision=HIGHEST) for c in rng]
    gc_bc = [x[:, :LANES] for x in cums]
    decay = [jnp.exp(jnp.where(causal, x[:, LANES:LANES + C], -jnp.inf)) for x in cums]
    k16 = [k.astype(BF16) for k in ks]
    q16 = [q.astype(BF16) for q in qs]
    kk = [_dot_nt(k16[c], k16[c]) for c in rng]
    qk = [_dot_nt(q16[c], k16[c]) for c in rng]
    p = [-jnp.where(strict, kk[c] * betas[c] * decay[c], 0.0) for c in rng]
    x = [eye + p[c] for c in rng]
    for _ in range(C.bit_length() - 2):
        p16 = [pp.astype(BF16) for pp in p]
        p = [_dot(p16[c], p16[c]) for c in rng]
        x = [x[c] + _dot(x[c].astype(BF16), p[c].astype(BF16)) for c in rng]
    t_inv = [xx.astype(BF16) for xx in x]
    egc = [jnp.exp(gc) for gc in gc_bc]
    u = [_dot(t_inv[c], (vs[c] * betas[c]).astype(BF16)).astype(BF16) for c in rng]
    w = [_dot(t_inv[c], (ks[c] * (betas[c] * egc[c])).astype(BF16)).astype(BF16) for c in rng]
    attn = [jnp.where(causal, qk[c] * decay[c], 0.0).astype(BF16) for c in rng]
    gc_last = [gc[C - 1:C, :] for gc in gc_bc]
    k_dec = [(ks[c] * jnp.exp(gc_last[c] - gc_bc[c])).astype(BF16) for c in rng]
    c_dec = [jnp.exp(gl[:, :1]) for gl in gc_last]
    a_mat = [_dot_tn(k_dec[c], w[c]).astype(BF16) for c in rng]
    n_mat = [_dot_tn(k_dec[c], u[c]) for c in rng]
    q_eff = [(qs[c] * egc[c] - _dot(attn[c], w[c])).astype(BF16) for c in rng]
    o_u = [_dot(attn[c], u[c]) for c in rng]

    s = [s_ref[hd] for hd in range(hps)]
    for c in range(nc):
        sl = slice(c * C, (c + 1) * C)
        for hd in range(hps):
            it = hd * nc + c
            s16 = s[hd].astype(BF16)
            o = _dot(q_eff[it], s16) + o_u[it]
            s[hd] = s[hd] * c_dec[it] - _dot(a_mat[it], s16) + n_mat[it]
            o = o * lax.rsqrt(jnp.mean(o * o, axis=-1, keepdims=True) + EPS) * nw_ref[...]
            z = z_ref[sl, hd * DV:(hd + 1) * DV].astype(F32)
            o_ref[sl, hd * DV:(hd + 1) * DV] = (o * _silu(z)).astype(BF16)
    for hd in range(hps):
        s_ref[hd] = s[hd]


def _gdn(proj, small, cwq, cwk, cwv, dtb_row, alog_row, nw_row, tc=256, hps=2):
    t = proj.shape[0]
    tc = min(tc, t)
    dk, dv = hps * GDN_HEAD_K, hps * GDN_HEAD_V
    q_blk = COL_GDN_QKV // dk
    k_blk = q_blk + GDN_N_HEADS // hps
    v_blk = (COL_GDN_QKV + 2 * GDN_KEY_DIM) // dv
    z_blk = COL_GDN_Z // dv
    kern = functools.partial(_gdn_kernel, tc=tc, hps=hps)
    return pl.pallas_call(
        kern,
        grid=(GDN_N_HEADS // hps, t // tc),
        in_specs=[
            pl.BlockSpec((tc, dk), lambda h, i: (i, q_blk + h)),
            pl.BlockSpec((tc, dk), lambda h, i: (i, k_blk + h)),
            pl.BlockSpec((tc, dv), lambda h, i: (i, v_blk + h)),
            pl.BlockSpec((tc, dv), lambda h, i: (i, z_blk + h)),
            pl.BlockSpec((tc, LANES), lambda h, i: (i, 0)),
            pl.BlockSpec((CONV_K, dk), lambda h, i: (0, h)),
            pl.BlockSpec((CONV_K, dk), lambda h, i: (0, h)),
            pl.BlockSpec((CONV_K, dv), lambda h, i: (0, h)),
            pl.BlockSpec((1, LANES), lambda h, i: (0, 0)),
            pl.BlockSpec((1, LANES), lambda h, i: (0, 0)),
            pl.BlockSpec((1, GDN_HEAD_V), lambda h, i: (0, 0)),
        ],
        out_specs=pl.BlockSpec((tc, dv), lambda h, i: (i, h)),
        out_shape=jax.ShapeDtypeStruct((t, GDN_VAL_DIM), BF16),
        scratch_shapes=[
            pltpu.VMEM((tc + 8, dk), F32),
            pltpu.VMEM((tc + 8, dk), F32),
            pltpu.VMEM((tc + 8, dv), F32),
            pltpu.VMEM((hps, GDN_HEAD_K, GDN_HEAD_V), F32),
        ],
        compiler_params=_params(2),
        name="gdn",
    )(proj, proj, proj, proj, small, cwq, cwk, cwv, dtb_row, alog_row, nw_row)


def _mix_kernel(ys_ref, yg_ref, g0_ref, g1_ref, gb_ref, ws_ref, wg_ref, o_ref):
    a = _dot(ys_ref[...], ws_ref[...])
    b = _dot(yg_ref[...], wg_ref[...])
    g0 = jax.nn.sigmoid(g0_ref[...].astype(F32) + gb_ref[0:1, :])
    g1 = jax.nn.sigmoid(g1_ref[...].astype(F32) + gb_ref[1:2, :])
    o_ref[...] = (g0 * a + g1 * b).astype(BF16)


def _mix(y_ssm, y_gdn, proj, gate_b, w_s, w_g, tm=512, tn=512):
    t = y_ssm.shape[0]
    tm = min(tm, t)
    g0_blk = COL_GATE // tn
    g1_blk = (COL_GATE + D_MODEL) // tn
    return pl.pallas_call(
        _mix_kernel,
        grid=(D_MODEL // tn, t // tm),
        in_specs=[
            pl.BlockSpec((tm, SSM_D_INNER), lambda n, i: (i, 0)),
            pl.BlockSpec((tm, GDN_VAL_DIM), lambda n, i: (i, 0)),
            pl.BlockSpec((tm, tn), lambda n, i: (i, g0_blk + n)),
            pl.BlockSpec((tm, tn), lambda n, i: (i, g1_blk + n)),
            pl.BlockSpec((2, tn), lambda n, i: (0, n)),
            pl.BlockSpec((SSM_D_INNER, tn), lambda n, i: (0, n)),
            pl.BlockSpec((GDN_VAL_DIM, tn), lambda n, i: (0, n)),
        ],
        out_specs=pl.BlockSpec((tm, tn), lambda n, i: (i, n)),
        out_shape=jax.ShapeDtypeStruct((t, D_MODEL), BF16),
        compiler_params=_params(2),
        name="mix",
    )(y_ssm, y_gdn, proj, proj, gate_b, w_s, w_g)


_N_RANK = PEER_TOPK + 1
_CAND_PAIRS = [(i, j) for i in range(_N_RANK) for j in range(_N_RANK) if (i + 1) * (j + 1) <= _N_RANK]
_CAND_ROWS = -(-len(_CAND_PAIRS) // 8) * 8


def _top_desc(s, n):
    out = []
    for _ in range(n):
        m = jnp.max(s, axis=0, keepdims=True)
        out.append(m)
        s = jnp.where(s == m, -jnp.inf, s)
    return out


def _post_kernel(x_ref, mix_ref, wo_ref, fw_ref, wq_ref, sk_ref,
                 h1_ref, hnt_ref, s2_ref, e2_ref, th_ref, e1_ref, cand_ref, *, tm):
    h1 = x_ref[...] + _dot(mix_ref[...], wo_ref[...])
    h1_ref[...] = h1
    hn = h1 * lax.rsqrt(jnp.mean(h1 * h1, axis=-1, keepdims=True) + EPS) * fw_ref[...]
    hnt_ref[...] = hn.T.astype(BF16)
    qv = _dot(hn.astype(BF16), wq_ref[...]).astype(BF16)
    nchunk = tm // LANES
    for h in range(PEER_HEADS):
        s1 = _dot_nt(sk_ref[2 * h], qv[:, (2 * h) * LANES:(2 * h + 1) * LANES])
        s2 = _dot_nt(sk_ref[2 * h + 1], qv[:, (2 * h + 1) * LANES:(2 * h + 2) * LANES])
        a1 = _top_desc(s1, _N_RANK)
        a2 = _top_desc(s2, _N_RANK)
        cand_ref[...] = jnp.full(cand_ref.shape, -jnp.inf, F32)
        for r, (i, j) in enumerate(_CAND_PAIRS):
            cand_ref[pl.ds(r, 1), :] = a1[i] + a2[j]
        cand = cand_ref[...]
        c = cand
        n_removed = jnp.zeros((1, tm), F32)
        v16 = jnp.full((1, tm), -jnp.inf, F32)
        v17 = jnp.full((1, tm), -jnp.inf, F32)
        for _ in range(_N_RANK):
            m = jnp.max(c, axis=0, keepdims=True)
            eq = c == m
            cnt = jnp.sum(eq.astype(F32), axis=0, keepdims=True)
            v16 = jnp.where(n_removed < PEER_TOPK, m, v16)
            v17 = jnp.where(n_removed < _N_RANK, m, v17)
            n_removed = n_removed + cnt
            c = jnp.where(eq, -jnp.inf, c)
        tau = 0.5 * (v16 + v17)
        tau = jnp.where(v17 == -jnp.inf, v16, tau)
        m_tot = a1[0] + a2[0]
        z = jnp.sum(jnp.where(cand >= tau, jnp.exp(cand - m_tot), 0.0), axis=0, keepdims=True)
        e2 = jnp.exp(s2 - a2[0]) / z
        e1 = jnp.exp(s1 - a1[0])
        th = tau - s1
        for cc in range(nchunk):
            ls = slice(cc * LANES, (cc + 1) * LANES)
            s2_ref[cc, h] = s2[:, ls]
            e2_ref[cc, h] = e2[:, ls]
            th_ref[cc, h] = th[:, ls]
            e1_ref[cc, h] = e1[:, ls]


def _post(x, mix, w_out, ffn_w, w_q, sub_keys, tm=256):
    t, d = x.shape
    tm = min(tm, t)
    nchunk = tm // LANES
    stat = jax.ShapeDtypeStruct((t // LANES, PEER_HEADS, PEER_N_KEYS, LANES), F32)
    stat_spec = pl.BlockSpec((nchunk, PEER_HEADS, PEER_N_KEYS, LANES), lambda i: (i, 0, 0, 0))
    kern = functools.partial(_post_kernel, tm=tm)
    return pl.pallas_call(
        kern,
        grid=(t // tm,),
        in_specs=[
            pl.BlockSpec((tm, d), lambda i: (i, 0)),
            pl.BlockSpec((tm, d), lambda i: (i, 0)),
            pl.BlockSpec((d, d), lambda i: (0, 0)),
            pl.BlockSpec((1, d), lambda i: (0, 0)),
            pl.BlockSpec((d, d), lambda i: (0, 0)),
            pl.BlockSpec((2 * PEER_HEADS, PEER_N_KEYS, LANES), lambda i: (0, 0, 0)),
        ],
        out_specs=[
            pl.BlockSpec((tm, d), lambda i: (i, 0)),
            pl.BlockSpec((d, tm), lambda i: (0, i)),
            stat_spec, stat_spec, stat_spec, stat_spec,
        ],
        out_shape=[
            jax.ShapeDtypeStruct((t, d), F32),
            jax.ShapeDtypeStruct((d, t), BF16),
            stat, stat, stat, stat,
        ],
        scratch_shapes=[pltpu.VMEM((_CAND_ROWS, tm), F32)],
        compiler_params=_params(1),
        name="post",
    )(x, mix, w_out, ffn_w, w_q, sub_keys)


_ROW_TILE = 64


def _gelu(x):
    return 0.5 * x * (1.0 + lax.erf(x * (2.0 ** -0.5)))


def _peer_kernel(hnt_ref, u_ref, vt_ref, s2_ref, e2_ref, th_ref, e1_ref, h1_ref, fw_ref, o_ref,
                 acc_ref, pt_ref, *, tb, eb):
    e = pl.program_id(1)
    n1 = eb // PEER_N_KEYS

    @pl.when(e == 0)
    def _():
        acc_ref[...] = jnp.zeros_like(acc_ref)

    at = _dot(u_ref[...], hnt_ref[...])
    for cc in range(tb // LANES):
        ls = slice(cc * LANES, (cc + 1) * LANES)
        for rt in range(PEER_N_KEYS // _ROW_TILE):
            rs = slice(rt * _ROW_TILE, (rt + 1) * _ROW_TILE)
            w = [jnp.zeros((_ROW_TILE, LANES), F32) for _ in range(n1)]
            for h in range(PEER_HEADS):
                s2 = s2_ref[cc, h, rs, :]
                e2 = e2_ref[cc, h, rs, :]
                for i in range(n1):
                    i1 = e * n1 + i
                    th = th_ref[cc, h, pl.ds(i1, 1), :]
                    e1 = e1_ref[cc, h, pl.ds(i1, 1), :]
                    w[i] = w[i] + jnp.where(s2 >= th, e2, 0.0) * e1
            for i in range(n1):
                r0 = i * PEER_N_KEYS + rt * _ROW_TILE
                a = at[r0:r0 + _ROW_TILE, ls]
                pt_ref[r0:r0 + _ROW_TILE, ls] = (w[i] * _gelu(a)).astype(BF16)
    acc_ref[...] += _dot(vt_ref[...], pt_ref[...])

    @pl.when(e == pl.num_programs(1) - 1)
    def _():
        hf = h1_ref[...] + acc_ref[...].T
        hf = hf * lax.rsqrt(jnp.mean(hf * hf, axis=-1, keepdims=True) + EPS)
        o_ref[...] = hf * fw_ref[...]


def _peer(hnt, u16, vt16, s2, e2, th, e1, h1, final_w, tb=512, eb=512):
    d, t = hnt.shape
    tb = min(tb, t)
    nchunk = tb // LANES
    stat_spec = pl.BlockSpec((nchunk, PEER_HEADS, PEER_N_KEYS, LANES), lambda i, e: (i, 0, 0, 0))
    kern = functools.partial(_peer_kernel, tb=tb, eb=eb)
    return pl.pallas_call(
        kern,
        grid=(t // tb, PEER_N_EXPERTS // eb),
        in_specs=[
            pl.BlockSpec((d, tb), lambda i, e: (0, i)),
            pl.BlockSpec((eb, d), lambda i, e: (e, 0)),
            pl.BlockSpec((d, eb), lambda i, e: (0, e)),
            stat_spec, stat_spec, stat_spec, stat_spec,
            pl.BlockSpec((tb, d), lambda i, e: (i, 0)),
            pl.BlockSpec((1, d), lambda i, e: (0, 0)),
        ],
        out_specs=pl.BlockSpec((tb, d), lambda i, e: (i, 0)),
        out_shape=jax.ShapeDtypeStruct((t, d), F32),
        scratch_shapes=[pltpu.VMEM((d, tb), F32), pltpu.VMEM((eb, tb), BF16)],
        compiler_params=_params(2),
        name="peer",
    )(hnt, u16, vt16, s2, e2, th, e1, h1, final_w)


def _row(v, width=None):
    v = v.astype(F32).reshape(1, -1)
    if width is not None and v.shape[1] < width:
        v = jnp.pad(v, ((0, 0), (0, width - v.shape[1])))
    return v


def _layer(h, mix_norm_w, w_in, gate_b, ssm_conv_w, ssm_conv_b, ssm_dt_bias, ssm_a_log, ssm_d,
           ssm_norm_w, gdn_conv_w, gdn_dt_bias, gdn_a_log, gdn_norm_w, w_branch_ssm, w_branch_gdn,
           w_out, ffn_norm_w, peer_w_q, peer_sub_keys, peer_u, peer_v, out_norm_w):
    ssm_proj = SSM_D_INNER + (SSM_D_INNER + 2 * SSM_N_GROUPS * SSM_D_STATE) + SSM_N_HEADS
    gdn_conv_dim = 2 * GDN_KEY_DIM + GDN_VAL_DIM
    o_dt = ssm_proj - SSM_N_HEADS
    o_qkv = ssm_proj
    o_gz = o_qkv + gdn_conv_dim
    o_beta = o_gz + GDN_VAL_DIM
    o_gate = o_beta + 2 * GDN_N_HEADS
    w_main = jnp.concatenate(
        [w_in[:, :o_dt], w_in[:, o_qkv:o_beta], w_in[:, o_gate:]], axis=1).astype(BF16)
    w_small = jnp.concatenate(
        [w_in[:, o_dt:o_qkv], w_in[:, o_beta:o_gate],
         jnp.zeros((D_MODEL, LANES - SSM_N_HEADS - 2 * GDN_N_HEADS), w_in.dtype)], axis=1).astype(BF16)

    proj, small = _inproj(h, _row(mix_norm_w), w_main, w_small)

    cw = ssm_conv_w.astype(F32)
    cbias = _row(ssm_conv_b)
    nx = SSM_D_INNER
    nb = SSM_N_GROUPS * SSM_D_STATE
    y_ssm = _ssd(proj, small,
                 cw[:, :nx], cw[:, nx:nx + nb], cw[:, nx + nb:],
                 cbias[:, :nx], cbias[:, nx:nx + nb], cbias[:, nx + nb:],
                 _row(ssm_dt_bias, LANES), _row(ssm_a_log, LANES),
                 _row(jnp.repeat(ssm_d, SSM_HEAD_DIM)), _row(ssm_norm_w))

    gw = gdn_conv_w.astype(F32)
    zeros_b = jnp.zeros((SMALL_ALPHA,), F32)
    y_gdn = _gdn(proj, small,
                 gw[:, :GDN_KEY_DIM], gw[:, GDN_KEY_DIM:2 * GDN_KEY_DIM], gw[:, 2 * GDN_KEY_DIM:],
                 _row(jnp.concatenate([zeros_b, gdn_dt_bias.astype(F32)]), LANES),
                 _row(jnp.concatenate([zeros_b, gdn_a_log.astype(F32)]), LANES),
                 _row(gdn_norm_w))

    mix = _mix(y_ssm, y_gdn, proj, gate_b.astype(F32),
               w_branch_ssm.astype(BF16), w_branch_gdn.astype(BF16))

    sk = peer_sub_keys.reshape(2 * PEER_HEADS, PEER_N_KEYS, LANES).astype(BF16)
    h1, hnt, s2, e2, th, e1 = _post(h, mix, w_out.astype(BF16), _row(ffn_norm_w),
                                    peer_w_q.astype(BF16), sk)

    return _peer(hnt, peer_u.astype(BF16), peer_v.T.astype(BF16), s2, e2, th, e1, h1,
                 _row(out_norm_w))


def kernel(x, mix_norm_w, w_in, gate_b, ssm_conv_w, ssm_conv_b, ssm_dt_bias, ssm_a_log, ssm_d,
           ssm_norm_w, gdn_conv_w, gdn_dt_bias, gdn_a_log, gdn_norm_w, w_branch_ssm, w_branch_gdn,
           w_out, ffn_norm_w, peer_w_q, peer_sub_keys, peer_u, peer_v, final_norm_w):
    b, t, d = x.shape
    assert b == 1 and mix_norm_w.shape[0] == 1, "single sequence, single layer"
    out = _layer(x[0], mix_norm_w[0], w_in[0], gate_b[0], ssm_conv_w[0], ssm_conv_b[0],
                 ssm_dt_bias[0], ssm_a_log[0], ssm_d[0], ssm_norm_w[0], gdn_conv_w[0],
                 gdn_dt_bias[0], gdn_a_log[0], gdn_norm_w[0], w_branch_ssm[0], w_branch_gdn[0],
                 w_out[0], ffn_norm_w[0], peer_w_q[0], peer_sub_keys[0], peer_u[0], peer_v[0],
                 final_norm_w)
    return out[None]
```

```python
import functools

import jax
import jax.numpy as jnp
from jax import lax
from jax.experimental import pallas as pl
from jax.experimental.pallas import tpu as pltpu

F32 = jnp.float32
BF16 = jnp.bfloat16
HIGHEST = lax.Precision.HIGHEST

EPS = 1e-6
D_MODEL = 2048
LANES = 128
SSM_D_INNER = 4096
SSM_HEAD_DIM = 64
SSM_N_HEADS = 64
SSM_N_GROUPS = 8
SSM_HPG = 8
SSM_D_STATE = 128
SSM_CHUNK = 128
SSM_GROUP_W = SSM_HPG * SSM_HEAD_DIM
GDN_N_HEADS = 16
GDN_HEAD_K = 128
GDN_HEAD_V = 256
GDN_CHUNK = 64
GDN_KEY_DIM = GDN_N_HEADS * GDN_HEAD_K
GDN_VAL_DIM = GDN_N_HEADS * GDN_HEAD_V
CONV_K = 4
PEER_HEADS = 8
PEER_N_KEYS = 128
PEER_TOPK = 16
PEER_N_EXPERTS = PEER_N_KEYS * PEER_N_KEYS

COL_SSM_Z = 0
COL_SSM_XBC = 4096
COL_GDN_QKV = 10240
COL_GDN_Z = 18432
COL_GATE = 22528
N_MAIN = 26624
SMALL_BETA = 64
SMALL_ALPHA = 80

VMEM_LIMIT = 56 * 1024 * 1024


def _softplus(x):
    return jnp.maximum(x, 0.0) + jnp.log1p(jnp.exp(-jnp.abs(x)))


def _silu(x):
    return x * jax.nn.sigmoid(x)


def _dot(a, b, **kw):
    return jnp.dot(a, b, preferred_element_type=F32, **kw)


def _dot_nt(a, b, **kw):
    return lax.dot_general(a, b, (((1,), (1,)), ((), ())), preferred_element_type=F32, **kw)


def _dot_tn(a, b, **kw):
    return lax.dot_general(a, b, (((0,), (0,)), ((), ())), preferred_element_type=F32, **kw)


def _params(n_grid):
    return pltpu.CompilerParams(dimension_semantics=("arbitrary",) * n_grid,
                                vmem_limit_bytes=VMEM_LIMIT)


def _inproj_kernel(x_ref, nw_ref, w_ref, ws_ref, o_ref, os_ref, xn_ref):
    @pl.when(pl.program_id(1) == 0)
    def _():
        x = x_ref[...]
        ms = jnp.mean(x * x, axis=-1, keepdims=True)
        xn = (x * lax.rsqrt(ms + EPS) * nw_ref[...]).astype(BF16)
        xn_ref[...] = xn
        os_ref[...] = _dot(xn, ws_ref[...])

    o_ref[...] = _dot(xn_ref[...], w_ref[...]).astype(BF16)


def _inproj(x, norm_w, w_main, w_small, tm=1024, tn=1024):
    t, d = x.shape
    n = w_main.shape[1]
    tm = min(tm, t)
    return pl.pallas_call(
        _inproj_kernel,
        grid=(t // tm, n // tn),
        in_specs=[
            pl.BlockSpec((tm, d), lambda i, j: (i, 0)),
            pl.BlockSpec((1, d), lambda i, j: (0, 0)),
            pl.BlockSpec((d, tn), lambda i, j: (0, j)),
            pl.BlockSpec((d, LANES), lambda i, j: (0, 0)),
        ],
        out_specs=[
            pl.BlockSpec((tm, tn), lambda i, j: (i, j)),
            pl.BlockSpec((tm, LANES), lambda i, j: (i, 0)),
        ],
        out_shape=[jax.ShapeDtypeStruct((t, n), BF16), jax.ShapeDtypeStruct((t, LANES), F32)],
        scratch_shapes=[pltpu.VMEM((tm, d), BF16)],
        compiler_params=_params(2),
        name="inproj",
    )(x, norm_w, w_main, w_small)


def _conv_block(raw_ref, buf_ref, w_ref, tc):
    buf_ref[pl.ds(8, tc), :] = raw_ref[...].astype(F32)
    y = buf_ref[pl.ds(8 - (CONV_K - 1), tc), :] * w_ref[0:1, :]
    for k in range(1, CONV_K):
        y = y + buf_ref[pl.ds(8 - (CONV_K - 1) + k, tc), :] * w_ref[k:k + 1, :]
    buf_ref[pl.ds(0, 8), :] = buf_ref[pl.ds(tc, 8), :]
    return y


def _ssd_kernel(xs_ref, b_ref, c_ref, z_ref, sm_ref, cwx_ref, cwb_ref, cwc_ref, cbx_ref, cbb_ref,
                cbc_ref, dtb_ref, alog_ref, dsk_ref, nw_ref, o_ref,
                xbuf, bbuf, cbuf, state_ref, acst_ref, *, tc):
    g = pl.program_id(0)
    L = SSM_CHUNK

    @pl.when(pl.program_id(1) == 0)
    def _():
        xbuf[pl.ds(0, 8), :] = jnp.zeros((8, SSM_GROUP_W), F32)
        bbuf[pl.ds(0, 8), :] = jnp.zeros((8, SSM_D_STATE), F32)
        cbuf[pl.ds(0, 8), :] = jnp.zeros((8, SSM_D_STATE), F32)
        state_ref[...] = jnp.zeros_like(state_ref)

    xs_all = _silu(_conv_block(xs_ref, xbuf, cwx_ref, tc) + cbx_ref[...])
    b_all = _silu(_conv_block(b_ref, bbuf, cwb_ref, tc) + cbb_ref[...])
    c_all = _silu(_conv_block(c_ref, cbuf, cwc_ref, tc) + cbc_ref[...])

    row = lax.broadcasted_iota(jnp.int32, (L, L), 0)
    col = lax.broadcasted_iota(jnp.int32, (L, L), 1)
    causal = row >= col
    tri = causal.astype(F32)
    lo_half = col < SSM_HEAD_DIM
    a_row = -jnp.exp(alog_ref[...])

    for c in range(tc // L):
        sl = slice(c * L, (c + 1) * L)
        xs, bm, cm = xs_all[sl], b_all[sl], c_all[sl]
        dt_full = _softplus(sm_ref[sl, :] + dtb_ref[...])
        acs_full = _dot(tri, dt_full * a_row, precision=HIGHEST)
        acst_ref[...] = acs_full.T
        bm16, cm16 = bm.astype(BF16), cm.astype(BF16)
        cb = _dot_nt(cm16, bm16)
        ys = []
        for p in range(SSM_HPG // 2):
            acs_bc, dt_bc, scores = [], [], []
            for r in (2 * p, 2 * p + 1):
                j = g * SSM_HPG + r
                sel = col == j
                a_c = jnp.sum(jnp.where(sel, acs_full, 0.0), axis=1, keepdims=True)
                d_c = jnp.sum(jnp.where(sel, dt_full, 0.0), axis=1, keepdims=True)
                a_b = jnp.broadcast_to(a_c, (L, L))
                acs_bc.append(a_b)
                dt_bc.append(jnp.broadcast_to(d_c, (L, L)))
                a_r = acst_ref[pl.ds(j, 1), :]
                seg = jnp.where(causal, a_b - a_r, -jnp.inf)
                scores.append((cb * jnp.exp(seg)).astype(BF16))
            acs_e = jnp.where(lo_half, acs_bc[0], acs_bc[1])
            dt_e = jnp.where(lo_half, dt_bc[0], dt_bc[1])
            lanes = slice(p * LANES, (p + 1) * LANES)
            x_p = xs[:, lanes]
            xdt = x_p * dt_e
            last = acs_e[L - 1:L, :]
            xdtw = (xdt * jnp.exp(last - acs_e)).astype(BF16)
            s2 = jnp.concatenate(scores, axis=1)
            x2 = jnp.concatenate([jnp.where(lo_half, xdt, 0.0), jnp.where(lo_half, 0.0, xdt)],
                                 axis=0).astype(BF16)
            y_diag = _dot(s2, x2)
            st = state_ref[:, lanes]
            y_off = _dot(cm16, st.astype(BF16)) * jnp.exp(acs_e)
            state_ref[:, lanes] = st * jnp.exp(last) + _dot_tn(bm16, xdtw)
            ys.append(y_diag + y_off + x_p * dsk_ref[:, lanes])
        y = jnp.concatenate(ys, axis=1) * _silu(z_ref[sl, :].astype(F32))
        y = y * lax.rsqrt(jnp.mean(y * y, axis=-1, keepdims=True) + EPS)
        o_ref[sl, :] = (y * nw_ref[...]).astype(BF16)


def _ssd(proj, small, cwx, cwb, cwc, cbx, cbb, cbc, dtb_row, alog_row, dsk_row, nw_row, tc=256):
    t = proj.shape[0]
    tc = min(tc, t)
    gw, ns = SSM_GROUP_W, SSM_D_STATE
    xs_blk = COL_SSM_XBC // gw
    b_blk = (COL_SSM_XBC + SSM_D_INNER) // ns
    c_blk = b_blk + SSM_N_GROUPS
    z_blk = COL_SSM_Z // gw
    kern = functools.partial(_ssd_kernel, tc=tc)
    return pl.pallas_call(
        kern,
        grid=(SSM_N_GROUPS, t // tc),
        in_specs=[
            pl.BlockSpec((tc, gw), lambda g, i: (i, xs_blk + g)),
            pl.BlockSpec((tc, ns), lambda g, i: (i, b_blk + g)),
            pl.BlockSpec((tc, ns), lambda g, i: (i, c_blk + g)),
            pl.BlockSpec((tc, gw), lambda g, i: (i, z_blk + g)),
            pl.BlockSpec((tc, LANES), lambda g, i: (i, 0)),
            pl.BlockSpec((CONV_K, gw), lambda g, i: (0, g)),
            pl.BlockSpec((CONV_K, ns), lambda g, i: (0, g)),
            pl.BlockSpec((CONV_K, ns), lambda g, i: (0, g)),
            pl.BlockSpec((1, gw), lambda g, i: (0, g)),
            pl.BlockSpec((1, ns), lambda g, i: (0, g)),
            pl.BlockSpec((1, ns), lambda g, i: (0, g)),
            pl.BlockSpec((1, LANES), lambda g, i: (0, 0)),
            pl.BlockSpec((1, LANES), lambda g, i: (0, 0)),
            pl.BlockSpec((1, gw), lambda g, i: (0, g)),
            pl.BlockSpec((1, gw), lambda g, i: (0, g)),
        ],
        out_specs=pl.BlockSpec((tc, gw), lambda g, i: (i, g)),
        out_shape=jax.ShapeDtypeStruct((t, SSM_D_INNER), BF16),
        scratch_shapes=[
            pltpu.VMEM((tc + 8, gw), F32),
            pltpu.VMEM((tc + 8, ns), F32),
            pltpu.VMEM((tc + 8, ns), F32),
            pltpu.VMEM((ns, gw), F32),
            pltpu.VMEM((LANES, SSM_CHUNK), F32),
        ],
        compiler_params=_params(2),
        name="ssd",
    )(proj, proj, proj, proj, small, cwx, cwb, cwc, cbx, cbb, cbc, dtb_row, alog_row, dsk_row, nw_row)


def _gdn_kernel(q_ref, k_ref, v_ref, z_ref, sm_ref, cwq_ref, cwk_ref, cwv_ref, dtb_ref, alog_ref,
                nw_ref, o_ref, qbuf, kbuf, vbuf, s_ref, *, tc, hps):
    h0 = pl.program_id(0) * hps
    C, DK, DV = GDN_CHUNK, GDN_HEAD_K, GDN_HEAD_V

    @pl.when(pl.program_id(1) == 0)
    def _():
        qbuf[pl.ds(0, 8), :] = jnp.zeros((8, hps * DK), F32)
        kbuf[pl.ds(0, 8), :] = jnp.zeros((8, hps * DK), F32)
        vbuf[pl.ds(0, 8), :] = jnp.zeros((8, hps * DV), F32)
        s_ref[...] = jnp.zeros_like(s_ref)

    q_cv = _silu(_conv_block(q_ref, qbuf, cwq_ref, tc))
    k_cv = _silu(_conv_block(k_ref, kbuf, cwk_ref, tc))
    v_cv = _silu(_conv_block(v_ref, vbuf, cwv_ref, tc))

    sm = sm_ref[...]
    lane = lax.broadcasted_iota(jnp.int32, sm.shape, 1)
    beta_full = jax.nn.sigmoid(sm)
    g_full = -jnp.exp(alog_ref[...]) * _softplus(sm + dtb_ref[...])
    q_hd, k_hd, v_hd, beta_hd, g_hd = [], [], [], [], []
    for hd in range(hps):
        q = q_cv[:, hd * DK:(hd + 1) * DK]
        k = k_cv[:, hd * DK:(hd + 1) * DK]
        q_hd.append(q * lax.rsqrt(jnp.sum(q * q, axis=-1, keepdims=True) + EPS) * (DK ** -0.5))
        k_hd.append(k * lax.rsqrt(jnp.sum(k * k, axis=-1, keepdims=True) + EPS))
        v_hd.append(v_cv[:, hd * DV:(hd + 1) * DV])
        beta_hd.append(jnp.sum(jnp.where(lane == SMALL_BETA + h0 + hd, beta_full, 0.0),
                               axis=1, keepdims=True))
        g_hd.append(jnp.sum(jnp.where(lane == SMALL_ALPHA + h0 + hd, g_full, 0.0),
                            axis=1, keepdims=True))

    row = lax.broadcasted_iota(jnp.int32, (C, C), 0)
    col = lax.broadcasted_iota(jnp.int32, (C, C), 1)
    causal = row >= col
    strict = row > col
    tri = causal.astype(F32)
    eye = (row == col).astype(F32)

    nc = tc // C
    items = [(hd, slice(c * C, (c + 1) * C)) for hd in range(hps) for c in range(nc)]
    rng = range(len(items))
    strict_w = (lax.broadcasted_iota(jnp.int32, (C, LANES), 0)
                > lax.broadcasted_iota(jnp.int32, (C, LANES), 1))
    qs = [q_hd[hd][sl] for hd, sl in items]
    ks = [k_hd[hd][sl] for hd, sl in items]
    vs = [v_hd[hd][sl] for hd, sl in items]
    betas = [beta_hd[hd][sl] for hd, sl in items]
    g_bc = [jnp.broadcast_to(g_hd[hd][sl], (C, LANES)) for hd, sl in items]
    cums = [_dot(tri, jnp.concatenate([g_bc[c], jnp.where(strict_w, g_bc[c], 0.0)], axis=1),
                 precision=HIGHEST) for c in rng]
    gc_bc = [x[:, :LANES] for x in cums]
    decay = [jnp.exp(jnp.where(causal, x[:, LANES:LANES + C], -jnp.inf)) for x in cums]
    k16 = [k.astype(BF16) for k in ks]
    q16 = [q.astype(BF16) for q in qs]
    kk = [_dot_nt(k16[c], k16[c]) for c in rng]
    qk = [_dot_nt(q16[c], k16[c]) for c in rng]
    p = [-jnp.where(strict, kk[c] * betas[c] * decay[c], 0.0) for c in rng]
    x = [eye + p[c] for c in rng]
    for _ in range(C.bit_length() - 2):
        p16 = [pp.astype(BF16) for pp in p]
        p = [_dot(p16[c], p16[c]) for c in rng]
        x = [x[c] + _dot(x[c].astype(BF16), p[c].astype(BF16)) for c in rng]
    t_inv = [xx.astype(BF16) for xx in x]
    egc = [jnp.exp(gc) for gc in gc_bc]
    u = [_dot(t_inv[c], (vs[c] * betas[c]).astype(BF16)).astype(BF16) for c in rng]
    w = [_dot(t_inv[c], (ks[c] * (betas[c] * egc[c])).astype(BF16)).astype(BF16) for c in rng]
    attn = [jnp.where(causal, qk[c] * decay[c], 0.0).astype(BF16) for c in rng]
    gc_last = [gc[C - 1:C, :] for gc in gc_bc]
    k_dec = [(ks[c] * jnp.exp(gc_last[c] - gc_bc[c])).astype(BF16) for c in rng]
    c_dec = [jnp.exp(gl[:, :1]) for gl in gc_last]
    a_mat = [_dot_tn(k_dec[c], w[c]).astype(BF16) for c in rng]
    n_mat = [_dot_tn(k_dec[c], u[c]) for c in rng]
    q_eff = [(qs[c] * egc[c] - _dot(attn[c], w[c])).astype(BF16) for c in rng]
    o_u = [_dot(attn[c], u[c]) for c in rng]

    s = [s_ref[hd] for hd in range(hps)]
    for c in range(nc):
        sl = slice(c * C, (c + 1) * C)
        for hd in range(hps):
            it = hd * nc + c
            s16 = s[hd].astype(BF16)
            o = _dot(q_eff[it], s16) + o_u[it]
            s[hd] = s[hd] * c_dec[it] - _dot(a_mat[it], s16) + n_mat[it]
            o = o * lax.rsqrt(jnp.mean(o * o, axis=-1, keepdims=True) + EPS) * nw_ref[...]
            z = z_ref[sl, hd * DV:(hd + 1) * DV].astype(F32)
            o_ref[sl, hd * DV:(hd + 1) * DV] = (o * _silu(z)).astype(BF16)
    for hd in range(hps):
        s_ref[hd] = s[hd]


def _gdn(proj, small, cwq, cwk, cwv, dtb_row, alog_row, nw_row, tc=256, hps=2):
    t = proj.shape[0]
    tc = min(tc, t)
    dk, dv = hps * GDN_HEAD_K, hps * GDN_HEAD_V
    q_blk = COL_GDN_QKV // dk
    k_blk = q_blk + GDN_N_HEADS // hps
    v_blk = (COL_GDN_QKV + 2 * GDN_KEY_DIM) // dv
    z_blk = COL_GDN_Z // dv
    kern = functools.partial(_gdn_kernel, tc=tc, hps=hps)
    return pl.pallas_call(
        kern,
        grid=(GDN_N_HEADS // hps, t // tc),
        in_specs=[
            pl.BlockSpec((tc, dk), lambda h, i: (i, q_blk + h)),
            pl.BlockSpec((tc, dk), lambda h, i: (i, k_blk + h)),
            pl.BlockSpec((tc, dv), lambda h, i: (i, v_blk + h)),
            pl.BlockSpec((tc, dv), lambda h, i: (i, z_blk + h)),
            pl.BlockSpec((tc, LANES), lambda h, i: (i, 0)),
            pl.BlockSpec((CONV_K, dk), lambda h, i: (0, h)),
            pl.BlockSpec((CONV_K, dk), lambda h, i: (0, h)),
            pl.BlockSpec((CONV_K, dv), lambda h, i: (0, h)),
            pl.BlockSpec((1, LANES), lambda h, i: (0, 0)),
            pl.BlockSpec((1, LANES), lambda h, i: (0, 0)),
            pl.BlockSpec((1, GDN_HEAD_V), lambda h, i: (0, 0)),
        ],
        out_specs=pl.BlockSpec((tc, dv), lambda h, i: (i, h)),
        out_shape=jax.ShapeDtypeStruct((t, GDN_VAL_DIM), BF16),
        scratch_shapes=[
            pltpu.VMEM((tc + 8, dk), F32),
            pltpu.VMEM((tc + 8, dk), F32),
            pltpu.VMEM((tc + 8, dv), F32),
            pltpu.VMEM((hps, GDN_HEAD_K, GDN_HEAD_V), F32),
        ],
        compiler_params=_params(2),
        name="gdn",
    )(proj, proj, proj, proj, small, cwq, cwk, cwv, dtb_row, alog_row, nw_row)


def _mix_kernel(ys_ref, yg_ref, g0_ref, g1_ref, gb_ref, ws_ref, wg_ref, o_ref):
    a = _dot(ys_ref[...], ws_ref[...])
    b = _dot(yg_ref[...], wg_ref[...])
    g0 = jax.nn.sigmoid(g0_ref[...].astype(F32) + gb_ref[0:1, :])
    g1 = jax.nn.sigmoid(g1_ref[...].astype(F32) + gb_ref[1:2, :])
    o_ref[...] = (g0 * a + g1 * b).astype(BF16)


def _mix(y_ssm, y_gdn, proj, gate_b, w_s, w_g, tm=512, tn=512):
    t = y_ssm.shape[0]
    tm = min(tm, t)
    g0_blk = COL_GATE // tn
    g1_blk = (COL_GATE + D_MODEL) // tn
    return pl.pallas_call(
        _mix_kernel,
        grid=(D_MODEL // tn, t // tm),
        in_specs=[
            pl.BlockSpec((tm, SSM_D_INNER), lambda n, i: (i, 0)),
            pl.BlockSpec((tm, GDN_VAL_DIM), lambda n, i: (i, 0)),
            pl.BlockSpec((tm, tn), lambda n, i: (i, g0_blk + n)),
            pl.BlockSpec((tm, tn), lambda n, i: (i, g1_blk + n)),
            pl.BlockSpec((2, tn), lambda n, i: (0, n)),
            pl.BlockSpec((SSM_D_INNER, tn), lambda n, i: (0, n)),
            pl.BlockSpec((GDN_VAL_DIM, tn), lambda n, i: (0, n)),
        ],
        out_specs=pl.BlockSpec((tm, tn), lambda n, i: (i, n)),
        out_shape=jax.ShapeDtypeStruct((t, D_MODEL), BF16),
        compiler_params=_params(2),
        name="mix",
    )(y_ssm, y_gdn, proj, proj, gate_b, w_s, w_g)


_N_RANK = PEER_TOPK + 1
_CAND_PAIRS = [(i, j) for i in range(_N_RANK) for j in range(_N_RANK) if (i + 1) * (j + 1) <= _N_RANK]
_CAND_ROWS = -(-len(_CAND_PAIRS) // 8) * 8


def _top_desc(s, n):
    out = []
    for _ in range(n):
        m = jnp.max(s, axis=0, keepdims=True)
        out.append(m)
        s = jnp.where(s == m, -jnp.inf, s)
    return out


def _post_kernel(x_ref, mix_ref, wo_ref, fw_ref, wq_ref, sk_ref,
                 h1_ref, hnt_ref, s2_ref, e2_ref, th_ref, e1_ref, cand_ref, *, tm):
    h1 = x_ref[...] + _dot(mix_ref[...], wo_ref[...])
    h1_ref[...] = h1
    hn = h1 * lax.rsqrt(jnp.mean(h1 * h1, axis=-1, keepdims=True) + EPS) * fw_ref[...]
    hnt_ref[...] = hn.T.astype(BF16)
    qv = _dot(hn.astype(BF16), wq_ref[...]).astype(BF16)
    nchunk = tm // LANES
    for h in range(PEER_HEADS):
        s1 = _dot_nt(sk_ref[2 * h], qv[:, (2 * h) * LANES:(2 * h + 1) * LANES])
        s2 = _dot_nt(sk_ref[2 * h + 1], qv[:, (2 * h + 1) * LANES:(2 * h + 2) * LANES])
        a1 = _top_desc(s1, _N_RANK)
        a2 = _top_desc(s2, _N_RANK)
        cand_ref[...] = jnp.full(cand_ref.shape, -jnp.inf, F32)
        for r, (i, j) in enumerate(_CAND_PAIRS):
            cand_ref[pl.ds(r, 1), :] = a1[i] + a2[j]
        cand = cand_ref[...]
        c = cand
        n_removed = jnp.zeros((1, tm), F32)
        v16 = jnp.full((1, tm), -jnp.inf, F32)
        v17 = jnp.full((1, tm), -jnp.inf, F32)
        for _ in range(_N_RANK):
            m = jnp.max(c, axis=0, keepdims=True)
            eq = c == m
            cnt = jnp.sum(eq.astype(F32), axis=0, keepdims=True)
            v16 = jnp.where(n_removed < PEER_TOPK, m, v16)
            v17 = jnp.where(n_removed < _N_RANK, m, v17)
            n_removed = n_removed + cnt
            c = jnp.where(eq, -jnp.inf, c)
        tau = 0.5 * (v16 + v17)
        tau = jnp.where(v17 == -jnp.inf, v16, tau)
        m_tot = a1[0] + a2[0]
        z = jnp.sum(jnp.where(cand >= tau, jnp.exp(cand - m_tot), 0.0), axis=0, keepdims=True)
        e2 = jnp.exp(s2 - a2[0]) / z
        e1 = jnp.exp(s1 - a1[0])
        th = tau - s1
        for cc in range(nchunk):
            ls = slice(cc * LANES, (cc + 1) * LANES)
            s2_ref[cc, h] = s2[:, ls]
            e2_ref[cc, h] = e2[:, ls]
            th_ref[cc, h] = th[:, ls]
            e1_ref[cc, h] = e1[:, ls]


def _post(x, mix, w_out, ffn_w, w_q, sub_keys, tm=256):
    t, d = x.shape
    tm = min(tm, t)
    nchunk = tm // LANES
    stat = jax.ShapeDtypeStruct((t // LANES, PEER_HEADS, PEER_N_KEYS, LANES), F32)
    stat_spec = pl.BlockSpec((nchunk, PEER_HEADS, PEER_N_KEYS, LANES), lambda i: (i, 0, 0, 0))
    kern = functools.partial(_post_kernel, tm=tm)
    return pl.pallas_call(
        kern,
        grid=(t // tm,),
        in_specs=[
            pl.BlockSpec((tm, d), lambda i: (i, 0)),
            pl.BlockSpec((tm, d), lambda i: (i, 0)),
            pl.BlockSpec((d, d), lambda i: (0, 0)),
            pl.BlockSpec((1, d), lambda i: (0, 0)),
            pl.BlockSpec((d, d), lambda i: (0, 0)),
            pl.BlockSpec((2 * PEER_HEADS, PEER_N_KEYS, LANES), lambda i: (0, 0, 0)),
        ],
        out_specs=[
            pl.BlockSpec((tm, d), lambda i: (i, 0)),
            pl.BlockSpec((d, tm), lambda i: (0, i)),
            stat_spec, stat_spec, stat_spec, stat_spec,
        ],
        out_shape=[
            jax.ShapeDtypeStruct((t, d), F32),
            jax.ShapeDtypeStruct((d, t), BF16),
            stat, stat, stat, stat,
        ],
        scratch_shapes=[pltpu.VMEM((_CAND_ROWS, tm), F32)],
        compiler_params=_params(1),
        name="post",
    )(x, mix, w_out, ffn_w, w_q, sub_keys)


_ROW_TILE = 64


def _gelu(x):
    return 0.5 * x * (1.0 + lax.erf(x * (2.0 ** -0.5)))


def _peer_kernel(hnt_ref, u_ref, vt_ref, s2_ref, e2_ref, th_ref, e1_ref, acc_ref, pt0_ref, pt1_ref,
                 *, tb, eb):
    n1 = eb // PEER_N_KEYS

    @pl.when(pl.program_id(1) == 0)
    def _():
        acc_ref[...] = jnp.zeros_like(acc_ref)

    def weights_times_act(at, pt_ref, row0):
        for cc in range(tb // LANES):
            ls = slice(cc * LANES, (cc + 1) * LANES)
            for rt in range(PEER_N_KEYS // _ROW_TILE):
                rs = slice(rt * _ROW_TILE, (rt + 1) * _ROW_TILE)
                w = [jnp.zeros((_ROW_TILE, LANES), F32) for _ in range(n1)]
                for h in range(PEER_HEADS):
                    s2 = s2_ref[cc, h, rs, :]
                    e2 = e2_ref[cc, h, rs, :]
                    for i in range(n1):
                        th = th_ref[cc, h, row0 + i:row0 + i + 1, :]
                        e1 = e1_ref[cc, h, row0 + i:row0 + i + 1, :]
                        w[i] = w[i] + jnp.where(s2 >= th, e2, 0.0) * e1
                for i in range(n1):
                    r0 = i * PEER_N_KEYS + rt * _ROW_TILE
                    a = at[r0:r0 + _ROW_TILE, ls]
                    pt_ref[r0:r0 + _ROW_TILE, ls] = (w[i] * _gelu(a)).astype(BF16)

    hnt = hnt_ref[...]
    at0 = _dot(u_ref[0:eb, :], hnt)
    at1 = _dot(u_ref[eb:2 * eb, :], hnt)
    weights_times_act(at0, pt0_ref, 0)
    acc_ref[...] += _dot(vt_ref[:, 0:eb], pt0_ref[...])
    weights_times_act(at1, pt1_ref, n1)
    acc_ref[...] += _dot(vt_ref[:, eb:2 * eb], pt1_ref[...])


def _peer(hnt, u16, vt16, s2, e2, th, e1, tb=512, eb=512):
    d, t = hnt.shape
    tb = min(tb, t)
    nchunk = tb // LANES
    n1 = eb // PEER_N_KEYS
    assert 2 * n1 == 8, "a pair of expert blocks spans one 8-row group of first-key statistics"
    stat_spec = pl.BlockSpec((nchunk, PEER_HEADS, PEER_N_KEYS, LANES), lambda i, k: (i, 0, 0, 0))
    row_spec = pl.BlockSpec((nchunk, PEER_HEADS, 8, LANES), lambda i, k: (i, 0, k, 0))
    kern = functools.partial(_peer_kernel, tb=tb, eb=eb)
    return pl.pallas_call(
        kern,
        grid=(t // tb, PEER_N_EXPERTS // (2 * eb)),
        in_specs=[
            pl.BlockSpec((d, tb), lambda i, k: (0, i)),
            pl.BlockSpec((2 * eb, d), lambda i, k: (k, 0)),
            pl.BlockSpec((d, 2 * eb), lambda i, k: (0, k)),
            stat_spec, stat_spec, row_spec, row_spec,
        ],
        out_specs=pl.BlockSpec((d, tb), lambda i, k: (0, i)),
        out_shape=jax.ShapeDtypeStruct((d, t), F32),
        scratch_shapes=[pltpu.VMEM((eb, tb), BF16), pltpu.VMEM((eb, tb), BF16)],
        compiler_params=_params(2),
        name="peer",
    )(hnt, u16, vt16, s2, e2, th, e1)


def _final_kernel(h_ref, pt_ref, w_ref, o_ref):
    hf = h_ref[...] + pt_ref[...].T
    o_ref[...] = hf * lax.rsqrt(jnp.mean(hf * hf, axis=-1, keepdims=True) + EPS) * w_ref[...]


def _final(h1, peer_out_t, w_row, tm=512):
    t, d = h1.shape
    tm = min(tm, t)
    spec = pl.BlockSpec((tm, d), lambda i: (i, 0))
    return pl.pallas_call(
        _final_kernel,
        grid=(t // tm,),
        in_specs=[spec, pl.BlockSpec((d, tm), lambda i: (0, i)), pl.BlockSpec((1, d), lambda i: (0, 0))],
        out_specs=spec,
        out_shape=jax.ShapeDtypeStruct((t, d), F32),
        compiler_params=_params(1),
        name="final",
    )(h1, peer_out_t, w_row)


def _row(v, width=None):
    v = v.astype(F32).reshape(1, -1)
    if width is not None and v.shape[1] < width:
        v = jnp.pad(v, ((0, 0), (0, width - v.shape[1])))
    return v


def _layer(h, mix_norm_w, w_in, gate_b, ssm_conv_w, ssm_conv_b, ssm_dt_bias, ssm_a_log, ssm_d,
           ssm_norm_w, gdn_conv_w, gdn_dt_bias, gdn_a_log, gdn_norm_w, w_branch_ssm, w_branch_gdn,
           w_out, ffn_norm_w, peer_w_q, peer_sub_keys, peer_u, peer_v, out_norm_w):
    ssm_proj = SSM_D_INNER + (SSM_D_INNER + 2 * SSM_N_GROUPS * SSM_D_STATE) + SSM_N_HEADS
    gdn_conv_dim = 2 * GDN_KEY_DIM + GDN_VAL_DIM
    o_dt = ssm_proj - SSM_N_HEADS
    o_qkv = ssm_proj
    o_gz = o_qkv + gdn_conv_dim
    o_beta = o_gz + GDN_VAL_DIM
    o_gate = o_beta + 2 * GDN_N_HEADS
    w_main = jnp.concatenate(
        [w_in[:, :o_dt], w_in[:, o_qkv:o_beta], w_in[:, o_gate:]], axis=1).astype(BF16)
    w_small = jnp.concatenate(
        [w_in[:, o_dt:o_qkv], w_in[:, o_beta:o_gate],
         jnp.zeros((D_MODEL, LANES - SSM_N_HEADS - 2 * GDN_N_HEADS), w_in.dtype)], axis=1).astype(BF16)

    proj, small = _inproj(h, _row(mix_norm_w), w_main, w_small)

    cw = ssm_conv_w.astype(F32)
    cbias = _row(ssm_conv_b)
    nx = SSM_D_INNER
    nb = SSM_N_GROUPS * SSM_D_STATE
    y_ssm = _ssd(proj, small,
                 cw[:, :nx], cw[:, nx:nx + nb], cw[:, nx + nb:],
                 cbias[:, :nx], cbias[:, nx:nx + nb], cbias[:, nx + nb:],
                 _row(ssm_dt_bias, LANES), _row(ssm_a_log, LANES),
                 _row(jnp.repeat(ssm_d, SSM_HEAD_DIM)), _row(ssm_norm_w))

    gw = gdn_conv_w.astype(F32)
    zeros_b = jnp.zeros((SMALL_ALPHA,), F32)
    y_gdn = _gdn(proj, small,
                 gw[:, :GDN_KEY_DIM], gw[:, GDN_KEY_DIM:2 * GDN_KEY_DIM], gw[:, 2 * GDN_KEY_DIM:],
                 _row(jnp.concatenate([zeros_b, gdn_dt_bias.astype(F32)]), LANES),
                 _row(jnp.concatenate([zeros_b, gdn_a_log.astype(F32)]), LANES),
                 _row(gdn_norm_w))

    mix = _mix(y_ssm, y_gdn, proj, gate_b.astype(F32),
               w_branch_ssm.astype(BF16), w_branch_gdn.astype(BF16))

    sk = peer_sub_keys.reshape(2 * PEER_HEADS, PEER_N_KEYS, LANES).astype(BF16)
    h1, hnt, s2, e2, th, e1 = _post(h, mix, w_out.astype(BF16), _row(ffn_norm_w),
                                    peer_w_q.astype(BF16), sk)

    peer_out_t = _peer(hnt, peer_u.astype(BF16), peer_v.T.astype(BF16), s2, e2, th, e1)
    return _final(h1, peer_out_t, _row(out_norm_w))


def kernel(x, mix_norm_w, w_in, gate_b, ssm_conv_w, ssm_conv_b, ssm_dt_bias, ssm_a_log, ssm_d,
           ssm_norm_w, gdn_conv_w, gdn_dt_bias, gdn_a_log, gdn_norm_w, w_branch_ssm, w_branch_gdn,
           w_out, ffn_norm_w, peer_w_q, peer_sub_keys, peer_u, peer_v, final_norm_w):
    b, t, d = x.shape
    assert b == 1 and mix_norm_w.shape[0] == 1, "single sequence, single layer"
    out = _layer(x[0], mix_norm_w[0], w_in[0], gate_b[0], ssm_conv_w[0], ssm_conv_b[0],
                 ssm_dt_bias[0], ssm_a_log[0], ssm_d[0], ssm_norm_w[0], gdn_conv_w[0],
                 gdn_dt_bias[0], gdn_a_log[0], gdn_norm_w[0], w_branch_ssm[0], w_branch_gdn[0],
                 w_out[0], ffn_norm_w[0], peer_w_q[0], peer_sub_keys[0], peer_u[0], peer_v[0],
                 final_norm_w)
    return out[None]
```

```python
import functools

import jax
import jax.numpy as jnp
from jax import lax
from jax.experimental import pallas as pl
from jax.experimental.pallas import tpu as pltpu

F32 = jnp.float32
BF16 = jnp.bfloat16
HIGHEST = lax.Precision.HIGHEST

EPS = 1e-6
D_MODEL = 2048
LANES = 128
SSM_D_INNER = 4096
SSM_HEAD_DIM = 64
SSM_N_HEADS = 64
SSM_N_GROUPS = 8
SSM_HPG = 8
SSM_D_STATE = 128
SSM_CHUNK = 128
SSM_GROUP_W = SSM_HPG * SSM_HEAD_DIM
GDN_N_HEADS = 16
GDN_HEAD_K = 128
GDN_HEAD_V = 256
GDN_CHUNK = 64
GDN_KEY_DIM = GDN_N_HEADS * GDN_HEAD_K
GDN_VAL_DIM = GDN_N_HEADS * GDN_HEAD_V
CONV_K = 4
PEER_HEADS = 8
PEER_N_KEYS = 128
PEER_TOPK = 16
PEER_N_EXPERTS = PEER_N_KEYS * PEER_N_KEYS

COL_SSM_Z = 0
COL_SSM_XBC = 4096
COL_GDN_QKV = 10240
COL_GDN_Z = 18432
COL_GATE = 22528
N_MAIN = 26624
SMALL_BETA = 64
SMALL_ALPHA = 80

VMEM_LIMIT = 56 * 1024 * 1024


def _softplus(x):
    return jnp.maximum(x, 0.0) + jnp.log1p(jnp.exp(-jnp.abs(x)))


def _silu(x):
    return x * jax.nn.sigmoid(x)


def _dot(a, b, **kw):
    return jnp.dot(a, b, preferred_element_type=F32, **kw)


def _dot_nt(a, b, **kw):
    return lax.dot_general(a, b, (((1,), (1,)), ((), ())), preferred_element_type=F32, **kw)


def _dot_tn(a, b, **kw):
    return lax.dot_general(a, b, (((0,), (0,)), ((), ())), preferred_element_type=F32, **kw)


def _params(n_grid):
    return pltpu.CompilerParams(dimension_semantics=("arbitrary",) * n_grid,
                                vmem_limit_bytes=VMEM_LIMIT)


def _inproj_kernel(x_ref, nw_ref, w_ref, ws_ref, o_ref, os_ref, xn_ref):
    @pl.when(pl.program_id(1) == 0)
    def _():
        x = x_ref[...]
        ms = jnp.mean(x * x, axis=-1, keepdims=True)
        xn = (x * lax.rsqrt(ms + EPS) * nw_ref[...]).astype(BF16)
        xn_ref[...] = xn
        os_ref[...] = _dot(xn, ws_ref[...])

    o_ref[...] = _dot(xn_ref[...], w_ref[...]).astype(BF16)


def _inproj(x, norm_w, w_main, w_small, tm=1024, tn=1024):
    t, d = x.shape
    n = w_main.shape[1]
    tm = min(tm, t)
    return pl.pallas_call(
        _inproj_kernel,
        grid=(t // tm, n // tn),
        in_specs=[
            pl.BlockSpec((tm, d), lambda i, j: (i, 0)),
            pl.BlockSpec((1, d), lambda i, j: (0, 0)),
            pl.BlockSpec((d, tn), lambda i, j: (0, j)),
            pl.BlockSpec((d, LANES), lambda i, j: (0, 0)),
        ],
        out_specs=[
            pl.BlockSpec((tm, tn), lambda i, j: (i, j)),
            pl.BlockSpec((tm, LANES), lambda i, j: (i, 0)),
        ],
        out_shape=[jax.ShapeDtypeStruct((t, n), BF16), jax.ShapeDtypeStruct((t, LANES), F32)],
        scratch_shapes=[pltpu.VMEM((tm, d), BF16)],
        compiler_params=_params(2),
        name="inproj",
    )(x, norm_w, w_main, w_small)


def _conv_block(raw_ref, buf_ref, w_ref, tc):
    buf_ref[pl.ds(8, tc), :] = raw_ref[...].astype(F32)
    y = buf_ref[pl.ds(8 - (CONV_K - 1), tc), :] * w_ref[0:1, :]
    for k in range(1, CONV_K):
        y = y + buf_ref[pl.ds(8 - (CONV_K - 1) + k, tc), :] * w_ref[k:k + 1, :]
    buf_ref[pl.ds(0, 8), :] = buf_ref[pl.ds(tc, 8), :]
    return y


def _ssd_kernel(xs_ref, b_ref, c_ref, z_ref, sm_ref, cwx_ref, cwb_ref, cwc_ref, cbx_ref, cbb_ref,
                cbc_ref, dtb_ref, alog_ref, dsk_ref, nw_ref, o_ref,
                xbuf, bbuf, cbuf, state_ref, acst_ref, *, tc):
    g = pl.program_id(0)
    L = SSM_CHUNK

    @pl.when(pl.program_id(1) == 0)
    def _():
        xbuf[pl.ds(0, 8), :] = jnp.zeros((8, SSM_GROUP_W), F32)
        bbuf[pl.ds(0, 8), :] = jnp.zeros((8, SSM_D_STATE), F32)
        cbuf[pl.ds(0, 8), :] = jnp.zeros((8, SSM_D_STATE), F32)
        state_ref[...] = jnp.zeros_like(state_ref)

    xs_all = _silu(_conv_block(xs_ref, xbuf, cwx_ref, tc) + cbx_ref[...])
    b_all = _silu(_conv_block(b_ref, bbuf, cwb_ref, tc) + cbb_ref[...])
    c_all = _silu(_conv_block(c_ref, cbuf, cwc_ref, tc) + cbc_ref[...])

    row = lax.broadcasted_iota(jnp.int32, (L, L), 0)
    col = lax.broadcasted_iota(jnp.int32, (L, L), 1)
    causal = row >= col
    tri = causal.astype(F32)
    lo_half = col < SSM_HEAD_DIM
    a_row = -jnp.exp(alog_ref[...])

    for c in range(tc // L):
        sl = slice(c * L, (c + 1) * L)
        xs, bm, cm = xs_all[sl], b_all[sl], c_all[sl]
        dt_full = _softplus(sm_ref[sl, :] + dtb_ref[...])
        acs_full = _dot(tri, dt_full * a_row, precision=HIGHEST)
        acst_ref[...] = acs_full.T
        bm16, cm16 = bm.astype(BF16), cm.astype(BF16)
        cb = _dot_nt(cm16, bm16)
        ys = []
        for p in range(SSM_HPG // 2):
            acs_bc, dt_bc, scores = [], [], []
            for r in (2 * p, 2 * p + 1):
                j = g * SSM_HPG + r
                sel = col == j
                a_c = jnp.sum(jnp.where(sel, acs_full, 0.0), axis=1, keepdims=True)
                d_c = jnp.sum(jnp.where(sel, dt_full, 0.0), axis=1, keepdims=True)
                a_b = jnp.broadcast_to(a_c, (L, L))
                acs_bc.append(a_b)
                dt_bc.append(jnp.broadcast_to(d_c, (L, L)))
                a_r = acst_ref[pl.ds(j, 1), :]
                seg = jnp.where(causal, a_b - a_r, -jnp.inf)
                scores.append((cb * jnp.exp(seg)).astype(BF16))
            acs_e = jnp.where(lo_half, acs_bc[0], acs_bc[1])
            dt_e = jnp.where(lo_half, dt_bc[0], dt_bc[1])
            lanes = slice(p * LANES, (p + 1) * LANES)
            x_p = xs[:, lanes]
            xdt = x_p * dt_e
            last = acs_e[L - 1:L, :]
            xdtw = (xdt * jnp.exp(last - acs_e)).astype(BF16)
            s2 = jnp.concatenate(scores, axis=1)
            x2 = jnp.concatenate([jnp.where(lo_half, xdt, 0.0), jnp.where(lo_half, 0.0, xdt)],
                                 axis=0).astype(BF16)
            y_diag = _dot(s2, x2)
            st = state_ref[:, lanes]
            y_off = _dot(cm16, st.astype(BF16)) * jnp.exp(acs_e)
            state_ref[:, lanes] = st * jnp.exp(last) + _dot_tn(bm16, xdtw)
            ys.append(y_diag + y_off + x_p * dsk_ref[:, lanes])
        y = jnp.concatenate(ys, axis=1) * _silu(z_ref[sl, :].astype(F32))
        y = y * lax.rsqrt(jnp.mean(y * y, axis=-1, keepdims=True) + EPS)
        o_ref[sl, :] = (y * nw_ref[...]).astype(BF16)


def _ssd(proj, small, cwx, cwb, cwc, cbx, cbb, cbc, dtb_row, alog_row, dsk_row, nw_row, tc=256):
    t = proj.shape[0]
    tc = min(tc, t)
    gw, ns = SSM_GROUP_W, SSM_D_STATE
    xs_blk = COL_SSM_XBC // gw
    b_blk = (COL_SSM_XBC + SSM_D_INNER) // ns
    c_blk = b_blk + SSM_N_GROUPS
    z_blk = COL_SSM_Z // gw
    kern = functools.partial(_ssd_kernel, tc=tc)
    return pl.pallas_call(
        kern,
        grid=(SSM_N_GROUPS, t // tc),
        in_specs=[
            pl.BlockSpec((tc, gw), lambda g, i: (i, xs_blk + g)),
            pl.BlockSpec((tc, ns), lambda g, i: (i, b_blk + g)),
            pl.BlockSpec((tc, ns), lambda g, i: (i, c_blk + g)),
            pl.BlockSpec((tc, gw), lambda g, i: (i, z_blk + g)),
            pl.BlockSpec((tc, LANES), lambda g, i: (i, 0)),
            pl.BlockSpec((CONV_K, gw), lambda g, i: (0, g)),
            pl.BlockSpec((CONV_K, ns), lambda g, i: (0, g)),
            pl.BlockSpec((CONV_K, ns), lambda g, i: (0, g)),
            pl.BlockSpec((1, gw), lambda g, i: (0, g)),
            pl.BlockSpec((1, ns), lambda g, i: (0, g)),
            pl.BlockSpec((1, ns), lambda g, i: (0, g)),
            pl.BlockSpec((1, LANES), lambda g, i: (0, 0)),
            pl.BlockSpec((1, LANES), lambda g, i: (0, 0)),
            pl.BlockSpec((1, gw), lambda g, i: (0, g)),
            pl.BlockSpec((1, gw), lambda g, i: (0, g)),
        ],
        out_specs=pl.BlockSpec((tc, gw), lambda g, i: (i, g)),
        out_shape=jax.ShapeDtypeStruct((t, SSM_D_INNER), BF16),
        scratch_shapes=[
            pltpu.VMEM((tc + 8, gw), F32),
            pltpu.VMEM((tc + 8, ns), F32),
            pltpu.VMEM((tc + 8, ns), F32),
            pltpu.VMEM((ns, gw), F32),
            pltpu.VMEM((LANES, SSM_CHUNK), F32),
        ],
        compiler_params=_params(2),
        name="ssd",
    )(proj, proj, proj, proj, small, cwx, cwb, cwc, cbx, cbb, cbc, dtb_row, alog_row, dsk_row, nw_row)


def _gdn_kernel(q_ref, k_ref, v_ref, z_ref, sm_ref, cwq_ref, cwk_ref, cwv_ref, dtb_ref, alog_ref,
                nw_ref, o_ref, qbuf, kbuf, vbuf, s_ref, *, tc, hps):
    h0 = pl.program_id(0) * hps
    C, DK, DV = GDN_CHUNK, GDN_HEAD_K, GDN_HEAD_V

    @pl.when(pl.program_id(1) == 0)
    def _():
        qbuf[pl.ds(0, 8), :] = jnp.zeros((8, hps * DK), F32)
        kbuf[pl.ds(0, 8), :] = jnp.zeros((8, hps * DK), F32)
        vbuf[pl.ds(0, 8), :] = jnp.zeros((8, hps * DV), F32)
        s_ref[...] = jnp.zeros_like(s_ref)

    q_cv = _silu(_conv_block(q_ref, qbuf, cwq_ref, tc))
    k_cv = _silu(_conv_block(k_ref, kbuf, cwk_ref, tc))
    v_cv = _silu(_conv_block(v_ref, vbuf, cwv_ref, tc))

    sm = sm_ref[...]
    lane = lax.broadcasted_iota(jnp.int32, sm.shape, 1)
    beta_full = jax.nn.sigmoid(sm)
    g_full = -jnp.exp(alog_ref[...]) * _softplus(sm + dtb_ref[...])
    q_hd, k_hd, v_hd, beta_hd, g_hd = [], [], [], [], []
    for hd in range(hps):
        q = q_cv[:, hd * DK:(hd + 1) * DK]
        k = k_cv[:, hd * DK:(hd + 1) * DK]
        q_hd.append(q * lax.rsqrt(jnp.sum(q * q, axis=-1, keepdims=True) + EPS) * (DK ** -0.5))
        k_hd.append(k * lax.rsqrt(jnp.sum(k * k, axis=-1, keepdims=True) + EPS))
        v_hd.append(v_cv[:, hd * DV:(hd + 1) * DV])
        beta_hd.append(jnp.sum(jnp.where(lane == SMALL_BETA + h0 + hd, beta_full, 0.0),
                               axis=1, keepdims=True))
        g_hd.append(jnp.sum(jnp.where(lane == SMALL_ALPHA + h0 + hd, g_full, 0.0),
                            axis=1, keepdims=True))

    row = lax.broadcasted_iota(jnp.int32, (C, C), 0)
    col = lax.broadcasted_iota(jnp.int32, (C, C), 1)
    causal = row >= col
    strict = row > col
    tri = causal.astype(F32)
    eye = (row == col).astype(F32)

    nc = tc // C
    items = [(hd, slice(c * C, (c + 1) * C)) for hd in range(hps) for c in range(nc)]
    rng = range(len(items))
    strict_w = (lax.broadcasted_iota(jnp.int32, (C, LANES), 0)
                > lax.broadcasted_iota(jnp.int32, (C, LANES), 1))
    qs = [q_hd[hd][sl] for hd, sl in items]
    ks = [k_hd[hd][sl] for hd, sl in items]
    vs = [v_hd[hd][sl] for hd, sl in items]
    betas = [beta_hd[hd][sl] for hd, sl in items]
    g_bc = [jnp.broadcast_to(g_hd[hd][sl], (C, LANES)) for hd, sl in items]
    cums = [_dot(tri, jnp.concatenate([g_bc[c], jnp.where(strict_w, g_bc[c], 0.0)], axis=1),
                 precision=HIGHEST) for c in rng]
    gc_bc = [x[:, :LANES] for x in cums]
    decay = [jnp.exp(jnp.where(causal, x[:, LANES:LANES + C], -jnp.inf)) for x in cums]
    k16 = [k.astype(BF16) for k in ks]
    q16 = [q.astype(BF16) for q in qs]
    kk = [_dot_nt(k16[c], k16[c]) for c in rng]
    qk = [_dot_nt(q16[c], k16[c]) for c in rng]
    p = [-jnp.where(strict, kk[c] * betas[c] * decay[c], 0.0) for c in rng]
    x = [eye + p[c] for c in rng]
    for _ in range(C.bit_length() - 2):
        p16 = [pp.astype(BF16) for pp in p]
        p = [_dot(p16[c], p16[c]) for c in rng]
        x = [x[c] + _dot(x[c].astype(BF16), p[c].astype(BF16)) for c in rng]
    t_inv = [xx.astype(BF16) for xx in x]
    egc = [jnp.exp(gc) for gc in gc_bc]
    u = [_dot(t_inv[c], (vs[c] * betas[c]).astype(BF16)).astype(BF16) for c in rng]
    w = [_dot(t_inv[c], (ks[c] * (betas[c] * egc[c])).astype(BF16)).astype(BF16) for c in rng]
    attn = [jnp.where(causal, qk[c] * decay[c], 0.0).astype(BF16) for c in rng]
    gc_last = [gc[C - 1:C, :] for gc in gc_bc]
    k_dec = [(ks[c] * jnp.exp(gc_last[c] - gc_bc[c])).astype(BF16) for c in rng]
    c_dec = [jnp.exp(gl[:, :1]) for gl in gc_last]
    a_mat = [_dot_tn(k_dec[c], w[c]).astype(BF16) for c in rng]
    n_mat = [_dot_tn(k_dec[c], u[c]) for c in rng]
    q_eff = [(qs[c] * egc[c] - _dot(attn[c], w[c])).astype(BF16) for c in rng]
    o_u = [_dot(attn[c], u[c]) for c in rng]

    s = [s_ref[hd] for hd in range(hps)]
    for c in range(nc):
        sl = slice(c * C, (c + 1) * C)
        for hd in range(hps):
            it = hd * nc + c
            s16 = s[hd].astype(BF16)
            o = _dot(q_eff[it], s16) + o_u[it]
            s[hd] = s[hd] * c_dec[it] - _dot(a_mat[it], s16) + n_mat[it]
            o = o * lax.rsqrt(jnp.mean(o * o, axis=-1, keepdims=True) + EPS) * nw_ref[...]
            z = z_ref[sl, hd * DV:(hd + 1) * DV].astype(F32)
            o_ref[sl, hd * DV:(hd + 1) * DV] = (o * _silu(z)).astype(BF16)
    for hd in range(hps):
        s_ref[hd] = s[hd]


def _gdn(proj, small, cwq, cwk, cwv, dtb_row, alog_row, nw_row, tc=256, hps=2):
    t = proj.shape[0]
    tc = min(tc, t)
    dk, dv = hps * GDN_HEAD_K, hps * GDN_HEAD_V
    q_blk = COL_GDN_QKV // dk
    k_blk = q_blk + GDN_N_HEADS // hps
    v_blk = (COL_GDN_QKV + 2 * GDN_KEY_DIM) // dv
    z_blk = COL_GDN_Z // dv
    kern = functools.partial(_gdn_kernel, tc=tc, hps=hps)
    return pl.pallas_call(
        kern,
        grid=(GDN_N_HEADS // hps, t // tc),
        in_specs=[
            pl.BlockSpec((tc, dk), lambda h, i: (i, q_blk + h)),
            pl.BlockSpec((tc, dk), lambda h, i: (i, k_blk + h)),
            pl.BlockSpec((tc, dv), lambda h, i: (i, v_blk + h)),
            pl.BlockSpec((tc, dv), lambda h, i: (i, z_blk + h)),
            pl.BlockSpec((tc, LANES), lambda h, i: (i, 0)),
            pl.BlockSpec((CONV_K, dk), lambda h, i: (0, h)),
            pl.BlockSpec((CONV_K, dk), lambda h, i: (0, h)),
            pl.BlockSpec((CONV_K, dv), lambda h, i: (0, h)),
            pl.BlockSpec((1, LANES), lambda h, i: (0, 0)),
            pl.BlockSpec((1, LANES), lambda h, i: (0, 0)),
            pl.BlockSpec((1, GDN_HEAD_V), lambda h, i: (0, 0)),
        ],
        out_specs=pl.BlockSpec((tc, dv), lambda h, i: (i, h)),
        out_shape=jax.ShapeDtypeStruct((t, GDN_VAL_DIM), BF16),
        scratch_shapes=[
            pltpu.VMEM((tc + 8, dk), F32),
            pltpu.VMEM((tc + 8, dk), F32),
            pltpu.VMEM((tc + 8, dv), F32),
            pltpu.VMEM((hps, GDN_HEAD_K, GDN_HEAD_V), F32),
        ],
        compiler_params=_params(2),
        name="gdn",
    )(proj, proj, proj, proj, small, cwq, cwk, cwv, dtb_row, alog_row, nw_row)


def _mix_kernel(ys_ref, yg_ref, g0_ref, g1_ref, gb_ref, ws_ref, wg_ref, o_ref):
    a = _dot(ys_ref[...], ws_ref[...])
    b = _dot(yg_ref[...], wg_ref[...])
    g0 = jax.nn.sigmoid(g0_ref[...].astype(F32) + gb_ref[0:1, :])
    g1 = jax.nn.sigmoid(g1_ref[...].astype(F32) + gb_ref[1:2, :])
    o_ref[...] = (g0 * a + g1 * b).astype(BF16)


def _mix(y_ssm, y_gdn, proj, gate_b, w_s, w_g, tm=512, tn=512):
    t = y_ssm.shape[0]
    tm = min(tm, t)
    g0_blk = COL_GATE // tn
    g1_blk = (COL_GATE + D_MODEL) // tn
    return pl.pallas_call(
        _mix_kernel,
        grid=(D_MODEL // tn, t // tm),
        in_specs=[
            pl.BlockSpec((tm, SSM_D_INNER), lambda n, i: (i, 0)),
            pl.BlockSpec((tm, GDN_VAL_DIM), lambda n, i: (i, 0)),
            pl.BlockSpec((tm, tn), lambda n, i: (i, g0_blk + n)),
            pl.BlockSpec((tm, tn), lambda n, i: (i, g1_blk + n)),
            pl.BlockSpec((2, tn), lambda n, i: (0, n)),
            pl.BlockSpec((SSM_D_INNER, tn), lambda n, i: (0, n)),
            pl.BlockSpec((GDN_VAL_DIM, tn), lambda n, i: (0, n)),
        ],
        out_specs=pl.BlockSpec((tm, tn), lambda n, i: (i, n)),
        out_shape=jax.ShapeDtypeStruct((t, D_MODEL), BF16),
        compiler_params=_params(2),
        name="mix",
    )(y_ssm, y_gdn, proj, proj, gate_b, w_s, w_g)


_N_RANK = PEER_TOPK + 1
_CAND_PAIRS = [(i, j) for i in range(_N_RANK) for j in range(_N_RANK) if (i + 1) * (j + 1) <= _N_RANK]
_CAND_ROWS = -(-len(_CAND_PAIRS) // 8) * 8


def _top_desc(s, n):
    out = []
    for _ in range(n):
        m = jnp.max(s, axis=0, keepdims=True)
        out.append(m)
        s = jnp.where(s == m, -jnp.inf, s)
    return out


def _post_kernel(x_ref, mix_ref, wo_ref, fw_ref, wq_ref, sk_ref,
                 h1_ref, hnt_ref, s2_ref, e2_ref, th_ref, e1_ref, cand_ref, *, tm):
    h1 = x_ref[...] + _dot(mix_ref[...], wo_ref[...])
    h1_ref[...] = h1
    hn = h1 * lax.rsqrt(jnp.mean(h1 * h1, axis=-1, keepdims=True) + EPS) * fw_ref[...]
    hnt_ref[...] = hn.T.astype(BF16)
    qv = _dot(hn.astype(BF16), wq_ref[...]).astype(BF16)
    nchunk = tm // LANES
    for h in range(PEER_HEADS):
        s1 = _dot_nt(sk_ref[2 * h], qv[:, (2 * h) * LANES:(2 * h + 1) * LANES])
        s2 = _dot_nt(sk_ref[2 * h + 1], qv[:, (2 * h + 1) * LANES:(2 * h + 2) * LANES])
        a1 = _top_desc(s1, _N_RANK)
        a2 = _top_desc(s2, _N_RANK)
        cand_ref[...] = jnp.full(cand_ref.shape, -jnp.inf, F32)
        for r, (i, j) in enumerate(_CAND_PAIRS):
            cand_ref[pl.ds(r, 1), :] = a1[i] + a2[j]
        cand = cand_ref[...]
        c = cand
        n_removed = jnp.zeros((1, tm), F32)
        v16 = jnp.full((1, tm), -jnp.inf, F32)
        v17 = jnp.full((1, tm), -jnp.inf, F32)
        for _ in range(_N_RANK):
            m = jnp.max(c, axis=0, keepdims=True)
            eq = c == m
            cnt = jnp.sum(eq.astype(F32), axis=0, keepdims=True)
            v16 = jnp.where(n_removed < PEER_TOPK, m, v16)
            v17 = jnp.where(n_removed < _N_RANK, m, v17)
            n_removed = n_removed + cnt
            c = jnp.where(eq, -jnp.inf, c)
        tau = 0.5 * (v16 + v17)
        tau = jnp.where(v17 == -jnp.inf, v16, tau)
        m_tot = a1[0] + a2[0]
        z = jnp.sum(jnp.where(cand >= tau, jnp.exp(cand - m_tot), 0.0), axis=0, keepdims=True)
        e2 = jnp.exp(s2 - a2[0]) / z
        e1 = jnp.exp(s1 - a1[0])
        th = tau - s1
        for cc in range(nchunk):
            ls = slice(cc * LANES, (cc + 1) * LANES)
            s2_ref[cc, h] = s2[:, ls]
            e2_ref[cc, h] = e2[:, ls]
            th_ref[cc, h] = th[:, ls]
            e1_ref[cc, h] = e1[:, ls]


def _post(x, mix, w_out, ffn_w, w_q, sub_keys, tm=256):
    t, d = x.shape
    tm = min(tm, t)
    nchunk = tm // LANES
    stat = jax.ShapeDtypeStruct((t // LANES, PEER_HEADS, PEER_N_KEYS, LANES), F32)
    stat_spec = pl.BlockSpec((nchunk, PEER_HEADS, PEER_N_KEYS, LANES), lambda i: (i, 0, 0, 0))
    kern = functools.partial(_post_kernel, tm=tm)
    return pl.pallas_call(
        kern,
        grid=(t // tm,),
        in_specs=[
            pl.BlockSpec((tm, d), lambda i: (i, 0)),
            pl.BlockSpec((tm, d), lambda i: (i, 0)),
            pl.BlockSpec((d, d), lambda i: (0, 0)),
            pl.BlockSpec((1, d), lambda i: (0, 0)),
            pl.BlockSpec((d, d), lambda i: (0, 0)),
            pl.BlockSpec((2 * PEER_HEADS, PEER_N_KEYS, LANES), lambda i: (0, 0, 0)),
        ],
        out_specs=[
            pl.BlockSpec((tm, d), lambda i: (i, 0)),
            pl.BlockSpec((d, tm), lambda i: (0, i)),
            stat_spec, stat_spec, stat_spec, stat_spec,
        ],
        out_shape=[
            jax.ShapeDtypeStruct((t, d), F32),
            jax.ShapeDtypeStruct((d, t), BF16),
            stat, stat, stat, stat,
        ],
        scratch_shapes=[pltpu.VMEM((_CAND_ROWS, tm), F32)],
        compiler_params=_params(1),
        name="post",
    )(x, mix, w_out, ffn_w, w_q, sub_keys)


_ROW_TILE = 64


def _gelu(x):
    return 0.5 * x * (1.0 + lax.erf(x * (2.0 ** -0.5)))


def _peer_kernel(hnt_ref, u_ref, vt_ref, s2_ref, e2_ref, th_ref, e1_ref, acc_ref, pt0_ref, pt1_ref,
                 *, tb, eb):
    n1 = eb // PEER_N_KEYS

    @pl.when(pl.program_id(1) == 0)
    def _():
        acc_ref[...] = jnp.zeros_like(acc_ref)

    def weights_times_act(at, pt_ref, row0):
        for cc in range(tb // LANES):
            ls = slice(cc * LANES, (cc + 1) * LANES)
            for rt in range(PEER_N_KEYS // _ROW_TILE):
                rs = slice(rt * _ROW_TILE, (rt + 1) * _ROW_TILE)
                w = [jnp.zeros((_ROW_TILE, LANES), F32) for _ in range(n1)]
                for h in range(PEER_HEADS):
                    s2 = s2_ref[cc, h, rs, :]
                    e2 = e2_ref[cc, h, rs, :]
                    for i in range(n1):
                        th = th_ref[cc, h, row0 + i:row0 + i + 1, :]
                        e1 = e1_ref[cc, h, row0 + i:row0 + i + 1, :]
                        w[i] = w[i] + jnp.where(s2 >= th, e2, 0.0) * e1
                for i in range(n1):
                    r0 = i * PEER_N_KEYS + rt * _ROW_TILE
                    a = at[r0:r0 + _ROW_TILE, ls]
                    pt_ref[r0:r0 + _ROW_TILE, ls] = (w[i] * _gelu(a)).astype(BF16)

    hnt = hnt_ref[...]
    at0 = _dot(u_ref[0:eb, :], hnt)
    at1 = _dot(u_ref[eb:2 * eb, :], hnt)
    weights_times_act(at0, pt0_ref, 0)
    acc_ref[...] += _dot(vt_ref[:, 0:eb], pt0_ref[...])
    weights_times_act(at1, pt1_ref, n1)
    acc_ref[...] += _dot(vt_ref[:, eb:2 * eb], pt1_ref[...])


def _peer(hnt, u16, v16, s2, e2, th, e1, tb=512, eb=512):
    d, t = hnt.shape
    tb = min(tb, t)
    nchunk = tb // LANES
    n1 = eb // PEER_N_KEYS
    nk = PEER_N_EXPERTS // (2 * eb)
    vt_slabs = jnp.swapaxes(v16.reshape(nk, 2 * eb, d), 1, 2)
    assert 2 * n1 == 8, "a pair of expert blocks spans one 8-row group of first-key statistics"
    stat_spec = pl.BlockSpec((nchunk, PEER_HEADS, PEER_N_KEYS, LANES), lambda i, k: (i, 0, 0, 0))
    row_spec = pl.BlockSpec((nchunk, PEER_HEADS, 8, LANES), lambda i, k: (i, 0, k, 0))
    kern = functools.partial(_peer_kernel, tb=tb, eb=eb)
    return pl.pallas_call(
        kern,
        grid=(t // tb, nk),
        in_specs=[
            pl.BlockSpec((d, tb), lambda i, k: (0, i)),
            pl.BlockSpec((2 * eb, d), lambda i, k: (k, 0)),
            pl.BlockSpec((None, d, 2 * eb), lambda i, k: (k, 0, 0)),
            stat_spec, stat_spec, row_spec, row_spec,
        ],
        out_specs=pl.BlockSpec((d, tb), lambda i, k: (0, i)),
        out_shape=jax.ShapeDtypeStruct((d, t), F32),
        scratch_shapes=[pltpu.VMEM((eb, tb), BF16), pltpu.VMEM((eb, tb), BF16)],
        compiler_params=_params(2),
        name="peer",
    )(hnt, u16, vt_slabs, s2, e2, th, e1)


def _final_kernel(h_ref, pt_ref, w_ref, o_ref):
    hf = h_ref[...] + pt_ref[...].T
    o_ref[...] = hf * lax.rsqrt(jnp.mean(hf * hf, axis=-1, keepdims=True) + EPS) * w_ref[...]


def _final(h1, peer_out_t, w_row, tm=512):
    t, d = h1.shape
    tm = min(tm, t)
    spec = pl.BlockSpec((tm, d), lambda i: (i, 0))
    return pl.pallas_call(
        _final_kernel,
        grid=(t // tm,),
        in_specs=[spec, pl.BlockSpec((d, tm), lambda i: (0, i)), pl.BlockSpec((1, d), lambda i: (0, 0))],
        out_specs=spec,
        out_shape=jax.ShapeDtypeStruct((t, d), F32),
        compiler_params=_params(1),
        name="final",
    )(h1, peer_out_t, w_row)


def _row(v, width=None):
    v = v.astype(F32).reshape(1, -1)
    if width is not None and v.shape[1] < width:
        v = jnp.pad(v, ((0, 0), (0, width - v.shape[1])))
    return v


def _layer(h, mix_norm_w, w_in, gate_b, ssm_conv_w, ssm_conv_b, ssm_dt_bias, ssm_a_log, ssm_d,
           ssm_norm_w, gdn_conv_w, gdn_dt_bias, gdn_a_log, gdn_norm_w, w_branch_ssm, w_branch_gdn,
           w_out, ffn_norm_w, peer_w_q, peer_sub_keys, peer_u, peer_v, out_norm_w):
    ssm_proj = SSM_D_INNER + (SSM_D_INNER + 2 * SSM_N_GROUPS * SSM_D_STATE) + SSM_N_HEADS
    gdn_conv_dim = 2 * GDN_KEY_DIM + GDN_VAL_DIM
    o_dt = ssm_proj - SSM_N_HEADS
    o_qkv = ssm_proj
    o_gz = o_qkv + gdn_conv_dim
    o_beta = o_gz + GDN_VAL_DIM
    o_gate = o_beta + 2 * GDN_N_HEADS
    w_main = jnp.concatenate(
        [w_in[:, :o_dt], w_in[:, o_qkv:o_beta], w_in[:, o_gate:]], axis=1).astype(BF16)
    w_small = jnp.concatenate(
        [w_in[:, o_dt:o_qkv], w_in[:, o_beta:o_gate],
         jnp.zeros((D_MODEL, LANES - SSM_N_HEADS - 2 * GDN_N_HEADS), w_in.dtype)], axis=1).astype(BF16)

    proj, small = _inproj(h, _row(mix_norm_w), w_main, w_small)

    cw = ssm_conv_w.astype(F32)
    cbias = _row(ssm_conv_b)
    nx = SSM_D_INNER
    nb = SSM_N_GROUPS * SSM_D_STATE
    y_ssm = _ssd(proj, small,
                 cw[:, :nx], cw[:, nx:nx + nb], cw[:, nx + nb:],
                 cbias[:, :nx], cbias[:, nx:nx + nb], cbias[:, nx + nb:],
                 _row(ssm_dt_bias, LANES), _row(ssm_a_log, LANES),
                 _row(jnp.repeat(ssm_d, SSM_HEAD_DIM)), _row(ssm_norm_w))

    gw = gdn_conv_w.astype(F32)
    zeros_b = jnp.zeros((SMALL_ALPHA,), F32)
    y_gdn = _gdn(proj, small,
                 gw[:, :GDN_KEY_DIM], gw[:, GDN_KEY_DIM:2 * GDN_KEY_DIM], gw[:, 2 * GDN_KEY_DIM:],
                 _row(jnp.concatenate([zeros_b, gdn_dt_bias.astype(F32)]), LANES),
                 _row(jnp.concatenate([zeros_b, gdn_a_log.astype(F32)]), LANES),
                 _row(gdn_norm_w))

    mix = _mix(y_ssm, y_gdn, proj, gate_b.astype(F32),
               w_branch_ssm.astype(BF16), w_branch_gdn.astype(BF16))

    sk = peer_sub_keys.reshape(2 * PEER_HEADS, PEER_N_KEYS, LANES).astype(BF16)
    h1, hnt, s2, e2, th, e1 = _post(h, mix, w_out.astype(BF16), _row(ffn_norm_w),
                                    peer_w_q.astype(BF16), sk)

    peer_out_t = _peer(hnt, peer_u.astype(BF16), peer_v.astype(BF16), s2, e2, th, e1)
    return _final(h1, peer_out_t, _row(out_norm_w))


def kernel(x, mix_norm_w, w_in, gate_b, ssm_conv_w, ssm_conv_b, ssm_dt_bias, ssm_a_log, ssm_d,
           ssm_norm_w, gdn_conv_w, gdn_dt_bias, gdn_a_log, gdn_norm_w, w_branch_ssm, w_branch_gdn,
           w_out, ffn_norm_w, peer_w_q, peer_sub_keys, peer_u, peer_v, final_norm_w):
    b, t, d = x.shape
    assert b == 1 and mix_norm_w.shape[0] == 1, "single sequence, single layer"
    out = _layer(x[0], mix_norm_w[0], w_in[0], gate_b[0], ssm_conv_w[0], ssm_conv_b[0],
                 ssm_dt_bias[0], ssm_a_log[0], ssm_d[0], ssm_norm_w[0], gdn_conv_w[0],
                 gdn_dt_bias[0], gdn_a_log[0], gdn_norm_w[0], w_branch_ssm[0], w_branch_gdn[0],
                 w_out[0], ffn_norm_w[0], peer_w_q[0], peer_sub_keys[0], peer_u[0], peer_v[0],
                 final_norm_w)
    return out[None]
```

```python
import functools

import jax
import jax.numpy as jnp
from jax import lax
from jax.experimental import pallas as pl
from jax.experimental.pallas import tpu as pltpu

F32 = jnp.float32
BF16 = jnp.bfloat16
HIGHEST = lax.Precision.HIGHEST

EPS = 1e-6
D_MODEL = 2048
LANES = 128
SSM_D_INNER = 4096
SSM_HEAD_DIM = 64
SSM_N_HEADS = 64
SSM_N_GROUPS = 8
SSM_HPG = 8
SSM_D_STATE = 128
SSM_CHUNK = 128
SSM_GROUP_W = SSM_HPG * SSM_HEAD_DIM
GDN_N_HEADS = 16
GDN_HEAD_K = 128
GDN_HEAD_V = 256
GDN_CHUNK = 64
GDN_KEY_DIM = GDN_N_HEADS * GDN_HEAD_K
GDN_VAL_DIM = GDN_N_HEADS * GDN_HEAD_V
CONV_K = 4
PEER_HEADS = 8
PEER_N_KEYS = 128
PEER_TOPK = 16
PEER_N_EXPERTS = PEER_N_KEYS * PEER_N_KEYS

COL_SSM_Z = 0
COL_SSM_XBC = 4096
COL_GDN_QKV = 10240
COL_GDN_Z = 18432
COL_GATE = 22528
N_MAIN = 26624
SMALL_BETA = 64
SMALL_ALPHA = 80

VMEM_LIMIT = 56 * 1024 * 1024


def _softplus(x):
    return jnp.maximum(x, 0.0) + jnp.log1p(jnp.exp(-jnp.abs(x)))


def _silu(x):
    return x * jax.nn.sigmoid(x)


def _dot(a, b, **kw):
    return jnp.dot(a, b, preferred_element_type=F32, **kw)


def _dot_nt(a, b, **kw):
    return lax.dot_general(a, b, (((1,), (1,)), ((), ())), preferred_element_type=F32, **kw)


def _dot_tn(a, b, **kw):
    return lax.dot_general(a, b, (((0,), (0,)), ((), ())), preferred_element_type=F32, **kw)


def _params(n_grid):
    return pltpu.CompilerParams(dimension_semantics=("arbitrary",) * n_grid,
                                vmem_limit_bytes=VMEM_LIMIT)


def _inproj_kernel(x_ref, nw_ref, w_ref, ws_ref, o_ref, os_ref, xn_ref):
    @pl.when(pl.program_id(1) == 0)
    def _():
        x = x_ref[...]
        ms = jnp.mean(x * x, axis=-1, keepdims=True)
        xn = (x * lax.rsqrt(ms + EPS) * nw_ref[...]).astype(BF16)
        xn_ref[...] = xn
        os_ref[...] = _dot(xn, ws_ref[...])

    o_ref[...] = _dot(xn_ref[...], w_ref[...]).astype(BF16)


def _inproj(x, norm_w, w_main, w_small, tm=1024, tn=1024):
    t, d = x.shape
    n = w_main.shape[1]
    tm = min(tm, t)
    return pl.pallas_call(
        _inproj_kernel,
        grid=(t // tm, n // tn),
        in_specs=[
            pl.BlockSpec((tm, d), lambda i, j: (i, 0)),
            pl.BlockSpec((1, d), lambda i, j: (0, 0)),
            pl.BlockSpec((d, tn), lambda i, j: (0, j)),
            pl.BlockSpec((d, LANES), lambda i, j: (0, 0)),
        ],
        out_specs=[
            pl.BlockSpec((tm, tn), lambda i, j: (i, j)),
            pl.BlockSpec((tm, LANES), lambda i, j: (i, 0)),
        ],
        out_shape=[jax.ShapeDtypeStruct((t, n), BF16), jax.ShapeDtypeStruct((t, LANES), F32)],
        scratch_shapes=[pltpu.VMEM((tm, d), BF16)],
        compiler_params=_params(2),
        name="inproj",
    )(x, norm_w, w_main, w_small)


def _shift_mats(tc):
    return jnp.concatenate([jnp.eye(tc, k=-(CONV_K - 1 - k), dtype=BF16) for k in range(CONV_K - 1)], axis=0)


def _conv_block(raw_ref, carry_ref, w_ref, shift_ref, tc):
    x16 = raw_ref[...]
    xf = x16.astype(F32)
    y = xf * w_ref[CONV_K - 1:CONV_K, :]
    c8 = carry_ref[...]
    row8 = lax.broadcasted_iota(jnp.int32, c8.shape, 0)
    head = jnp.zeros_like(c8)
    shifted = _dot(shift_ref[...], x16)
    for k in range(CONV_K - 1):
        lag = CONV_K - 1 - k
        y = y + shifted[k * tc:(k + 1) * tc] * w_ref[k:k + 1, :]
        head = head + jnp.where(row8 < lag, pltpu.roll(c8, shift=lag, axis=0), 0.0) * w_ref[k:k + 1, :]
    carry_ref[...] = xf[tc - 8:tc]
    return jnp.concatenate([y[0:8] + head, y[8:]], axis=0)


def _ssd_kernel(xs_ref, b_ref, c_ref, z_ref, sm_ref, cwx_ref, cwb_ref, cwc_ref, cbx_ref, cbb_ref,
                cbc_ref, dtb_ref, alog_ref, dsk_ref, nw_ref, sh_ref, o_ref,
                xbuf, bbuf, cbuf, state_ref, acst_ref, *, tc):
    g = pl.program_id(0)
    L = SSM_CHUNK

    @pl.when(pl.program_id(1) == 0)
    def _():
        xbuf[...] = jnp.zeros_like(xbuf)
        bbuf[...] = jnp.zeros_like(bbuf)
        cbuf[...] = jnp.zeros_like(cbuf)
        state_ref[...] = jnp.zeros_like(state_ref)

    xs_all = _silu(_conv_block(xs_ref, xbuf, cwx_ref, sh_ref, tc) + cbx_ref[...])
    b_all = _silu(_conv_block(b_ref, bbuf, cwb_ref, sh_ref, tc) + cbb_ref[...])
    c_all = _silu(_conv_block(c_ref, cbuf, cwc_ref, sh_ref, tc) + cbc_ref[...])

    row = lax.broadcasted_iota(jnp.int32, (L, L), 0)
    col = lax.broadcasted_iota(jnp.int32, (L, L), 1)
    causal = row >= col
    tri = causal.astype(F32)
    lo_half = col < SSM_HEAD_DIM
    a_row = -jnp.exp(alog_ref[...])

    for c in range(tc // L):
        sl = slice(c * L, (c + 1) * L)
        xs, bm, cm = xs_all[sl], b_all[sl], c_all[sl]
        dt_full = _softplus(sm_ref[sl, :] + dtb_ref[...])
        acs_full = _dot(tri, dt_full * a_row, precision=HIGHEST)
        acst_ref[...] = acs_full.T
        bm16, cm16 = bm.astype(BF16), cm.astype(BF16)
        cb = _dot_nt(cm16, bm16)
        ys = []
        for p in range(SSM_HPG // 2):
            acs_bc, dt_bc, scores = [], [], []
            for r in (2 * p, 2 * p + 1):
                j = g * SSM_HPG + r
                sel = col == j
                a_c = jnp.sum(jnp.where(sel, acs_full, 0.0), axis=1, keepdims=True)
                d_c = jnp.sum(jnp.where(sel, dt_full, 0.0), axis=1, keepdims=True)
                a_b = jnp.broadcast_to(a_c, (L, L))
                acs_bc.append(a_b)
                dt_bc.append(jnp.broadcast_to(d_c, (L, L)))
                a_r = acst_ref[pl.ds(j, 1), :]
                seg = jnp.where(causal, a_b - a_r, -jnp.inf)
                scores.append((cb * jnp.exp(seg)).astype(BF16))
            acs_e = jnp.where(lo_half, acs_bc[0], acs_bc[1])
            dt_e = jnp.where(lo_half, dt_bc[0], dt_bc[1])
            lanes = slice(p * LANES, (p + 1) * LANES)
            x_p = xs[:, lanes]
            xdt = x_p * dt_e
            last = acs_e[L - 1:L, :]
            xdtw = (xdt * jnp.exp(last - acs_e)).astype(BF16)
            s2 = jnp.concatenate(scores, axis=1)
            x2 = jnp.concatenate([jnp.where(lo_half, xdt, 0.0), jnp.where(lo_half, 0.0, xdt)],
                                 axis=0).astype(BF16)
            y_diag = _dot(s2, x2)
            st = state_ref[:, lanes]
            y_off = _dot(cm16, st.astype(BF16)) * jnp.exp(acs_e)
            state_ref[:, lanes] = st * jnp.exp(last) + _dot_tn(bm16, xdtw)
            ys.append(y_diag + y_off + x_p * dsk_ref[:, lanes])
        y = jnp.concatenate(ys, axis=1) * _silu(z_ref[sl, :].astype(F32))
        y = y * lax.rsqrt(jnp.mean(y * y, axis=-1, keepdims=True) + EPS)
        o_ref[sl, :] = (y * nw_ref[...]).astype(BF16)


def _ssd(proj, small, cwx, cwb, cwc, cbx, cbb, cbc, dtb_row, alog_row, dsk_row, nw_row, tc=256):
    t = proj.shape[0]
    tc = min(tc, t)
    gw, ns = SSM_GROUP_W, SSM_D_STATE
    xs_blk = COL_SSM_XBC // gw
    b_blk = (COL_SSM_XBC + SSM_D_INNER) // ns
    c_blk = b_blk + SSM_N_GROUPS
    z_blk = COL_SSM_Z // gw
    kern = functools.partial(_ssd_kernel, tc=tc)
    return pl.pallas_call(
        kern,
        grid=(SSM_N_GROUPS, t // tc),
        in_specs=[
            pl.BlockSpec((tc, gw), lambda g, i: (i, xs_blk + g)),
            pl.BlockSpec((tc, ns), lambda g, i: (i, b_blk + g)),
            pl.BlockSpec((tc, ns), lambda g, i: (i, c_blk + g)),
            pl.BlockSpec((tc, gw), lambda g, i: (i, z_blk + g)),
            pl.BlockSpec((tc, LANES), lambda g, i: (i, 0)),
            pl.BlockSpec((CONV_K, gw), lambda g, i: (0, g)),
            pl.BlockSpec((CONV_K, ns), lambda g, i: (0, g)),
            pl.BlockSpec((CONV_K, ns), lambda g, i: (0, g)),
            pl.BlockSpec((1, gw), lambda g, i: (0, g)),
            pl.BlockSpec((1, ns), lambda g, i: (0, g)),
            pl.BlockSpec((1, ns), lambda g, i: (0, g)),
            pl.BlockSpec((1, LANES), lambda g, i: (0, 0)),
            pl.BlockSpec((1, LANES), lambda g, i: (0, 0)),
            pl.BlockSpec((1, gw), lambda g, i: (0, g)),
            pl.BlockSpec((1, gw), lambda g, i: (0, g)),
            pl.BlockSpec(((CONV_K - 1) * tc, tc), lambda g, i: (0, 0)),
        ],
        out_specs=pl.BlockSpec((tc, gw), lambda g, i: (i, g)),
        out_shape=jax.ShapeDtypeStruct((t, SSM_D_INNER), BF16),
        scratch_shapes=[
            pltpu.VMEM((8, gw), F32),
            pltpu.VMEM((8, ns), F32),
            pltpu.VMEM((8, ns), F32),
            pltpu.VMEM((ns, gw), F32),
            pltpu.VMEM((LANES, SSM_CHUNK), F32),
        ],
        compiler_params=_params(2),
        name="ssd",
    )(proj, proj, proj, proj, small, cwx, cwb, cwc, cbx, cbb, cbc, dtb_row, alog_row, dsk_row, nw_row,
      _shift_mats(tc))


def _gdn_kernel(q_ref, k_ref, v_ref, z_ref, sm_ref, cwq_ref, cwk_ref, cwv_ref, dtb_ref, alog_ref,
                nw_ref, sh_ref, o_ref, qbuf, kbuf, vbuf, s_ref, *, tc, hps):
    h0 = pl.program_id(0) * hps
    C, DK, DV = GDN_CHUNK, GDN_HEAD_K, GDN_HEAD_V

    @pl.when(pl.program_id(1) == 0)
    def _():
        qbuf[...] = jnp.zeros_like(qbuf)
        kbuf[...] = jnp.zeros_like(kbuf)
        vbuf[...] = jnp.zeros_like(vbuf)
        s_ref[...] = jnp.zeros_like(s_ref)

    q_cv = _silu(_conv_block(q_ref, qbuf, cwq_ref, sh_ref, tc))
    k_cv = _silu(_conv_block(k_ref, kbuf, cwk_ref, sh_ref, tc))
    v_cv = _silu(_conv_block(v_ref, vbuf, cwv_ref, sh_ref, tc))

    sm = sm_ref[...]
    lane = lax.broadcasted_iota(jnp.int32, sm.shape, 1)
    beta_full = jax.nn.sigmoid(sm)
    g_full = -jnp.exp(alog_ref[...]) * _softplus(sm + dtb_ref[...])
    q_hd, k_hd, v_hd, beta_hd, g_hd = [], [], [], [], []
    for hd in range(hps):
        q = q_cv[:, hd * DK:(hd + 1) * DK]
        k = k_cv[:, hd * DK:(hd + 1) * DK]
        q_hd.append(q * lax.rsqrt(jnp.sum(q * q, axis=-1, keepdims=True) + EPS) * (DK ** -0.5))
        k_hd.append(k * lax.rsqrt(jnp.sum(k * k, axis=-1, keepdims=True) + EPS))
        v_hd.append(v_cv[:, hd * DV:(hd + 1) * DV])
        beta_hd.append(jnp.sum(jnp.where(lane == SMALL_BETA + h0 + hd, beta_full, 0.0),
                               axis=1, keepdims=True))
        g_hd.append(jnp.sum(jnp.where(lane == SMALL_ALPHA + h0 + hd, g_full, 0.0),
                            axis=1, keepdims=True))

    row = lax.broadcasted_iota(jnp.int32, (C, C), 0)
    col = lax.broadcasted_iota(jnp.int32, (C, C), 1)
    causal = row >= col
    strict = row > col
    tri = causal.astype(F32)
    eye = (row == col).astype(F32)

    nc = tc // C
    items = [(hd, slice(c * C, (c + 1) * C)) for hd in range(hps) for c in range(nc)]
    rng = range(len(items))
    strict_w = (lax.broadcasted_iota(jnp.int32, (C, LANES), 0)
                > lax.broadcasted_iota(jnp.int32, (C, LANES), 1))
    qs = [q_hd[hd][sl] for hd, sl in items]
    ks = [k_hd[hd][sl] for hd, sl in items]
    vs = [v_hd[hd][sl] for hd, sl in items]
    betas = [beta_hd[hd][sl] for hd, sl in items]
    g_bc = [jnp.broadcast_to(g_hd[hd][sl], (C, LANES)) for hd, sl in items]
    cums = [_dot(tri, jnp.concatenate([g_bc[c], jnp.where(strict_w, g_bc[c], 0.0)], axis=1),
                 precision=HIGHEST) for c in rng]
    gc_bc = [x[:, :LANES] for x in cums]
    decay = [jnp.exp(jnp.where(causal, x[:, LANES:LANES + C], -jnp.inf)) for x in cums]
    k16 = [k.astype(BF16) for k in ks]
    q16 = [q.astype(BF16) for q in qs]
    kk = [_dot_nt(k16[c], k16[c]) for c in rng]
    qk = [_dot_nt(q16[c], k16[c]) for c in rng]
    p = [-jnp.where(strict, kk[c] * betas[c] * decay[c], 0.0) for c in rng]
    x = [eye + p[c] for c in rng]
    for _ in range(C.bit_length() - 2):
        p16 = [pp.astype(BF16) for pp in p]
        p = [_dot(p16[c], p16[c]) for c in rng]
        x = [x[c] + _dot(x[c].astype(BF16), p[c].astype(BF16)) for c in rng]
    t_inv = [xx.astype(BF16) for xx in x]
    egc = [jnp.exp(gc) for gc in gc_bc]
    u = [_dot(t_inv[c], (vs[c] * betas[c]).astype(BF16)).astype(BF16) for c in rng]
    w = [_dot(t_inv[c], (ks[c] * (betas[c] * egc[c])).astype(BF16)).astype(BF16) for c in rng]
    attn = [jnp.where(causal, qk[c] * decay[c], 0.0).astype(BF16) for c in rng]
    gc_last = [gc[C - 1:C, :] for gc in gc_bc]
    k_dec = [(ks[c] * jnp.exp(gc_last[c] - gc_bc[c])).astype(BF16) for c in rng]
    c_dec = [jnp.exp(gl[:, :1]) for gl in gc_last]
    a_mat = [_dot_tn(k_dec[c], w[c]).astype(BF16) for c in rng]
    n_mat = [_dot_tn(k_dec[c], u[c]) for c in rng]
    q_eff = [(qs[c] * egc[c] - _dot(attn[c], w[c])).astype(BF16) for c in rng]
    o_u = [_dot(attn[c], u[c]) for c in rng]

    s = [s_ref[hd] for hd in range(hps)]
    for c in range(nc):
        sl = slice(c * C, (c + 1) * C)
        for hd in range(hps):
            it = hd * nc + c
            s16 = s[hd].astype(BF16)
            o = _dot(q_eff[it], s16) + o_u[it]
            s[hd] = s[hd] * c_dec[it] - _dot(a_mat[it], s16) + n_mat[it]
            o = o * lax.rsqrt(jnp.mean(o * o, axis=-1, keepdims=True) + EPS) * nw_ref[...]
            z = z_ref[sl, hd * DV:(hd + 1) * DV].astype(F32)
            o_ref[sl, hd * DV:(hd + 1) * DV] = (o * _silu(z)).astype(BF16)
    for hd in range(hps):
        s_ref[hd] = s[hd]


def _gdn(proj, small, cwq, cwk, cwv, dtb_row, alog_row, nw_row, tc=256, hps=4):
    t = proj.shape[0]
    tc = min(tc, t)
    dk, dv = hps * GDN_HEAD_K, hps * GDN_HEAD_V
    q_blk = COL_GDN_QKV // dk
    k_blk = q_blk + GDN_N_HEADS // hps
    v_blk = (COL_GDN_QKV + 2 * GDN_KEY_DIM) // dv
    z_blk = COL_GDN_Z // dv
    kern = functools.partial(_gdn_kernel, tc=tc, hps=hps)
    return pl.pallas_call(
        kern,
        grid=(GDN_N_HEADS // hps, t // tc),
        in_specs=[
            pl.BlockSpec((tc, dk), lambda h, i: (i, q_blk + h)),
            pl.BlockSpec((tc, dk), lambda h, i: (i, k_blk + h)),
            pl.BlockSpec((tc, dv), lambda h, i: (i, v_blk + h)),
            pl.BlockSpec((tc, dv), lambda h, i: (i, z_blk + h)),
            pl.BlockSpec((tc, LANES), lambda h, i: (i, 0)),
            pl.BlockSpec((CONV_K, dk), lambda h, i: (0, h)),
            pl.BlockSpec((CONV_K, dk), lambda h, i: (0, h)),
            pl.BlockSpec((CONV_K, dv), lambda h, i: (0, h)),
            pl.BlockSpec((1, LANES), lambda h, i: (0, 0)),
            pl.BlockSpec((1, LANES), lambda h, i: (0, 0)),
            pl.BlockSpec((1, GDN_HEAD_V), lambda h, i: (0, 0)),
            pl.BlockSpec(((CONV_K - 1) * tc, tc), lambda h, i: (0, 0)),
        ],
        out_specs=pl.BlockSpec((tc, dv), lambda h, i: (i, h)),
        out_shape=jax.ShapeDtypeStruct((t, GDN_VAL_DIM), BF16),
        scratch_shapes=[
            pltpu.VMEM((8, dk), F32),
            pltpu.VMEM((8, dk), F32),
            pltpu.VMEM((8, dv), F32),
            pltpu.VMEM((hps, GDN_HEAD_K, GDN_HEAD_V), F32),
        ],
        compiler_params=_params(2),
        name="gdn",
    )(proj, proj, proj, proj, small, cwq, cwk, cwv, dtb_row, alog_row, nw_row, _shift_mats(tc))


def _mix_kernel(ys_ref, yg_ref, g0_ref, g1_ref, gb_ref, ws_ref, wg_ref, o_ref):
    a = _dot(ys_ref[...], ws_ref[...])
    b = _dot(yg_ref[...], wg_ref[...])
    g0 = jax.nn.sigmoid(g0_ref[...].astype(F32) + gb_ref[0:1, :])
    g1 = jax.nn.sigmoid(g1_ref[...].astype(F32) + gb_ref[1:2, :])
    o_ref[...] = (g0 * a + g1 * b).astype(BF16)


def _mix(y_ssm, y_gdn, proj, gate_b, w_s, w_g, tm=512, tn=512):
    t = y_ssm.shape[0]
    tm = min(tm, t)
    g0_blk = COL_GATE // tn
    g1_blk = (COL_GATE + D_MODEL) // tn
    return pl.pallas_call(
        _mix_kernel,
        grid=(D_MODEL // tn, t // tm),
        in_specs=[
            pl.BlockSpec((tm, SSM_D_INNER), lambda n, i: (i, 0)),
            pl.BlockSpec((tm, GDN_VAL_DIM), lambda n, i: (i, 0)),
            pl.BlockSpec((tm, tn), lambda n, i: (i, g0_blk + n)),
            pl.BlockSpec((tm, tn), lambda n, i: (i, g1_blk + n)),
            pl.BlockSpec((2, tn), lambda n, i: (0, n)),
            pl.BlockSpec((SSM_D_INNER, tn), lambda n, i: (0, n)),
            pl.BlockSpec((GDN_VAL_DIM, tn), lambda n, i: (0, n)),
        ],
        out_specs=pl.BlockSpec((tm, tn), lambda n, i: (i, n)),
        out_shape=jax.ShapeDtypeStruct((t, D_MODEL), BF16),
        compiler_params=_params(2),
        name="mix",
    )(y_ssm, y_gdn, proj, proj, gate_b, w_s, w_g)


_N_RANK = PEER_TOPK + 1
_CAND_PAIRS = [(i, j) for i in range(_N_RANK) for j in range(_N_RANK) if (i + 1) * (j + 1) <= _N_RANK]
_CAND_ROWS = -(-len(_CAND_PAIRS) // 8) * 8


def _top_desc(s, n):
    out = []
    for _ in range(n):
        m = jnp.max(s, axis=0, keepdims=True)
        out.append(m)
        s = jnp.where(s == m, -jnp.inf, s)
    return out


def _post_kernel(x_ref, mix_ref, wo_ref, fw_ref, wq_ref, sk_ref,
                 h1_ref, hnt_ref, s2_ref, e2_ref, th_ref, e1_ref, cand_ref, *, tm):
    h1 = x_ref[...] + _dot(mix_ref[...], wo_ref[...])
    h1_ref[...] = h1
    hn = h1 * lax.rsqrt(jnp.mean(h1 * h1, axis=-1, keepdims=True) + EPS) * fw_ref[...]
    hnt_ref[...] = hn.T.astype(BF16)
    qv = _dot(hn.astype(BF16), wq_ref[...]).astype(BF16)
    nchunk = tm // LANES
    for h in range(PEER_HEADS):
        s1 = _dot_nt(sk_ref[2 * h], qv[:, (2 * h) * LANES:(2 * h + 1) * LANES])
        s2 = _dot_nt(sk_ref[2 * h + 1], qv[:, (2 * h + 1) * LANES:(2 * h + 2) * LANES])
        a1 = _top_desc(s1, _N_RANK)
        a2 = _top_desc(s2, _N_RANK)
        cand_ref[...] = jnp.full(cand_ref.shape, -jnp.inf, F32)
        for r, (i, j) in enumerate(_CAND_PAIRS):
            cand_ref[pl.ds(r, 1), :] = a1[i] + a2[j]
        cand = cand_ref[...]
        c = cand
        n_removed = jnp.zeros((1, tm), F32)
        v16 = jnp.full((1, tm), -jnp.inf, F32)
        v17 = jnp.full((1, tm), -jnp.inf, F32)
        for _ in range(_N_RANK):
            m = jnp.max(c, axis=0, keepdims=True)
            eq = c == m
            cnt = jnp.sum(eq.astype(F32), axis=0, keepdims=True)
            v16 = jnp.where(n_removed < PEER_TOPK, m, v16)
            v17 = jnp.where(n_removed < _N_RANK, m, v17)
            n_removed = n_removed + cnt
            c = jnp.where(eq, -jnp.inf, c)
        tau = 0.5 * (v16 + v17)
        tau = jnp.where(v17 == -jnp.inf, v16, tau)
        m_tot = a1[0] + a2[0]
        z = jnp.sum(jnp.where(cand >= tau, jnp.exp(cand - m_tot), 0.0), axis=0, keepdims=True)
        e2 = jnp.exp(s2 - a2[0]) / z
        e1 = jnp.exp(s1 - a1[0])
        th = tau - s1
        for cc in range(nchunk):
            ls = slice(cc * LANES, (cc + 1) * LANES)
            s2_ref[cc, h] = s2[:, ls]
            e2_ref[cc, h] = e2[:, ls]
            th_ref[cc, h] = th[:, ls]
            e1_ref[cc, h] = e1[:, ls]


def _post(x, mix, w_out, ffn_w, w_q, sub_keys, tm=256):
    t, d = x.shape
    tm = min(tm, t)
    nchunk = tm // LANES
    stat = jax.ShapeDtypeStruct((t // LANES, PEER_HEADS, PEER_N_KEYS, LANES), F32)
    stat_spec = pl.BlockSpec((nchunk, PEER_HEADS, PEER_N_KEYS, LANES), lambda i: (i, 0, 0, 0))
    kern = functools.partial(_post_kernel, tm=tm)
    return pl.pallas_call(
        kern,
        grid=(t // tm,),
        in_specs=[
            pl.BlockSpec((tm, d), lambda i: (i, 0)),
            pl.BlockSpec((tm, d), lambda i: (i, 0)),
            pl.BlockSpec((d, d), lambda i: (0, 0)),
            pl.BlockSpec((1, d), lambda i: (0, 0)),
            pl.BlockSpec((d, d), lambda i: (0, 0)),
            pl.BlockSpec((2 * PEER_HEADS, PEER_N_KEYS, LANES), lambda i: (0, 0, 0)),
        ],
        out_specs=[
            pl.BlockSpec((tm, d), lambda i: (i, 0)),
            pl.BlockSpec((d, tm), lambda i: (0, i)),
            stat_spec, stat_spec, stat_spec, stat_spec,
        ],
        out_shape=[
            jax.ShapeDtypeStruct((t, d), F32),
            jax.ShapeDtypeStruct((d, t), BF16),
            stat, stat, stat, stat,
        ],
        scratch_shapes=[pltpu.VMEM((_CAND_ROWS, tm), F32)],
        compiler_params=_params(1),
        name="post",
    )(x, mix, w_out, ffn_w, w_q, sub_keys)


_ROW_TILE = 64


def _gelu(x):
    return 0.5 * x * (1.0 + lax.erf(x * (2.0 ** -0.5)))


def _peer_kernel(hnt_ref, u0_ref, u1_ref, vt0_ref, vt1_ref, s2_ref, e2_ref, th_ref, e1_ref, acc_ref,
                 pt0_ref, pt1_ref,
                 *, tb, eb):
    n1 = eb // PEER_N_KEYS

    @pl.when(pl.program_id(1) == 0)
    def _():
        acc_ref[...] = jnp.zeros_like(acc_ref)

    def weights_times_act(at, pt_ref, row0):
        for cc in range(tb // LANES):
            ls = slice(cc * LANES, (cc + 1) * LANES)
            for rt in range(PEER_N_KEYS // _ROW_TILE):
                rs = slice(rt * _ROW_TILE, (rt + 1) * _ROW_TILE)
                w = [jnp.zeros((_ROW_TILE, LANES), F32) for _ in range(n1)]
                for h in range(PEER_HEADS):
                    s2 = s2_ref[cc, h, rs, :]
                    e2 = e2_ref[cc, h, rs, :]
                    for i in range(n1):
                        th = th_ref[cc, h, row0 + i:row0 + i + 1, :]
                        e1 = e1_ref[cc, h, row0 + i:row0 + i + 1, :]
                        w[i] = w[i] + jnp.where(s2 >= th, e2, 0.0) * e1
                for i in range(n1):
                    r0 = i * PEER_N_KEYS + rt * _ROW_TILE
                    a = at[r0:r0 + _ROW_TILE, ls]
                    pt_ref[r0:r0 + _ROW_TILE, ls] = (w[i] * _gelu(a)).astype(BF16)

    hnt = hnt_ref[...]
    at0 = _dot(u0_ref[...], hnt)
    at1 = _dot(u1_ref[...], hnt)
    weights_times_act(at0, pt0_ref, 0)
    acc_ref[...] += _dot(vt0_ref[...], pt0_ref[...])
    weights_times_act(at1, pt1_ref, n1)
    acc_ref[...] += _dot(vt1_ref[...], pt1_ref[...])


def _peer(hnt, u16, v16, s2, e2, th, e1, tb=512, eb=512):
    d, t = hnt.shape
    tb = min(tb, t)
    nchunk = tb // LANES
    n1 = eb // PEER_N_KEYS
    nk = PEER_N_EXPERTS // (2 * eb)
    vt_slabs = jnp.swapaxes(v16.reshape(2 * nk, eb, d), 1, 2)
    assert 2 * n1 == 8, "a pair of expert blocks spans one 8-row group of first-key statistics"
    stat_spec = pl.BlockSpec((nchunk, PEER_HEADS, PEER_N_KEYS, LANES), lambda i, k: (i, 0, 0, 0))
    row_spec = pl.BlockSpec((nchunk, PEER_HEADS, 8, LANES), lambda i, k: (i, 0, k, 0))
    kern = functools.partial(_peer_kernel, tb=tb, eb=eb)
    return pl.pallas_call(
        kern,
        grid=(t // tb, nk),
        in_specs=[
            pl.BlockSpec((d, tb), lambda i, k: (0, i)),
            pl.BlockSpec((eb, d), lambda i, k: (2 * k, 0)),
            pl.BlockSpec((eb, d), lambda i, k: (2 * k + 1, 0)),
            pl.BlockSpec((None, d, eb), lambda i, k: (2 * k, 0, 0)),
            pl.BlockSpec((None, d, eb), lambda i, k: (2 * k + 1, 0, 0)),
            stat_spec, stat_spec, row_spec, row_spec,
        ],
        out_specs=pl.BlockSpec((d, tb), lambda i, k: (0, i)),
        out_shape=jax.ShapeDtypeStruct((d, t), F32),
        scratch_shapes=[pltpu.VMEM((eb, tb), BF16), pltpu.VMEM((eb, tb), BF16)],
        compiler_params=_params(2),
        name="peer",
    )(hnt, u16, u16, vt_slabs, vt_slabs, s2, e2, th, e1)


def _final_kernel(h_ref, pt_ref, w_ref, o_ref):
    hf = h_ref[...] + pt_ref[...].T
    o_ref[...] = hf * lax.rsqrt(jnp.mean(hf * hf, axis=-1, keepdims=True) + EPS) * w_ref[...]


def _final(h1, peer_out_t, w_row, tm=512):
    t, d = h1.shape
    tm = min(tm, t)
    spec = pl.BlockSpec((tm, d), lambda i: (i, 0))
    return pl.pallas_call(
        _final_kernel,
        grid=(t // tm,),
        in_specs=[spec, pl.BlockSpec((d, tm), lambda i: (0, i)), pl.BlockSpec((1, d), lambda i: (0, 0))],
        out_specs=spec,
        out_shape=jax.ShapeDtypeStruct((t, d), F32),
        compiler_params=_params(1),
        name="final",
    )(h1, peer_out_t, w_row)


def _row(v, width=None):
    v = v.astype(F32).reshape(1, -1)
    if width is not None and v.shape[1] < width:
        v = jnp.pad(v, ((0, 0), (0, width - v.shape[1])))
    return v


def _layer(h, mix_norm_w, w_in, gate_b, ssm_conv_w, ssm_conv_b, ssm_dt_bias, ssm_a_log, ssm_d,
           ssm_norm_w, gdn_conv_w, gdn_dt_bias, gdn_a_log, gdn_norm_w, w_branch_ssm, w_branch_gdn,
           w_out, ffn_norm_w, peer_w_q, peer_sub_keys, peer_u, peer_v, out_norm_w):
    ssm_proj = SSM_D_INNER + (SSM_D_INNER + 2 * SSM_N_GROUPS * SSM_D_STATE) + SSM_N_HEADS
    gdn_conv_dim = 2 * GDN_KEY_DIM + GDN_VAL_DIM
    o_dt = ssm_proj - SSM_N_HEADS
    o_qkv = ssm_proj
    o_gz = o_qkv + gdn_conv_dim
    o_beta = o_gz + GDN_VAL_DIM
    o_gate = o_beta + 2 * GDN_N_HEADS
    w_main = jnp.concatenate(
        [w_in[:, :o_dt], w_in[:, o_qkv:o_beta], w_in[:, o_gate:]], axis=1).astype(BF16)
    w_small = jnp.concatenate(
        [w_in[:, o_dt:o_qkv], w_in[:, o_beta:o_gate],
         jnp.zeros((D_MODEL, LANES - SSM_N_HEADS - 2 * GDN_N_HEADS), w_in.dtype)], axis=1).astype(BF16)

    proj, small = _inproj(h, _row(mix_norm_w), w_main, w_small)

    cw = ssm_conv_w.astype(F32)
    cbias = _row(ssm_conv_b)
    nx = SSM_D_INNER
    nb = SSM_N_GROUPS * SSM_D_STATE
    y_ssm = _ssd(proj, small,
                 cw[:, :nx], cw[:, nx:nx + nb], cw[:, nx + nb:],
                 cbias[:, :nx], cbias[:, nx:nx + nb], cbias[:, nx + nb:],
                 _row(ssm_dt_bias, LANES), _row(ssm_a_log, LANES),
                 _row(jnp.repeat(ssm_d, SSM_HEAD_DIM)), _row(ssm_norm_w))

    gw = gdn_conv_w.astype(F32)
    zeros_b = jnp.zeros((SMALL_ALPHA,), F32)
    y_gdn = _gdn(proj, small,
                 gw[:, :GDN_KEY_DIM], gw[:, GDN_KEY_DIM:2 * GDN_KEY_DIM], gw[:, 2 * GDN_KEY_DIM:],
                 _row(jnp.concatenate([zeros_b, gdn_dt_bias.astype(F32)]), LANES),
                 _row(jnp.concatenate([zeros_b, gdn_a_log.astype(F32)]), LANES),
                 _row(gdn_norm_w))

    mix = _mix(y_ssm, y_gdn, proj, gate_b.astype(F32),
               w_branch_ssm.astype(BF16), w_branch_gdn.astype(BF16))

    sk = peer_sub_keys.reshape(2 * PEER_HEADS, PEER_N_KEYS, LANES).astype(BF16)
    h1, hnt, s2, e2, th, e1 = _post(h, mix, w_out.astype(BF16), _row(ffn_norm_w),
                                    peer_w_q.astype(BF16), sk)

    peer_out_t = _peer(hnt, peer_u.astype(BF16), peer_v.astype(BF16), s2, e2, th, e1)
    return _final(h1, peer_out_t, _row(out_norm_w))


def kernel(x, mix_norm_w, w_in, gate_b, ssm_conv_w, ssm_conv_b, ssm_dt_bias, ssm_a_log, ssm_d,
           ssm_norm_w, gdn_conv_w, gdn_dt_bias, gdn_a_log, gdn_norm_w, w_branch_ssm, w_branch_gdn,
           w_out, ffn_norm_w, peer_w_q, peer_sub_keys, peer_u, peer_v, final_norm_w):
    b, t, d = x.shape
    assert b == 1 and mix_norm_w.shape[0] == 1, "single sequence, single layer"
    out = _layer(x[0], mix_norm_w[0], w_in[0], gate_b[0], ssm_conv_w[0], ssm_conv_b[0],
                 ssm_dt_bias[0], ssm_a_log[0], ssm_d[0], ssm_norm_w[0], gdn_conv_w[0],
                 gdn_dt_bias[0], gdn_a_log[0], gdn_norm_w[0], w_branch_ssm[0], w_branch_gdn[0],
                 w_out[0], ffn_norm_w[0], peer_w_q[0], peer_sub_keys[0], peer_u[0], peer_v[0],
                 final_norm_w)
    return out[None]
```

```python
import functools

import jax
import jax.numpy as jnp
from jax import lax
from jax.experimental import pallas as pl
from jax.experimental.pallas import tpu as pltpu

F32 = jnp.float32
BF16 = jnp.bfloat16
HIGHEST = lax.Precision.HIGHEST

EPS = 1e-6
D_MODEL = 2048
LANES = 128
SSM_D_INNER = 4096
SSM_HEAD_DIM = 64
SSM_N_HEADS = 64
SSM_N_GROUPS = 8
SSM_HPG = 8
SSM_D_STATE = 128
SSM_CHUNK = 128
SSM_GROUP_W = SSM_HPG * SSM_HEAD_DIM
GDN_N_HEADS = 16
GDN_HEAD_K = 128
GDN_HEAD_V = 256
GDN_CHUNK = 64
GDN_KEY_DIM = GDN_N_HEADS * GDN_HEAD_K
GDN_VAL_DIM = GDN_N_HEADS * GDN_HEAD_V
CONV_K = 4
PEER_HEADS = 8
PEER_N_KEYS = 128
PEER_TOPK = 16
PEER_N_EXPERTS = PEER_N_KEYS * PEER_N_KEYS

COL_SSM_Z = 0
COL_SSM_XBC = 4096
COL_GDN_QKV = 10240
COL_GDN_Z = 18432
COL_GATE = 22528
N_MAIN = 26624
SMALL_BETA = 64
SMALL_ALPHA = 80

VMEM_LIMIT = 56 * 1024 * 1024


def _softplus(x):
    return jnp.maximum(x, 0.0) + jnp.log1p(jnp.exp(-jnp.abs(x)))


def _silu(x):
    return x * jax.nn.sigmoid(x)


def _dot(a, b, **kw):
    return jnp.dot(a, b, preferred_element_type=F32, **kw)


def _dot_nt(a, b, **kw):
    return lax.dot_general(a, b, (((1,), (1,)), ((), ())), preferred_element_type=F32, **kw)


def _dot_tn(a, b, **kw):
    return lax.dot_general(a, b, (((0,), (0,)), ((), ())), preferred_element_type=F32, **kw)


def _params(n_grid):
    return pltpu.CompilerParams(dimension_semantics=("arbitrary",) * n_grid,
                                vmem_limit_bytes=VMEM_LIMIT)


def _inproj_kernel(x_ref, nw_ref, w_ref, ws_ref, o_ref, os_ref, xn_ref):
    @pl.when(pl.program_id(1) == 0)
    def _():
        x = x_ref[...]
        ms = jnp.mean(x * x, axis=-1, keepdims=True)
        xn = (x * lax.rsqrt(ms + EPS) * nw_ref[...]).astype(BF16)
        xn_ref[...] = xn
        os_ref[...] = _dot(xn, ws_ref[...])

    o_ref[...] = _dot(xn_ref[...], w_ref[...]).astype(BF16)


def _inproj(x, norm_w, w_main, w_small, tm=1024, tn=1024):
    t, d = x.shape
    n = w_main.shape[1]
    tm = min(tm, t)
    return pl.pallas_call(
        _inproj_kernel,
        grid=(t // tm, n // tn),
        in_specs=[
            pl.BlockSpec((tm, d), lambda i, j: (i, 0)),
            pl.BlockSpec((1, d), lambda i, j: (0, 0)),
            pl.BlockSpec((d, tn), lambda i, j: (0, j)),
            pl.BlockSpec((d, LANES), lambda i, j: (0, 0)),
        ],
        out_specs=[
            pl.BlockSpec((tm, tn), lambda i, j: (i, j)),
            pl.BlockSpec((tm, LANES), lambda i, j: (i, 0)),
        ],
        out_shape=[jax.ShapeDtypeStruct((t, n), BF16), jax.ShapeDtypeStruct((t, LANES), F32)],
        scratch_shapes=[pltpu.VMEM((tm, d), BF16)],
        compiler_params=_params(2),
        name="inproj",
    )(x, norm_w, w_main, w_small)


def _shift_mats(tc):
    return jnp.concatenate([jnp.eye(tc, k=-(CONV_K - 1 - k), dtype=BF16) for k in range(CONV_K - 1)], axis=0)


def _conv_block(raw_ref, carry_ref, w_ref, shift_ref, tc):
    x16 = raw_ref[...]
    xf = x16.astype(F32)
    y = xf * w_ref[CONV_K - 1:CONV_K, :]
    c8 = carry_ref[...]
    row8 = lax.broadcasted_iota(jnp.int32, c8.shape, 0)
    head = jnp.zeros_like(c8)
    shifted = _dot(shift_ref[...], x16)
    for k in range(CONV_K - 1):
        lag = CONV_K - 1 - k
        y = y + shifted[k * tc:(k + 1) * tc] * w_ref[k:k + 1, :]
        head = head + jnp.where(row8 < lag, pltpu.roll(c8, shift=lag, axis=0), 0.0) * w_ref[k:k + 1, :]
    carry_ref[...] = xf[tc - 8:tc]
    return jnp.concatenate([y[0:8] + head, y[8:]], axis=0)


def _ssd_kernel(xs_ref, b_ref, c_ref, z_ref, sm_ref, cwx_ref, cwb_ref, cwc_ref, cbx_ref, cbb_ref,
                cbc_ref, dtb_ref, alog_ref, dsk_ref, nw_ref, sh_ref, o_ref,
                xbuf, bbuf, cbuf, state_ref, acst_ref, *, tc):
    g = pl.program_id(0)
    L = SSM_CHUNK

    @pl.when(pl.program_id(1) == 0)
    def _():
        xbuf[...] = jnp.zeros_like(xbuf)
        bbuf[...] = jnp.zeros_like(bbuf)
        cbuf[...] = jnp.zeros_like(cbuf)
        state_ref[...] = jnp.zeros_like(state_ref)

    xs_all = _silu(_conv_block(xs_ref, xbuf, cwx_ref, sh_ref, tc) + cbx_ref[...])
    b_all = _silu(_conv_block(b_ref, bbuf, cwb_ref, sh_ref, tc) + cbb_ref[...])
    c_all = _silu(_conv_block(c_ref, cbuf, cwc_ref, sh_ref, tc) + cbc_ref[...])

    row = lax.broadcasted_iota(jnp.int32, (L, L), 0)
    col = lax.broadcasted_iota(jnp.int32, (L, L), 1)
    causal = row >= col
    tri = causal.astype(F32)
    lo_half = col < SSM_HEAD_DIM
    a_row = -jnp.exp(alog_ref[...])

    for c in range(tc // L):
        sl = slice(c * L, (c + 1) * L)
        xs, bm, cm = xs_all[sl], b_all[sl], c_all[sl]
        dt_full = _softplus(sm_ref[sl, :] + dtb_ref[...])
        acs_full = _dot(tri, dt_full * a_row, precision=HIGHEST)
        acst_ref[...] = acs_full.T
        bm16, cm16 = bm.astype(BF16), cm.astype(BF16)
        cb = _dot_nt(cm16, bm16)
        ys = []
        for p in range(SSM_HPG // 2):
            acs_bc, dt_bc, scores = [], [], []
            for r in (2 * p, 2 * p + 1):
                j = g * SSM_HPG + r
                sel = col == j
                a_c = jnp.sum(jnp.where(sel, acs_full, 0.0), axis=1, keepdims=True)
                d_c = jnp.sum(jnp.where(sel, dt_full, 0.0), axis=1, keepdims=True)
                a_b = jnp.broadcast_to(a_c, (L, L))
                acs_bc.append(a_b)
                dt_bc.append(jnp.broadcast_to(d_c, (L, L)))
                a_r = acst_ref[pl.ds(j, 1), :]
                seg = jnp.where(causal, a_b - a_r, -jnp.inf)
                scores.append((cb * jnp.exp(seg)).astype(BF16))
            acs_e = jnp.where(lo_half, acs_bc[0], acs_bc[1])
            dt_e = jnp.where(lo_half, dt_bc[0], dt_bc[1])
            lanes = slice(p * LANES, (p + 1) * LANES)
            x_p = xs[:, lanes]
            xdt = x_p * dt_e
            last = acs_e[L - 1:L, :]
            xdtw = (xdt * jnp.exp(last - acs_e)).astype(BF16)
            s2 = jnp.concatenate(scores, axis=1)
            x2 = jnp.concatenate([jnp.where(lo_half, xdt, 0.0), jnp.where(lo_half, 0.0, xdt)],
                                 axis=0).astype(BF16)
            y_diag = _dot(s2, x2)
            st = state_ref[:, lanes]
            y_off = _dot(cm16, st.astype(BF16)) * jnp.exp(acs_e)
            state_ref[:, lanes] = st * jnp.exp(last) + _dot_tn(bm16, xdtw)
            ys.append(y_diag + y_off + x_p * dsk_ref[:, lanes])
        y = jnp.concatenate(ys, axis=1) * _silu(z_ref[sl, :].astype(F32))
        y = y * lax.rsqrt(jnp.mean(y * y, axis=-1, keepdims=True) + EPS)
        o_ref[sl, :] = (y * nw_ref[...]).astype(BF16)


def _ssd(proj, small, cwx, cwb, cwc, cbx, cbb, cbc, dtb_row, alog_row, dsk_row, nw_row, tc=256):
    t = proj.shape[0]
    tc = min(tc, t)
    gw, ns = SSM_GROUP_W, SSM_D_STATE
    xs_blk = COL_SSM_XBC // gw
    b_blk = (COL_SSM_XBC + SSM_D_INNER) // ns
    c_blk = b_blk + SSM_N_GROUPS
    z_blk = COL_SSM_Z // gw
    kern = functools.partial(_ssd_kernel, tc=tc)
    return pl.pallas_call(
        kern,
        grid=(SSM_N_GROUPS, t // tc),
        in_specs=[
            pl.BlockSpec((tc, gw), lambda g, i: (i, xs_blk + g)),
            pl.BlockSpec((tc, ns), lambda g, i: (i, b_blk + g)),
            pl.BlockSpec((tc, ns), lambda g, i: (i, c_blk + g)),
            pl.BlockSpec((tc, gw), lambda g, i: (i, z_blk + g)),
            pl.BlockSpec((tc, LANES), lambda g, i: (i, 0)),
            pl.BlockSpec((CONV_K, gw), lambda g, i: (0, g)),
            pl.BlockSpec((CONV_K, ns), lambda g, i: (0, g)),
            pl.BlockSpec((CONV_K, ns), lambda g, i: (0, g)),
            pl.BlockSpec((1, gw), lambda g, i: (0, g)),
            pl.BlockSpec((1, ns), lambda g, i: (0, g)),
            pl.BlockSpec((1, ns), lambda g, i: (0, g)),
            pl.BlockSpec((1, LANES), lambda g, i: (0, 0)),
            pl.BlockSpec((1, LANES), lambda g, i: (0, 0)),
            pl.BlockSpec((1, gw), lambda g, i: (0, g)),
            pl.BlockSpec((1, gw), lambda g, i: (0, g)),
            pl.BlockSpec(((CONV_K - 1) * tc, tc), lambda g, i: (0, 0)),
        ],
        out_specs=pl.BlockSpec((tc, gw), lambda g, i: (i, g)),
        out_shape=jax.ShapeDtypeStruct((t, SSM_D_INNER), BF16),
        scratch_shapes=[
            pltpu.VMEM((8, gw), F32),
            pltpu.VMEM((8, ns), F32),
            pltpu.VMEM((8, ns), F32),
            pltpu.VMEM((ns, gw), F32),
            pltpu.VMEM((LANES, SSM_CHUNK), F32),
        ],
        compiler_params=_params(2),
        name="ssd",
    )(proj, proj, proj, proj, small, cwx, cwb, cwc, cbx, cbb, cbc, dtb_row, alog_row, dsk_row, nw_row,
      _shift_mats(tc))


def _gdn_kernel(q_ref, k_ref, v_ref, z_ref, sm_ref, cwq_ref, cwk_ref, cwv_ref, dtb_ref, alog_ref,
                nw_ref, sh_ref, o_ref, qbuf, kbuf, vbuf, s_ref, *, tc, hps):
    h0 = pl.program_id(0) * hps
    C, DK, DV = GDN_CHUNK, GDN_HEAD_K, GDN_HEAD_V

    @pl.when(pl.program_id(1) == 0)
    def _():
        qbuf[...] = jnp.zeros_like(qbuf)
        kbuf[...] = jnp.zeros_like(kbuf)
        vbuf[...] = jnp.zeros_like(vbuf)
        s_ref[...] = jnp.zeros_like(s_ref)

    q_cv = _silu(_conv_block(q_ref, qbuf, cwq_ref, sh_ref, tc))
    k_cv = _silu(_conv_block(k_ref, kbuf, cwk_ref, sh_ref, tc))
    v_cv = _silu(_conv_block(v_ref, vbuf, cwv_ref, sh_ref, tc))

    sm = sm_ref[...]
    lane = lax.broadcasted_iota(jnp.int32, sm.shape, 1)
    beta_full = jax.nn.sigmoid(sm)
    g_full = -jnp.exp(alog_ref[...]) * _softplus(sm + dtb_ref[...])
    q_hd, k_hd, v_hd, beta_hd, g_hd = [], [], [], [], []
    for hd in range(hps):
        q = q_cv[:, hd * DK:(hd + 1) * DK]
        k = k_cv[:, hd * DK:(hd + 1) * DK]
        q_hd.append(q * lax.rsqrt(jnp.sum(q * q, axis=-1, keepdims=True) + EPS) * (DK ** -0.5))
        k_hd.append(k * lax.rsqrt(jnp.sum(k * k, axis=-1, keepdims=True) + EPS))
        v_hd.append(v_cv[:, hd * DV:(hd + 1) * DV])
        beta_hd.append(jnp.sum(jnp.where(lane == SMALL_BETA + h0 + hd, beta_full, 0.0),
                               axis=1, keepdims=True))
        g_hd.append(jnp.sum(jnp.where(lane == SMALL_ALPHA + h0 + hd, g_full, 0.0),
                            axis=1, keepdims=True))

    row = lax.broadcasted_iota(jnp.int32, (C, C), 0)
    col = lax.broadcasted_iota(jnp.int32, (C, C), 1)
    causal = row >= col
    strict = row > col
    tri = causal.astype(F32)
    eye = (row == col).astype(F32)

    nc = tc // C
    items = [(hd, slice(c * C, (c + 1) * C)) for hd in range(hps) for c in range(nc)]
    rng = range(len(items))
    strict_w = (lax.broadcasted_iota(jnp.int32, (C, LANES), 0)
                > lax.broadcasted_iota(jnp.int32, (C, LANES), 1))
    qs = [q_hd[hd][sl] for hd, sl in items]
    ks = [k_hd[hd][sl] for hd, sl in items]
    vs = [v_hd[hd][sl] for hd, sl in items]
    betas = [beta_hd[hd][sl] for hd, sl in items]
    g_bc = [jnp.broadcast_to(g_hd[hd][sl], (C, LANES)) for hd, sl in items]
    cums = [_dot(tri, jnp.concatenate([g_bc[c], jnp.where(strict_w, g_bc[c], 0.0)], axis=1),
                 precision=HIGHEST) for c in rng]
    gc_bc = [x[:, :LANES] for x in cums]
    decay = [jnp.exp(jnp.where(causal, x[:, LANES:LANES + C], -jnp.inf)) for x in cums]
    k16 = [k.astype(BF16) for k in ks]
    q16 = [q.astype(BF16) for q in qs]
    kk = [_dot_nt(k16[c], k16[c]) for c in rng]
    qk = [_dot_nt(q16[c], k16[c]) for c in rng]
    p = [-jnp.where(strict, kk[c] * betas[c] * decay[c], 0.0) for c in rng]
    x = [eye + p[c] for c in rng]
    for _ in range(C.bit_length() - 2):
        p16 = [pp.astype(BF16) for pp in p]
        p = [_dot(p16[c], p16[c]) for c in rng]
        x = [x[c] + _dot(x[c].astype(BF16), p[c].astype(BF16)) for c in rng]
    t_inv = [xx.astype(BF16) for xx in x]
    egc = [jnp.exp(gc) for gc in gc_bc]
    u = [_dot(t_inv[c], (vs[c] * betas[c]).astype(BF16)).astype(BF16) for c in rng]
    w = [_dot(t_inv[c], (ks[c] * (betas[c] * egc[c])).astype(BF16)).astype(BF16) for c in rng]
    attn = [jnp.where(causal, qk[c] * decay[c], 0.0).astype(BF16) for c in rng]
    gc_last = [gc[C - 1:C, :] for gc in gc_bc]
    k_dec = [(ks[c] * jnp.exp(gc_last[c] - gc_bc[c])).astype(BF16) for c in rng]
    c_dec = [jnp.exp(gl[:, :1]) for gl in gc_last]
    a_mat = [_dot_tn(k_dec[c], w[c]).astype(BF16) for c in rng]
    n_mat = [_dot_tn(k_dec[c], u[c]) for c in rng]
    q_eff = [(qs[c] * egc[c] - _dot(attn[c], w[c])).astype(BF16) for c in rng]
    o_u = [_dot(attn[c], u[c]) for c in rng]

    s = [s_ref[hd] for hd in range(hps)]
    for c in range(nc):
        sl = slice(c * C, (c + 1) * C)
        for hd in range(hps):
            it = hd * nc + c
            s16 = s[hd].astype(BF16)
            o = _dot(q_eff[it], s16) + o_u[it]
            s[hd] = s[hd] * c_dec[it] - _dot(a_mat[it], s16) + n_mat[it]
            o = o * lax.rsqrt(jnp.mean(o * o, axis=-1, keepdims=True) + EPS) * nw_ref[...]
            z = z_ref[sl, hd * DV:(hd + 1) * DV].astype(F32)
            o_ref[sl, hd * DV:(hd + 1) * DV] = (o * _silu(z)).astype(BF16)
    for hd in range(hps):
        s_ref[hd] = s[hd]


def _gdn(proj, small, cwq, cwk, cwv, dtb_row, alog_row, nw_row, tc=256, hps=4):
    t = proj.shape[0]
    tc = min(tc, t)
    dk, dv = hps * GDN_HEAD_K, hps * GDN_HEAD_V
    q_blk = COL_GDN_QKV // dk
    k_blk = q_blk + GDN_N_HEADS // hps
    v_blk = (COL_GDN_QKV + 2 * GDN_KEY_DIM) // dv
    z_blk = COL_GDN_Z // dv
    kern = functools.partial(_gdn_kernel, tc=tc, hps=hps)
    return pl.pallas_call(
        kern,
        grid=(GDN_N_HEADS // hps, t // tc),
        in_specs=[
            pl.BlockSpec((tc, dk), lambda h, i: (i, q_blk + h)),
            pl.BlockSpec((tc, dk), lambda h, i: (i, k_blk + h)),
            pl.BlockSpec((tc, dv), lambda h, i: (i, v_blk + h)),
            pl.BlockSpec((tc, dv), lambda h, i: (i, z_blk + h)),
            pl.BlockSpec((tc, LANES), lambda h, i: (i, 0)),
            pl.BlockSpec((CONV_K, dk), lambda h, i: (0, h)),
            pl.BlockSpec((CONV_K, dk), lambda h, i: (0, h)),
            pl.BlockSpec((CONV_K, dv), lambda h, i: (0, h)),
            pl.BlockSpec((1, LANES), lambda h, i: (0, 0)),
            pl.BlockSpec((1, LANES), lambda h, i: (0, 0)),
            pl.BlockSpec((1, GDN_HEAD_V), lambda h, i: (0, 0)),
            pl.BlockSpec(((CONV_K - 1) * tc, tc), lambda h, i: (0, 0)),
        ],
        out_specs=pl.BlockSpec((tc, dv), lambda h, i: (i, h)),
        out_shape=jax.ShapeDtypeStruct((t, GDN_VAL_DIM), BF16),
        scratch_shapes=[
            pltpu.VMEM((8, dk), F32),
            pltpu.VMEM((8, dk), F32),
            pltpu.VMEM((8, dv), F32),
            pltpu.VMEM((hps, GDN_HEAD_K, GDN_HEAD_V), F32),
        ],
        compiler_params=_params(2),
        name="gdn",
    )(proj, proj, proj, proj, small, cwq, cwk, cwv, dtb_row, alog_row, nw_row, _shift_mats(tc))


def _mix_kernel(ys_ref, yg_ref, g0_ref, g1_ref, gb_ref, ws_ref, wg_ref, o_ref):
    a = _dot(ys_ref[...], ws_ref[...])
    b = _dot(yg_ref[...], wg_ref[...])
    g0 = jax.nn.sigmoid(g0_ref[...].astype(F32) + gb_ref[0:1, :])
    g1 = jax.nn.sigmoid(g1_ref[...].astype(F32) + gb_ref[1:2, :])
    o_ref[...] = (g0 * a + g1 * b).astype(BF16)


def _mix(y_ssm, y_gdn, proj, gate_b, w_s, w_g, tm=512, tn=512):
    t = y_ssm.shape[0]
    tm = min(tm, t)
    g0_blk = COL_GATE // tn
    g1_blk = (COL_GATE + D_MODEL) // tn
    return pl.pallas_call(
        _mix_kernel,
        grid=(D_MODEL // tn, t // tm),
        in_specs=[
            pl.BlockSpec((tm, SSM_D_INNER), lambda n, i: (i, 0)),
            pl.BlockSpec((tm, GDN_VAL_DIM), lambda n, i: (i, 0)),
            pl.BlockSpec((tm, tn), lambda n, i: (i, g0_blk + n)),
            pl.BlockSpec((tm, tn), lambda n, i: (i, g1_blk + n)),
            pl.BlockSpec((2, tn), lambda n, i: (0, n)),
            pl.BlockSpec((SSM_D_INNER, tn), lambda n, i: (0, n)),
            pl.BlockSpec((GDN_VAL_DIM, tn), lambda n, i: (0, n)),
        ],
        out_specs=pl.BlockSpec((tm, tn), lambda n, i: (i, n)),
        out_shape=jax.ShapeDtypeStruct((t, D_MODEL), BF16),
        compiler_params=_params(2),
        name="mix",
    )(y_ssm, y_gdn, proj, proj, gate_b, w_s, w_g)


_N_RANK = PEER_TOPK + 1
_CAND_PAIRS = [(i, j) for i in range(_N_RANK) for j in range(_N_RANK) if (i + 1) * (j + 1) <= _N_RANK]
_CAND_ROWS = -(-len(_CAND_PAIRS) // 8) * 8


def _top_desc(s, n):
    out = []
    for _ in range(n):
        m = jnp.max(s, axis=0, keepdims=True)
        out.append(m)
        s = jnp.where(s == m, -jnp.inf, s)
    return out


def _post_kernel(x_ref, mix_ref, wo_ref, fw_ref, wq_ref, sk_ref,
                 h1_ref, hnt_ref, s2_ref, e2_ref, th_ref, e1_ref, cand_ref, *, tm):
    h1 = x_ref[...] + _dot(mix_ref[...], wo_ref[...])
    h1_ref[...] = h1
    hn = h1 * lax.rsqrt(jnp.mean(h1 * h1, axis=-1, keepdims=True) + EPS) * fw_ref[...]
    hnt_ref[...] = hn.T.astype(BF16)
    qv = _dot(hn.astype(BF16), wq_ref[...]).astype(BF16)
    nchunk = tm // LANES
    for h in range(PEER_HEADS):
        s1 = _dot_nt(sk_ref[2 * h], qv[:, (2 * h) * LANES:(2 * h + 1) * LANES])
        s2 = _dot_nt(sk_ref[2 * h + 1], qv[:, (2 * h + 1) * LANES:(2 * h + 2) * LANES])
        a1 = _top_desc(s1, _N_RANK)
        a2 = _top_desc(s2, _N_RANK)
        cand_ref[...] = jnp.full(cand_ref.shape, -jnp.inf, F32)
        for r, (i, j) in enumerate(_CAND_PAIRS):
            cand_ref[pl.ds(r, 1), :] = a1[i] + a2[j]
        cand = cand_ref[...]
        c = cand
        n_removed = jnp.zeros((1, tm), F32)
        v16 = jnp.full((1, tm), -jnp.inf, F32)
        v17 = jnp.full((1, tm), -jnp.inf, F32)
        for _ in range(_N_RANK):
            m = jnp.max(c, axis=0, keepdims=True)
            eq = c == m
            cnt = jnp.sum(eq.astype(F32), axis=0, keepdims=True)
            v16 = jnp.where(n_removed < PEER_TOPK, m, v16)
            v17 = jnp.where(n_removed < _N_RANK, m, v17)
            n_removed = n_removed + cnt
            c = jnp.where(eq, -jnp.inf, c)
        tau = 0.5 * (v16 + v17)
        tau = jnp.where(v17 == -jnp.inf, v16, tau)
        m_tot = a1[0] + a2[0]
        z = jnp.sum(jnp.where(cand >= tau, jnp.exp(cand - m_tot), 0.0), axis=0, keepdims=True)
        e2 = jnp.exp(s2 - a2[0]) / z
        e1 = jnp.exp(s1 - a1[0])
        th = tau - s1
        for cc in range(nchunk):
            ls = slice(cc * LANES, (cc + 1) * LANES)
            s2_ref[cc, h] = s2[:, ls]
            e2_ref[cc, h] = e2[:, ls]
            th_ref[cc, h] = th[:, ls]
            e1_ref[cc, h] = e1[:, ls]


def _post(x, mix, w_out, ffn_w, w_q, sub_keys, tm=256):
    t, d = x.shape
    tm = min(tm, t)
    nchunk = tm // LANES
    stat = jax.ShapeDtypeStruct((t // LANES, PEER_HEADS, PEER_N_KEYS, LANES), F32)
    stat_spec = pl.BlockSpec((nchunk, PEER_HEADS, PEER_N_KEYS, LANES), lambda i: (i, 0, 0, 0))
    kern = functools.partial(_post_kernel, tm=tm)
    return pl.pallas_call(
        kern,
        grid=(t // tm,),
        in_specs=[
            pl.BlockSpec((tm, d), lambda i: (i, 0)),
            pl.BlockSpec((tm, d), lambda i: (i, 0)),
            pl.BlockSpec((d, d), lambda i: (0, 0)),
            pl.BlockSpec((1, d), lambda i: (0, 0)),
            pl.BlockSpec((d, d), lambda i: (0, 0)),
            pl.BlockSpec((2 * PEER_HEADS, PEER_N_KEYS, LANES), lambda i: (0, 0, 0)),
        ],
        out_specs=[
            pl.BlockSpec((tm, d), lambda i: (i, 0)),
            pl.BlockSpec((d, tm), lambda i: (0, i)),
            stat_spec, stat_spec, stat_spec, stat_spec,
        ],
        out_shape=[
            jax.ShapeDtypeStruct((t, d), F32),
            jax.ShapeDtypeStruct((d, t), BF16),
            stat, stat, stat, stat,
        ],
        scratch_shapes=[pltpu.VMEM((_CAND_ROWS, tm), F32)],
        compiler_params=_params(1),
        name="post",
    )(x, mix, w_out, ffn_w, w_q, sub_keys)


_ROW_TILE = 32


def _gelu(x):
    return 0.5 * x * (1.0 + lax.erf(x * (2.0 ** -0.5)))


def _peer_kernel(hnt_ref, u0_ref, u1_ref, vt0_ref, vt1_ref, s2_ref, e2_ref, th_ref, e1_ref, acc_ref,
                 at0_ref, at1_ref, pt0_ref, pt1_ref, *, tb, eb):
    n1 = eb // PEER_N_KEYS
    nchunk = tb // LANES
    nrt = PEER_N_KEYS // _ROW_TILE
    d = acc_ref.shape[0]

    @pl.when(pl.program_id(1) == 0)
    def _():
        acc_ref[...] = jnp.zeros_like(acc_ref)

    hnt = hnt_ref[...]
    at0_ref[...] = _dot(u0_ref[...], hnt)
    at1_ref[...] = _dot(u1_ref[...], hnt)

    def tile(at_ref, pt_ref, row0, cc, rt):
        ls = slice(cc * LANES, (cc + 1) * LANES)
        rs = pl.ds(pl.multiple_of(rt * _ROW_TILE, _ROW_TILE), _ROW_TILE)
        w = [jnp.zeros((_ROW_TILE, LANES), F32) for _ in range(n1)]
        for h in range(PEER_HEADS):
            s2 = s2_ref[cc, h, rs, :]
            e2 = e2_ref[cc, h, rs, :]
            for i in range(n1):
                th = th_ref[cc, h, row0 + i:row0 + i + 1, :]
                e1 = e1_ref[cc, h, row0 + i:row0 + i + 1, :]
                w[i] = w[i] + jnp.where(s2 >= th, e2, 0.0) * e1
        for i in range(n1):
            r = pl.ds(pl.multiple_of(i * PEER_N_KEYS + rt * _ROW_TILE, _ROW_TILE), _ROW_TILE)
            pt_ref[r, ls] = (w[i] * _gelu(at_ref[r, ls])).astype(BF16)

    for cc in range(nchunk):
        def body0(rt, carry, cc=cc):
            tile(at0_ref, pt0_ref, 0, cc, rt)
            return carry
        lax.fori_loop(0, nrt, body0, 0)

    acc_ref[...] += _dot(vt0_ref[...], pt0_ref[...])

    for cc in range(nchunk):
        def body1(rt, carry, cc=cc):
            tile(at1_ref, pt1_ref, n1, cc, rt)
            return carry
        lax.fori_loop(0, nrt, body1, 0)

    acc_ref[...] += _dot(vt1_ref[...], pt1_ref[...])


def _peer(hnt, u16, v16, s2, e2, th, e1, tb=512, eb=512):
    d, t = hnt.shape
    tb = min(tb, t)
    nchunk = tb // LANES
    n1 = eb // PEER_N_KEYS
    nk = PEER_N_EXPERTS // (2 * eb)
    vt_slabs = jnp.swapaxes(v16.reshape(2 * nk, eb, d), 1, 2)
    assert 2 * n1 == 8, "a pair of expert blocks spans one 8-row group of first-key statistics"
    stat_spec = pl.BlockSpec((nchunk, PEER_HEADS, PEER_N_KEYS, LANES), lambda i, k: (i, 0, 0, 0))
    row_spec = pl.BlockSpec((nchunk, PEER_HEADS, 8, LANES), lambda i, k: (i, 0, k, 0))
    kern = functools.partial(_peer_kernel, tb=tb, eb=eb)
    return pl.pallas_call(
        kern,
        grid=(t // tb, nk),
        in_specs=[
            pl.BlockSpec((d, tb), lambda i, k: (0, i)),
            pl.BlockSpec((eb, d), lambda i, k: (2 * k, 0)),
            pl.BlockSpec((eb, d), lambda i, k: (2 * k + 1, 0)),
            pl.BlockSpec((None, d, eb), lambda i, k: (2 * k, 0, 0)),
            pl.BlockSpec((None, d, eb), lambda i, k: (2 * k + 1, 0, 0)),
            stat_spec, stat_spec, row_spec, row_spec,
        ],
        out_specs=pl.BlockSpec((d, tb), lambda i, k: (0, i)),
        out_shape=jax.ShapeDtypeStruct((d, t), F32),
        scratch_shapes=[pltpu.VMEM((eb, tb), F32), pltpu.VMEM((eb, tb), F32),
                        pltpu.VMEM((eb, tb), BF16), pltpu.VMEM((eb, tb), BF16)],
        compiler_params=_params(2),
        name="peer",
    )(hnt, u16, u16, vt_slabs, vt_slabs, s2, e2, th, e1)


def _final_kernel(h_ref, pt_ref, w_ref, o_ref):
    hf = h_ref[...] + pt_ref[...].T
    o_ref[...] = hf * lax.rsqrt(jnp.mean(hf * hf, axis=-1, keepdims=True) + EPS) * w_ref[...]


def _final(h1, peer_out_t, w_row, tm=512):
    t, d = h1.shape
    tm = min(tm, t)
    spec = pl.BlockSpec((tm, d), lambda i: (i, 0))
    return pl.pallas_call(
        _final_kernel,
        grid=(t // tm,),
        in_specs=[spec, pl.BlockSpec((d, tm), lambda i: (0, i)), pl.BlockSpec((1, d), lambda i: (0, 0))],
        out_specs=spec,
        out_shape=jax.ShapeDtypeStruct((t, d), F32),
        compiler_params=_params(1),
        name="final",
    )(h1, peer_out_t, w_row)


def _row(v, width=None):
    v = v.astype(F32).reshape(1, -1)
    if width is not None and v.shape[1] < width:
        v = jnp.pad(v, ((0, 0), (0, width - v.shape[1])))
    return v


def _layer(h, mix_norm_w, w_in, gate_b, ssm_conv_w, ssm_conv_b, ssm_dt_bias, ssm_a_log, ssm_d,
           ssm_norm_w, gdn_conv_w, gdn_dt_bias, gdn_a_log, gdn_norm_w, w_branch_ssm, w_branch_gdn,
           w_out, ffn_norm_w, peer_w_q, peer_sub_keys, peer_u, peer_v, out_norm_w):
    ssm_proj = SSM_D_INNER + (SSM_D_INNER + 2 * SSM_N_GROUPS * SSM_D_STATE) + SSM_N_HEADS
    gdn_conv_dim = 2 * GDN_KEY_DIM + GDN_VAL_DIM
    o_dt = ssm_proj - SSM_N_HEADS
    o_qkv = ssm_proj
    o_gz = o_qkv + gdn_conv_dim
    o_beta = o_gz + GDN_VAL_DIM
    o_gate = o_beta + 2 * GDN_N_HEADS
    w_main = jnp.concatenate(
        [w_in[:, :o_dt], w_in[:, o_qkv:o_beta], w_in[:, o_gate:]], axis=1).astype(BF16)
    w_small = jnp.concatenate(
        [w_in[:, o_dt:o_qkv], w_in[:, o_beta:o_gate],
         jnp.zeros((D_MODEL, LANES - SSM_N_HEADS - 2 * GDN_N_HEADS), w_in.dtype)], axis=1).astype(BF16)

    proj, small = _inproj(h, _row(mix_norm_w), w_main, w_small)

    cw = ssm_conv_w.astype(F32)
    cbias = _row(ssm_conv_b)
    nx = SSM_D_INNER
    nb = SSM_N_GROUPS * SSM_D_STATE
    y_ssm = _ssd(proj, small,
                 cw[:, :nx], cw[:, nx:nx + nb], cw[:, nx + nb:],
                 cbias[:, :nx], cbias[:, nx:nx + nb], cbias[:, nx + nb:],
                 _row(ssm_dt_bias, LANES), _row(ssm_a_log, LANES),
                 _row(jnp.repeat(ssm_d, SSM_HEAD_DIM)), _row(ssm_norm_w))

    gw = gdn_conv_w.astype(F32)
    zeros_b = jnp.zeros((SMALL_ALPHA,), F32)
    y_gdn = _gdn(proj, small,
                 gw[:, :GDN_KEY_DIM], gw[:, GDN_KEY_DIM:2 * GDN_KEY_DIM], gw[:, 2 * GDN_KEY_DIM:],
                 _row(jnp.concatenate([zeros_b, gdn_dt_bias.astype(F32)]), LANES),
                 _row(jnp.concatenate([zeros_b, gdn_a_log.astype(F32)]), LANES),
                 _row(gdn_norm_w))

    mix = _mix(y_ssm, y_gdn, proj, gate_b.astype(F32),
               w_branch_ssm.astype(BF16), w_branch_gdn.astype(BF16))

    sk = peer_sub_keys.reshape(2 * PEER_HEADS, PEER_N_KEYS, LANES).astype(BF16)
    h1, hnt, s2, e2, th, e1 = _post(h, mix, w_out.astype(BF16), _row(ffn_norm_w),
                                    peer_w_q.astype(BF16), sk)

    peer_out_t = _peer(hnt, peer_u.astype(BF16), peer_v.astype(BF16), s2, e2, th, e1)
    return _final(h1, peer_out_t, _row(out_norm_w))


def kernel(x, mix_norm_w, w_in, gate_b, ssm_conv_w, ssm_conv_b, ssm_dt_bias, ssm_a_log, ssm_d,
           ssm_norm_w, gdn_conv_w, gdn_dt_bias, gdn_a_log, gdn_norm_w, w_branch_ssm, w_branch_gdn,
           w_out, ffn_norm_w, peer_w_q, peer_sub_keys, peer_u, peer_v, final_norm_w):
    b, t, d = x.shape
    assert b == 1 and mix_norm_w.shape[0] == 1, "single sequence, single layer"
    out = _layer(x[0], mix_norm_w[0], w_in[0], gate_b[0], ssm_conv_w[0], ssm_conv_b[0],
                 ssm_dt_bias[0], ssm_a_log[0], ssm_d[0], ssm_norm_w[0], gdn_conv_w[0],
                 gdn_dt_bias[0], gdn_a_log[0], gdn_norm_w[0], w_branch_ssm[0], w_branch_gdn[0],
                 w_out[0], ffn_norm_w[0], peer_w_q[0], peer_sub_keys[0], peer_u[0], peer_v[0],
                 final_norm_w)
    return out[None]
```

```python
import functools

import jax
import jax.numpy as jnp
from jax import lax
from jax.experimental import pallas as pl
from jax.experimental.pallas import tpu as pltpu

F32 = jnp.float32
BF16 = jnp.bfloat16
HIGHEST = lax.Precision.HIGHEST

EPS = 1e-6
D_MODEL = 2048
LANES = 128
SSM_D_INNER = 4096
SSM_HEAD_DIM = 64
SSM_N_HEADS = 64
SSM_N_GROUPS = 8
SSM_HPG = 8
SSM_D_STATE = 128
SSM_CHUNK = 128
SSM_GROUP_W = SSM_HPG * SSM_HEAD_DIM
GDN_N_HEADS = 16
GDN_HEAD_K = 128
GDN_HEAD_V = 256
GDN_CHUNK = 64
GDN_KEY_DIM = GDN_N_HEADS * GDN_HEAD_K
GDN_VAL_DIM = GDN_N_HEADS * GDN_HEAD_V
CONV_K = 4
PEER_HEADS = 8
PEER_N_KEYS = 128
PEER_TOPK = 16
PEER_N_EXPERTS = PEER_N_KEYS * PEER_N_KEYS

COL_SSM_Z = 0
COL_SSM_XBC = 4096
COL_GDN_QKV = 10240
COL_GDN_Z = 18432
COL_GATE = 22528
N_MAIN = 26624
SMALL_BETA = 64
SMALL_ALPHA = 80

VMEM_LIMIT = 56 * 1024 * 1024


def _softplus(x):
    return jnp.maximum(x, 0.0) + jnp.log1p(jnp.exp(-jnp.abs(x)))


def _silu(x):
    return x * jax.nn.sigmoid(x)


def _dot(a, b, **kw):
    return jnp.dot(a, b, preferred_element_type=F32, **kw)


def _dot_nt(a, b, **kw):
    return lax.dot_general(a, b, (((1,), (1,)), ((), ())), preferred_element_type=F32, **kw)


def _dot_tn(a, b, **kw):
    return lax.dot_general(a, b, (((0,), (0,)), ((), ())), preferred_element_type=F32, **kw)


def _params(n_grid):
    return pltpu.CompilerParams(dimension_semantics=("arbitrary",) * n_grid,
                                vmem_limit_bytes=VMEM_LIMIT)


def _inproj_kernel(x_ref, nw_ref, w_ref, ws_ref, o_ref, os_ref, xn_ref):
    @pl.when(pl.program_id(1) == 0)
    def _():
        x = x_ref[...]
        ms = jnp.mean(x * x, axis=-1, keepdims=True)
        xn = (x * lax.rsqrt(ms + EPS) * nw_ref[...]).astype(BF16)
        xn_ref[...] = xn
        os_ref[...] = _dot(xn, ws_ref[...])

    o_ref[...] = _dot(xn_ref[...], w_ref[...]).astype(BF16)


def _inproj(x, norm_w, w_main, w_small, tm=1024, tn=1024):
    t, d = x.shape
    n = w_main.shape[1]
    tm = min(tm, t)
    return pl.pallas_call(
        _inproj_kernel,
        grid=(t // tm, n // tn),
        in_specs=[
            pl.BlockSpec((tm, d), lambda i, j: (i, 0)),
            pl.BlockSpec((1, d), lambda i, j: (0, 0)),
            pl.BlockSpec((d, tn), lambda i, j: (0, j)),
            pl.BlockSpec((d, LANES), lambda i, j: (0, 0)),
        ],
        out_specs=[
            pl.BlockSpec((tm, tn), lambda i, j: (i, j)),
            pl.BlockSpec((tm, LANES), lambda i, j: (i, 0)),
        ],
        out_shape=[jax.ShapeDtypeStruct((t, n), BF16), jax.ShapeDtypeStruct((t, LANES), F32)],
        scratch_shapes=[pltpu.VMEM((tm, d), BF16)],
        compiler_params=_params(2),
        name="inproj",
    )(x, norm_w, w_main, w_small)


def _shift_mats(tc):
    return jnp.concatenate([jnp.eye(tc, k=-(CONV_K - 1 - k), dtype=BF16) for k in range(CONV_K - 1)], axis=0)


def _conv_block(raw_ref, carry_ref, w_ref, shift_ref, tc):
    x16 = raw_ref[...]
    xf = x16.astype(F32)
    y = xf * w_ref[CONV_K - 1:CONV_K, :]
    c8 = carry_ref[...]
    row8 = lax.broadcasted_iota(jnp.int32, c8.shape, 0)
    head = jnp.zeros_like(c8)
    shifted = _dot(shift_ref[...], x16)
    for k in range(CONV_K - 1):
        lag = CONV_K - 1 - k
        y = y + shifted[k * tc:(k + 1) * tc] * w_ref[k:k + 1, :]
        head = head + jnp.where(row8 < lag, pltpu.roll(c8, shift=lag, axis=0), 0.0) * w_ref[k:k + 1, :]
    carry_ref[...] = xf[tc - 8:tc]
    return jnp.concatenate([y[0:8] + head, y[8:]], axis=0)


def _ssd_kernel(xs_ref, b_ref, c_ref, z_ref, sm_ref, cwx_ref, cwb_ref, cwc_ref, cbx_ref, cbb_ref,
                cbc_ref, dtb_ref, alog_ref, dsk_ref, nw_ref, sh_ref, o_ref,
                xbuf, bbuf, cbuf, state_ref, acst_ref, *, tc):
    g = pl.program_id(0)
    L = SSM_CHUNK

    @pl.when(pl.program_id(1) == 0)
    def _():
        xbuf[...] = jnp.zeros_like(xbuf)
        bbuf[...] = jnp.zeros_like(bbuf)
        cbuf[...] = jnp.zeros_like(cbuf)
        state_ref[...] = jnp.zeros_like(state_ref)

    xs_all = _silu(_conv_block(xs_ref, xbuf, cwx_ref, sh_ref, tc) + cbx_ref[...])
    b_all = _silu(_conv_block(b_ref, bbuf, cwb_ref, sh_ref, tc) + cbb_ref[...])
    c_all = _silu(_conv_block(c_ref, cbuf, cwc_ref, sh_ref, tc) + cbc_ref[...])

    row = lax.broadcasted_iota(jnp.int32, (L, L), 0)
    col = lax.broadcasted_iota(jnp.int32, (L, L), 1)
    causal = row >= col
    tri = causal.astype(F32)
    lo_half = col < SSM_HEAD_DIM
    a_row = -jnp.exp(alog_ref[...])

    for c in range(tc // L):
        sl = slice(c * L, (c + 1) * L)
        xs, bm, cm = xs_all[sl], b_all[sl], c_all[sl]
        dt_full = _softplus(sm_ref[sl, :] + dtb_ref[...])
        acs_full = _dot(tri, dt_full * a_row, precision=HIGHEST)
        acst_ref[...] = acs_full.T
        bm16, cm16 = bm.astype(BF16), cm.astype(BF16)
        cb = _dot_nt(cm16, bm16)
        ys = []
        for p in range(SSM_HPG // 2):
            acs_bc, dt_bc, scores = [], [], []
            for r in (2 * p, 2 * p + 1):
                j = g * SSM_HPG + r
                sel = col == j
                a_c = jnp.sum(jnp.where(sel, acs_full, 0.0), axis=1, keepdims=True)
                d_c = jnp.sum(jnp.where(sel, dt_full, 0.0), axis=1, keepdims=True)
                a_b = jnp.broadcast_to(a_c, (L, L))
                acs_bc.append(a_b)
                dt_bc.append(jnp.broadcast_to(d_c, (L, L)))
                a_r = acst_ref[pl.ds(j, 1), :]
                seg = jnp.where(causal, a_b - a_r, -jnp.inf)
                scores.append((cb * jnp.exp(seg)).astype(BF16))
            acs_e = jnp.where(lo_half, acs_bc[0], acs_bc[1])
            dt_e = jnp.where(lo_half, dt_bc[0], dt_bc[1])
            lanes = slice(p * LANES, (p + 1) * LANES)
            x_p = xs[:, lanes]
            xdt = x_p * dt_e
            last = acs_e[L - 1:L, :]
            xdtw = (xdt * jnp.exp(last - acs_e)).astype(BF16)
            s2 = jnp.concatenate(scores, axis=1)
            x2 = jnp.concatenate([jnp.where(lo_half, xdt, 0.0), jnp.where(lo_half, 0.0, xdt)],
                                 axis=0).astype(BF16)
            y_diag = _dot(s2, x2)
            st = state_ref[:, lanes]
            y_off = _dot(cm16, st.astype(BF16)) * jnp.exp(acs_e)
            state_ref[:, lanes] = st * jnp.exp(last) + _dot_tn(bm16, xdtw)
            ys.append(y_diag + y_off + x_p * dsk_ref[:, lanes])
        y = jnp.concatenate(ys, axis=1) * _silu(z_ref[sl, :].astype(F32))
        y = y * lax.rsqrt(jnp.mean(y * y, axis=-1, keepdims=True) + EPS)
        o_ref[sl, :] = (y * nw_ref[...]).astype(BF16)


def _ssd(proj, small, cwx, cwb, cwc, cbx, cbb, cbc, dtb_row, alog_row, dsk_row, nw_row, tc=256):
    t = proj.shape[0]
    tc = min(tc, t)
    gw, ns = SSM_GROUP_W, SSM_D_STATE
    xs_blk = COL_SSM_XBC // gw
    b_blk = (COL_SSM_XBC + SSM_D_INNER) // ns
    c_blk = b_blk + SSM_N_GROUPS
    z_blk = COL_SSM_Z // gw
    kern = functools.partial(_ssd_kernel, tc=tc)
    return pl.pallas_call(
        kern,
        grid=(SSM_N_GROUPS, t // tc),
        in_specs=[
            pl.BlockSpec((tc, gw), lambda g, i: (i, xs_blk + g)),
            pl.BlockSpec((tc, ns), lambda g, i: (i, b_blk + g)),
            pl.BlockSpec((tc, ns), lambda g, i: (i, c_blk + g)),
            pl.BlockSpec((tc, gw), lambda g, i: (i, z_blk + g)),
            pl.BlockSpec((tc, LANES), lambda g, i: (i, 0)),
            pl.BlockSpec((CONV_K, gw), lambda g, i: (0, g)),
            pl.BlockSpec((CONV_K, ns), lambda g, i: (0, g)),
            pl.BlockSpec((CONV_K, ns), lambda g, i: (0, g)),
            pl.BlockSpec((1, gw), lambda g, i: (0, g)),
            pl.BlockSpec((1, ns), lambda g, i: (0, g)),
            pl.BlockSpec((1, ns), lambda g, i: (0, g)),
            pl.BlockSpec((1, LANES), lambda g, i: (0, 0)),
            pl.BlockSpec((1, LANES), lambda g, i: (0, 0)),
            pl.BlockSpec((1, gw), lambda g, i: (0, g)),
            pl.BlockSpec((1, gw), lambda g, i: (0, g)),
            pl.BlockSpec(((CONV_K - 1) * tc, tc), lambda g, i: (0, 0)),
        ],
        out_specs=pl.BlockSpec((tc, gw), lambda g, i: (i, g)),
        out_shape=jax.ShapeDtypeStruct((t, SSM_D_INNER), BF16),
        scratch_shapes=[
            pltpu.VMEM((8, gw), F32),
            pltpu.VMEM((8, ns), F32),
            pltpu.VMEM((8, ns), F32),
            pltpu.VMEM((ns, gw), F32),
            pltpu.VMEM((LANES, SSM_CHUNK), F32),
        ],
        compiler_params=_params(2),
        name="ssd",
    )(proj, proj, proj, proj, small, cwx, cwb, cwc, cbx, cbb, cbc, dtb_row, alog_row, dsk_row, nw_row,
      _shift_mats(tc))


def _gdn_kernel(q_ref, k_ref, v_ref, z_ref, sm_ref, cwq_ref, cwk_ref, cwv_ref, dtb_ref, alog_ref,
                nw_ref, sh_ref, o_ref, qbuf, kbuf, vbuf, s_ref, *, tc, hps):
    h0 = pl.program_id(0) * hps
    C, DK, DV = GDN_CHUNK, GDN_HEAD_K, GDN_HEAD_V

    @pl.when(pl.program_id(1) == 0)
    def _():
        qbuf[...] = jnp.zeros_like(qbuf)
        kbuf[...] = jnp.zeros_like(kbuf)
        vbuf[...] = jnp.zeros_like(vbuf)
        s_ref[...] = jnp.zeros_like(s_ref)

    q_cv = _silu(_conv_block(q_ref, qbuf, cwq_ref, sh_ref, tc))
    k_cv = _silu(_conv_block(k_ref, kbuf, cwk_ref, sh_ref, tc))
    v_cv = _silu(_conv_block(v_ref, vbuf, cwv_ref, sh_ref, tc))

    sm = sm_ref[...]
    lane = lax.broadcasted_iota(jnp.int32, sm.shape, 1)
    beta_full = jax.nn.sigmoid(sm)
    g_full = -jnp.exp(alog_ref[...]) * _softplus(sm + dtb_ref[...])
    q_hd, k_hd, v_hd, beta_hd, g_hd = [], [], [], [], []
    for hd in range(hps):
        q = q_cv[:, hd * DK:(hd + 1) * DK]
        k = k_cv[:, hd * DK:(hd + 1) * DK]
        q_hd.append(q * lax.rsqrt(jnp.sum(q * q, axis=-1, keepdims=True) + EPS) * (DK ** -0.5))
        k_hd.append(k * lax.rsqrt(jnp.sum(k * k, axis=-1, keepdims=True) + EPS))
        v_hd.append(v_cv[:, hd * DV:(hd + 1) * DV])
        beta_hd.append(jnp.sum(jnp.where(lane == SMALL_BETA + h0 + hd, beta_full, 0.0),
                               axis=1, keepdims=True))
        g_hd.append(jnp.sum(jnp.where(lane == SMALL_ALPHA + h0 + hd, g_full, 0.0),
                            axis=1, keepdims=True))

    row = lax.broadcasted_iota(jnp.int32, (C, C), 0)
    col = lax.broadcasted_iota(jnp.int32, (C, C), 1)
    causal = row >= col
    strict = row > col
    tri = causal.astype(F32)
    eye = (row == col).astype(F32)

    nc = tc // C
    items = [(hd, slice(c * C, (c + 1) * C)) for hd in range(hps) for c in range(nc)]
    rng = range(len(items))
    strict_w = (lax.broadcasted_iota(jnp.int32, (C, LANES), 0)
                > lax.broadcasted_iota(jnp.int32, (C, LANES), 1))
    qs = [q_hd[hd][sl] for hd, sl in items]
    ks = [k_hd[hd][sl] for hd, sl in items]
    vs = [v_hd[hd][sl] for hd, sl in items]
    betas = [beta_hd[hd][sl] for hd, sl in items]
    g_bc = [jnp.broadcast_to(g_hd[hd][sl], (C, LANES)) for hd, sl in items]
    cums = [_dot(tri, jnp.concatenate([g_bc[c], jnp.where(strict_w, g_bc[c], 0.0)], axis=1),
                 precision=HIGHEST) for c in rng]
    gc_bc = [x[:, :LANES] for x in cums]
    decay = [jnp.exp(jnp.where(causal, x[:, LANES:LANES + C], -jnp.inf)) for x in cums]
    k16 = [k.astype(BF16) for k in ks]
    q16 = [q.astype(BF16) for q in qs]
    kk = [_dot_nt(k16[c], k16[c]) for c in rng]
    qk = [_dot_nt(q16[c], k16[c]) for c in rng]
    p = [-jnp.where(strict, kk[c] * betas[c] * decay[c], 0.0) for c in rng]
    x = [eye + p[c] for c in rng]
    for _ in range(C.bit_length() - 2):
        p16 = [pp.astype(BF16) for pp in p]
        p = [_dot(p16[c], p16[c]) for c in rng]
        x = [x[c] + _dot(x[c].astype(BF16), p[c].astype(BF16)) for c in rng]
    t_inv = [xx.astype(BF16) for xx in x]
    egc = [jnp.exp(gc) for gc in gc_bc]
    u = [_dot(t_inv[c], (vs[c] * betas[c]).astype(BF16)).astype(BF16) for c in rng]
    w = [_dot(t_inv[c], (ks[c] * (betas[c] * egc[c])).astype(BF16)).astype(BF16) for c in rng]
    attn = [jnp.where(causal, qk[c] * decay[c], 0.0).astype(BF16) for c in rng]
    gc_last = [gc[C - 1:C, :] for gc in gc_bc]
    k_dec = [(ks[c] * jnp.exp(gc_last[c] - gc_bc[c])).astype(BF16) for c in rng]
    c_dec = [jnp.exp(gl[:, :1]) for gl in gc_last]
    a_mat = [_dot_tn(k_dec[c], w[c]).astype(BF16) for c in rng]
    n_mat = [_dot_tn(k_dec[c], u[c]) for c in rng]
    q_eff = [(qs[c] * egc[c] - _dot(attn[c], w[c])).astype(BF16) for c in rng]
    o_u = [_dot(attn[c], u[c]) for c in rng]

    s = [s_ref[hd] for hd in range(hps)]
    for c in range(nc):
        sl = slice(c * C, (c + 1) * C)
        for hd in range(hps):
            it = hd * nc + c
            s16 = s[hd].astype(BF16)
            o = _dot(q_eff[it], s16) + o_u[it]
            s[hd] = s[hd] * c_dec[it] - _dot(a_mat[it], s16) + n_mat[it]
            o = o * lax.rsqrt(jnp.mean(o * o, axis=-1, keepdims=True) + EPS) * nw_ref[...]
            z = z_ref[sl, hd * DV:(hd + 1) * DV].astype(F32)
            o_ref[sl, hd * DV:(hd + 1) * DV] = (o * _silu(z)).astype(BF16)
    for hd in range(hps):
        s_ref[hd] = s[hd]


def _gdn(proj, small, cwq, cwk, cwv, dtb_row, alog_row, nw_row, tc=256, hps=4):
    t = proj.shape[0]
    tc = min(tc, t)
    dk, dv = hps * GDN_HEAD_K, hps * GDN_HEAD_V
    q_blk = COL_GDN_QKV // dk
    k_blk = q_blk + GDN_N_HEADS // hps
    v_blk = (COL_GDN_QKV + 2 * GDN_KEY_DIM) // dv
    z_blk = COL_GDN_Z // dv
    kern = functools.partial(_gdn_kernel, tc=tc, hps=hps)
    return pl.pallas_call(
        kern,
        grid=(GDN_N_HEADS // hps, t // tc),
        in_specs=[
            pl.BlockSpec((tc, dk), lambda h, i: (i, q_blk + h)),
            pl.BlockSpec((tc, dk), lambda h, i: (i, k_blk + h)),
            pl.BlockSpec((tc, dv), lambda h, i: (i, v_blk + h)),
            pl.BlockSpec((tc, dv), lambda h, i: (i, z_blk + h)),
            pl.BlockSpec((tc, LANES), lambda h, i: (i, 0)),
            pl.BlockSpec((CONV_K, dk), lambda h, i: (0, h)),
            pl.BlockSpec((CONV_K, dk), lambda h, i: (0, h)),
            pl.BlockSpec((CONV_K, dv), lambda h, i: (0, h)),
            pl.BlockSpec((1, LANES), lambda h, i: (0, 0)),
            pl.BlockSpec((1, LANES), lambda h, i: (0, 0)),
            pl.BlockSpec((1, GDN_HEAD_V), lambda h, i: (0, 0)),
            pl.BlockSpec(((CONV_K - 1) * tc, tc), lambda h, i: (0, 0)),
        ],
        out_specs=pl.BlockSpec((tc, dv), lambda h, i: (i, h)),
        out_shape=jax.ShapeDtypeStruct((t, GDN_VAL_DIM), BF16),
        scratch_shapes=[
            pltpu.VMEM((8, dk), F32),
            pltpu.VMEM((8, dk), F32),
            pltpu.VMEM((8, dv), F32),
            pltpu.VMEM((hps, GDN_HEAD_K, GDN_HEAD_V), F32),
        ],
        compiler_params=_params(2),
        name="gdn",
    )(proj, proj, proj, proj, small, cwq, cwk, cwv, dtb_row, alog_row, nw_row, _shift_mats(tc))


def _mix_kernel(ys_ref, yg_ref, g0_ref, g1_ref, gb_ref, ws_ref, wg_ref, o_ref):
    a = _dot(ys_ref[...], ws_ref[...])
    b = _dot(yg_ref[...], wg_ref[...])
    g0 = jax.nn.sigmoid(g0_ref[...].astype(F32) + gb_ref[0:1, :])
    g1 = jax.nn.sigmoid(g1_ref[...].astype(F32) + gb_ref[1:2, :])
    o_ref[...] = (g0 * a + g1 * b).astype(BF16)


def _mix(y_ssm, y_gdn, proj, gate_b, w_s, w_g, tm=512, tn=512):
    t = y_ssm.shape[0]
    tm = min(tm, t)
    g0_blk = COL_GATE // tn
    g1_blk = (COL_GATE + D_MODEL) // tn
    return pl.pallas_call(
        _mix_kernel,
        grid=(D_MODEL // tn, t // tm),
        in_specs=[
            pl.BlockSpec((tm, SSM_D_INNER), lambda n, i: (i, 0)),
            pl.BlockSpec((tm, GDN_VAL_DIM), lambda n, i: (i, 0)),
            pl.BlockSpec((tm, tn), lambda n, i: (i, g0_blk + n)),
            pl.BlockSpec((tm, tn), lambda n, i: (i, g1_blk + n)),
            pl.BlockSpec((2, tn), lambda n, i: (0, n)),
            pl.BlockSpec((SSM_D_INNER, tn), lambda n, i: (0, n)),
            pl.BlockSpec((GDN_VAL_DIM, tn), lambda n, i: (0, n)),
        ],
        out_specs=pl.BlockSpec((tm, tn), lambda n, i: (i, n)),
        out_shape=jax.ShapeDtypeStruct((t, D_MODEL), BF16),
        compiler_params=_params(2),
        name="mix",
    )(y_ssm, y_gdn, proj, proj, gate_b, w_s, w_g)


_N_RANK = PEER_TOPK + 1
_CAND_PAIRS = [(i, j) for i in range(_N_RANK) for j in range(_N_RANK) if (i + 1) * (j + 1) <= _N_RANK]
_CAND_ROWS = -(-len(_CAND_PAIRS) // 8) * 8


def _top_desc(s, n):
    out = []
    for _ in range(n):
        m = jnp.max(s, axis=0, keepdims=True)
        out.append(m)
        s = jnp.where(s == m, -jnp.inf, s)
    return out


def _post_kernel(x_ref, mix_ref, wo_ref, fw_ref, wq_ref, sk_ref,
                 h1_ref, hnt_ref, s2_ref, e2_ref, th_ref, e1_ref, cand_ref, *, tm):
    h1 = x_ref[...] + _dot(mix_ref[...], wo_ref[...])
    h1_ref[...] = h1
    hn = h1 * lax.rsqrt(jnp.mean(h1 * h1, axis=-1, keepdims=True) + EPS) * fw_ref[...]
    hnt_ref[...] = hn.T.astype(BF16)
    qv = _dot(hn.astype(BF16), wq_ref[...]).astype(BF16)
    nchunk = tm // LANES
    for h in range(PEER_HEADS):
        s1 = _dot_nt(sk_ref[2 * h], qv[:, (2 * h) * LANES:(2 * h + 1) * LANES])
        s2 = _dot_nt(sk_ref[2 * h + 1], qv[:, (2 * h + 1) * LANES:(2 * h + 2) * LANES])
        a1 = _top_desc(s1, _N_RANK)
        a2 = _top_desc(s2, _N_RANK)
        cand_ref[...] = jnp.full(cand_ref.shape, -jnp.inf, F32)
        for r, (i, j) in enumerate(_CAND_PAIRS):
            cand_ref[pl.ds(r, 1), :] = a1[i] + a2[j]
        cand = cand_ref[...]
        c = cand
        n_removed = jnp.zeros((1, tm), F32)
        v16 = jnp.full((1, tm), -jnp.inf, F32)
        v17 = jnp.full((1, tm), -jnp.inf, F32)
        for _ in range(_N_RANK):
            m = jnp.max(c, axis=0, keepdims=True)
            eq = c == m
            cnt = jnp.sum(eq.astype(F32), axis=0, keepdims=True)
            v16 = jnp.where(n_removed < PEER_TOPK, m, v16)
            v17 = jnp.where(n_removed < _N_RANK, m, v17)
            n_removed = n_removed + cnt
            c = jnp.where(eq, -jnp.inf, c)
        tau = 0.5 * (v16 + v17)
        tau = jnp.where(v17 == -jnp.inf, v16, tau)
        m_tot = a1[0] + a2[0]
        z = jnp.sum(jnp.where(cand >= tau, jnp.exp(cand - m_tot), 0.0), axis=0, keepdims=True)
        e2 = jnp.exp(s2 - a2[0]) / z
        e1 = jnp.exp(s1 - a1[0])
        th = tau - s1
        for cc in range(nchunk):
            ls = slice(cc * LANES, (cc + 1) * LANES)
            s2_ref[cc, h] = s2[:, ls]
            e2_ref[cc, h] = e2[:, ls]
            th_ref[cc, h] = th[:, ls]
            e1_ref[cc, h] = e1[:, ls]


def _post(x, mix, w_out, ffn_w, w_q, sub_keys, tm=256):
    t, d = x.shape
    tm = min(tm, t)
    nchunk = tm // LANES
    stat = jax.ShapeDtypeStruct((t // LANES, PEER_HEADS, PEER_N_KEYS, LANES), F32)
    stat_spec = pl.BlockSpec((nchunk, PEER_HEADS, PEER_N_KEYS, LANES), lambda i: (i, 0, 0, 0))
    kern = functools.partial(_post_kernel, tm=tm)
    return pl.pallas_call(
        kern,
        grid=(t // tm,),
        in_specs=[
            pl.BlockSpec((tm, d), lambda i: (i, 0)),
            pl.BlockSpec((tm, d), lambda i: (i, 0)),
            pl.BlockSpec((d, d), lambda i: (0, 0)),
            pl.BlockSpec((1, d), lambda i: (0, 0)),
            pl.BlockSpec((d, d), lambda i: (0, 0)),
            pl.BlockSpec((2 * PEER_HEADS, PEER_N_KEYS, LANES), lambda i: (0, 0, 0)),
        ],
        out_specs=[
            pl.BlockSpec((tm, d), lambda i: (i, 0)),
            pl.BlockSpec((d, tm), lambda i: (0, i)),
            stat_spec, stat_spec, stat_spec, stat_spec,
        ],
        out_shape=[
            jax.ShapeDtypeStruct((t, d), F32),
            jax.ShapeDtypeStruct((d, t), BF16),
            stat, stat, stat, stat,
        ],
        scratch_shapes=[pltpu.VMEM((_CAND_ROWS, tm), F32)],
        compiler_params=_params(1),
        name="post",
    )(x, mix, w_out, ffn_w, w_q, sub_keys)


_ROW_TILE = 32


def _gelu(x):
    return 0.5 * x * (1.0 + lax.erf(x * (2.0 ** -0.5)))


def _peer_kernel(hnt_ref, u0_ref, u1_ref, vt0_ref, vt1_ref, s2_ref, e2_ref, th_ref, e1_ref, acc_ref,
                 at0_ref, at1_ref, pt0_ref, pt1_ref, *, tb, eb):
    n1 = eb // PEER_N_KEYS
    nchunk = tb // LANES
    nrt = PEER_N_KEYS // _ROW_TILE
    d = acc_ref.shape[0]

    @pl.when(pl.program_id(1) == 0)
    def _():
        acc_ref[...] = jnp.zeros_like(acc_ref)

    hnt = hnt_ref[...]
    at0_ref[...] = _dot(u0_ref[...], hnt)
    at1_ref[...] = _dot(u1_ref[...], hnt)

    def tile(at_ref, pt_ref, row0, cc, rt):
        ls = slice(cc * LANES, (cc + 1) * LANES)
        rs = pl.ds(pl.multiple_of(rt * _ROW_TILE, _ROW_TILE), _ROW_TILE)
        w = [jnp.zeros((_ROW_TILE, LANES), F32) for _ in range(n1)]
        for h in range(PEER_HEADS):
            s2 = s2_ref[cc, h, rs, :]
            e2 = e2_ref[cc, h, rs, :]
            for i in range(n1):
                th = th_ref[cc, h, row0 + i:row0 + i + 1, :]
                e1 = e1_ref[cc, h, row0 + i:row0 + i + 1, :]
                w[i] = w[i] + jnp.where(s2 >= th, e2, 0.0) * e1
        for i in range(n1):
            r = pl.ds(pl.multiple_of(i * PEER_N_KEYS + rt * _ROW_TILE, _ROW_TILE), _ROW_TILE)
            pt_ref[r, ls] = (w[i] * _gelu(at_ref[r, ls])).astype(BF16)

    c_rows = d // nrt

    def body0(rt, carry):
        for cc in range(nchunk):
            tile(at0_ref, pt0_ref, 0, cc, rt)
        return carry
    lax.fori_loop(0, nrt, body0, 0)

    def body1(rt, carry):
        for cc in range(nchunk):
            tile(at1_ref, pt1_ref, n1, cc, rt)
        r = pl.ds(pl.multiple_of(rt * c_rows, c_rows), c_rows)
        acc_ref[r, :] += _dot(vt0_ref[r, :], pt0_ref[...])
        return carry
    lax.fori_loop(0, nrt, body1, 0)

    acc_ref[...] += _dot(vt1_ref[...], pt1_ref[...])


def _peer(hnt, u16, v16, s2, e2, th, e1, tb=512, eb=512):
    d, t = hnt.shape
    tb = min(tb, t)
    nchunk = tb // LANES
    n1 = eb // PEER_N_KEYS
    nk = PEER_N_EXPERTS // (2 * eb)
    vt_slabs = jnp.swapaxes(v16.reshape(2 * nk, eb, d), 1, 2)
    assert 2 * n1 == 8, "a pair of expert blocks spans one 8-row group of first-key statistics"
    stat_spec = pl.BlockSpec((nchunk, PEER_HEADS, PEER_N_KEYS, LANES), lambda i, k: (i, 0, 0, 0))
    row_spec = pl.BlockSpec((nchunk, PEER_HEADS, 8, LANES), lambda i, k: (i, 0, k, 0))
    kern = functools.partial(_peer_kernel, tb=tb, eb=eb)
    return pl.pallas_call(
        kern,
        grid=(t // tb, nk),
        in_specs=[
            pl.BlockSpec((d, tb), lambda i, k: (0, i)),
            pl.BlockSpec((eb, d), lambda i, k: (2 * k, 0)),
            pl.BlockSpec((eb, d), lambda i, k: (2 * k + 1, 0)),
            pl.BlockSpec((None, d, eb), lambda i, k: (2 * k, 0, 0)),
            pl.BlockSpec((None, d, eb), lambda i, k: (2 * k + 1, 0, 0)),
            stat_spec, stat_spec, row_spec, row_spec,
        ],
        out_specs=pl.BlockSpec((d, tb), lambda i, k: (0, i)),
        out_shape=jax.ShapeDtypeStruct((d, t), F32),
        scratch_shapes=[pltpu.VMEM((eb, tb), F32), pltpu.VMEM((eb, tb), F32),
                        pltpu.VMEM((eb, tb), BF16), pltpu.VMEM((eb, tb), BF16)],
        compiler_params=_params(2),
        name="peer",
    )(hnt, u16, u16, vt_slabs, vt_slabs, s2, e2, th, e1)


def _final_kernel(h_ref, pt_ref, w_ref, o_ref):
    hf = h_ref[...] + pt_ref[...].T
    o_ref[...] = hf * lax.rsqrt(jnp.mean(hf * hf, axis=-1, keepdims=True) + EPS) * w_ref[...]


def _final(h1, peer_out_t, w_row, tm=512):
    t, d = h1.shape
    tm = min(tm, t)
    spec = pl.BlockSpec((tm, d), lambda i: (i, 0))
    return pl.pallas_call(
        _final_kernel,
        grid=(t // tm,),
        in_specs=[spec, pl.BlockSpec((d, tm), lambda i: (0, i)), pl.BlockSpec((1, d), lambda i: (0, 0))],
        out_specs=spec,
        out_shape=jax.ShapeDtypeStruct((t, d), F32),
        compiler_params=_params(1),
        name="final",
    )(h1, peer_out_t, w_row)


def _row(v, width=None):
    v = v.astype(F32).reshape(1, -1)
    if width is not None and v.shape[1] < width:
        v = jnp.pad(v, ((0, 0), (0, width - v.shape[1])))
    return v


def _layer(h, mix_norm_w, w_in, gate_b, ssm_conv_w, ssm_conv_b, ssm_dt_bias, ssm_a_log, ssm_d,
           ssm_norm_w, gdn_conv_w, gdn_dt_bias, gdn_a_log, gdn_norm_w, w_branch_ssm, w_branch_gdn,
           w_out, ffn_norm_w, peer_w_q, peer_sub_keys, peer_u, peer_v, out_norm_w):
    ssm_proj = SSM_D_INNER + (SSM_D_INNER + 2 * SSM_N_GROUPS * SSM_D_STATE) + SSM_N_HEADS
    gdn_conv_dim = 2 * GDN_KEY_DIM + GDN_VAL_DIM
    o_dt = ssm_proj - SSM_N_HEADS
    o_qkv = ssm_proj
    o_gz = o_qkv + gdn_conv_dim
    o_beta = o_gz + GDN_VAL_DIM
    o_gate = o_beta + 2 * GDN_N_HEADS
    w_main = jnp.concatenate(
        [w_in[:, :o_dt], w_in[:, o_qkv:o_beta], w_in[:, o_gate:]], axis=1).astype(BF16)
    w_small = jnp.concatenate(
        [w_in[:, o_dt:o_qkv], w_in[:, o_beta:o_gate],
         jnp.zeros((D_MODEL, LANES - SSM_N_HEADS - 2 * GDN_N_HEADS), w_in.dtype)], axis=1).astype(BF16)

    proj, small = _inproj(h, _row(mix_norm_w), w_main, w_small)

    cw = ssm_conv_w.astype(F32)
    cbias = _row(ssm_conv_b)
    nx = SSM_D_INNER
    nb = SSM_N_GROUPS * SSM_D_STATE
    y_ssm = _ssd(proj, small,
                 cw[:, :nx], cw[:, nx:nx + nb], cw[:, nx + nb:],
                 cbias[:, :nx], cbias[:, nx:nx + nb], cbias[:, nx + nb:],
                 _row(ssm_dt_bias, LANES), _row(ssm_a_log, LANES),
                 _row(jnp.repeat(ssm_d, SSM_HEAD_DIM)), _row(ssm_norm_w))

    gw = gdn_conv_w.astype(F32)
    zeros_b = jnp.zeros((SMALL_ALPHA,), F32)
    y_gdn = _gdn(proj, small,
                 gw[:, :GDN_KEY_DIM], gw[:, GDN_KEY_DIM:2 * GDN_KEY_DIM], gw[:, 2 * GDN_KEY_DIM:],
                 _row(jnp.concatenate([zeros_b, gdn_dt_bias.astype(F32)]), LANES),
                 _row(jnp.concatenate([zeros_b, gdn_a_log.astype(F32)]), LANES),
                 _row(gdn_norm_w))

    mix = _mix(y_ssm, y_gdn, proj, gate_b.astype(F32),
               w_branch_ssm.astype(BF16), w_branch_gdn.astype(BF16))

    sk = peer_sub_keys.reshape(2 * PEER_HEADS, PEER_N_KEYS, LANES).astype(BF16)
    h1, hnt, s2, e2, th, e1 = _post(h, mix, w_out.astype(BF16), _row(ffn_norm_w),
                                    peer_w_q.astype(BF16), sk)

    peer_out_t = _peer(hnt, peer_u.astype(BF16), peer_v.astype(BF16), s2, e2, th, e1)
    return _final(h1, peer_out_t, _row(out_norm_w))


def kernel(x, mix_norm_w, w_in, gate_b, ssm_conv_w, ssm_conv_b, ssm_dt_bias, ssm_a_log, ssm_d,
           ssm_norm_w, gdn_conv_w, gdn_dt_bias, gdn_a_log, gdn_norm_w, w_branch_ssm, w_branch_gdn,
           w_out, ffn_norm_w, peer_w_q, peer_sub_keys, peer_u, peer_v, final_norm_w):
    b, t, d = x.shape
    assert b == 1 and mix_norm_w.shape[0] == 1, "single sequence, single layer"
    out = _layer(x[0], mix_norm_w[0], w_in[0], gate_b[0], ssm_conv_w[0], ssm_conv_b[0],
                 ssm_dt_bias[0], ssm_a_log[0], ssm_d[0], ssm_norm_w[0], gdn_conv_w[0],
                 gdn_dt_bias[0], gdn_a_log[0], gdn_norm_w[0], w_branch_ssm[0], w_branch_gdn[0],
                 w_out[0], ffn_norm_w[0], peer_w_q[0], peer_sub_keys[0], peer_u[0], peer_v[0],
                 final_norm_w)
    return out[None]
```

```python
import functools

import jax
import jax.numpy as jnp
from jax import lax
from jax.experimental import pallas as pl
from jax.experimental.pallas import tpu as pltpu

F32 = jnp.float32
BF16 = jnp.bfloat16
HIGHEST = lax.Precision.HIGHEST

EPS = 1e-6
D_MODEL = 2048
LANES = 128
SSM_D_INNER = 4096
SSM_HEAD_DIM = 64
SSM_N_HEADS = 64
SSM_N_GROUPS = 8
SSM_HPG = 8
SSM_D_STATE = 128
SSM_CHUNK = 128
SSM_GROUP_W = SSM_HPG * SSM_HEAD_DIM
GDN_N_HEADS = 16
GDN_HEAD_K = 128
GDN_HEAD_V = 256
GDN_CHUNK = 64
GDN_KEY_DIM = GDN_N_HEADS * GDN_HEAD_K
GDN_VAL_DIM = GDN_N_HEADS * GDN_HEAD_V
CONV_K = 4
PEER_HEADS = 8
PEER_N_KEYS = 128
PEER_TOPK = 16
PEER_N_EXPERTS = PEER_N_KEYS * PEER_N_KEYS

COL_SSM_Z = 0
COL_SSM_XBC = 4096
COL_GDN_QKV = 10240
COL_GDN_Z = 18432
COL_GATE = 22528
N_MAIN = 26624
SMALL_BETA = 64
SMALL_ALPHA = 80

VMEM_LIMIT = 56 * 1024 * 1024


def _softplus(x):
    return jnp.maximum(x, 0.0) + jnp.log1p(jnp.exp(-jnp.abs(x)))


def _silu(x):
    return x * jax.nn.sigmoid(x)


def _dot(a, b, **kw):
    return jnp.dot(a, b, preferred_element_type=F32, **kw)


def _dot_nt(a, b, **kw):
    return lax.dot_general(a, b, (((1,), (1,)), ((), ())), preferred_element_type=F32, **kw)


def _dot_tn(a, b, **kw):
    return lax.dot_general(a, b, (((0,), (0,)), ((), ())), preferred_element_type=F32, **kw)


def _params(n_grid):
    return pltpu.CompilerParams(dimension_semantics=("arbitrary",) * n_grid,
                                vmem_limit_bytes=VMEM_LIMIT)


def _inproj_kernel(x_ref, nw_ref, w_ref, ws_ref, o_ref, os_ref, xn_ref):
    @pl.when(pl.program_id(1) == 0)
    def _():
        x = x_ref[...]
        ms = jnp.mean(x * x, axis=-1, keepdims=True)
        xn = (x * lax.rsqrt(ms + EPS) * nw_ref[...]).astype(BF16)
        xn_ref[...] = xn
        os_ref[...] = _dot(xn, ws_ref[...])

    o_ref[...] = _dot(xn_ref[...], w_ref[...]).astype(BF16)


def _inproj(x, norm_w, w_main, w_small, tm=1024, tn=1024):
    t, d = x.shape
    n = w_main.shape[1]
    tm = min(tm, t)
    return pl.pallas_call(
        _inproj_kernel,
        grid=(t // tm, n // tn),
        in_specs=[
            pl.BlockSpec((tm, d), lambda i, j: (i, 0)),
            pl.BlockSpec((1, d), lambda i, j: (0, 0)),
            pl.BlockSpec((d, tn), lambda i, j: (0, j)),
            pl.BlockSpec((d, LANES), lambda i, j: (0, 0)),
        ],
        out_specs=[
            pl.BlockSpec((tm, tn), lambda i, j: (i, j)),
            pl.BlockSpec((tm, LANES), lambda i, j: (i, 0)),
        ],
        out_shape=[jax.ShapeDtypeStruct((t, n), BF16), jax.ShapeDtypeStruct((t, LANES), F32)],
        scratch_shapes=[pltpu.VMEM((tm, d), BF16)],
        compiler_params=_params(2),
        name="inproj",
    )(x, norm_w, w_main, w_small)


def _shift_mats(tc):
    return jnp.concatenate([jnp.eye(tc, k=-(CONV_K - 1 - k), dtype=BF16) for k in range(CONV_K - 1)], axis=0)


def _conv_block(raw_ref, carry_ref, w_ref, shift_ref, tc):
    x16 = raw_ref[...]
    xf = x16.astype(F32)
    y = xf * w_ref[CONV_K - 1:CONV_K, :]
    c8 = carry_ref[...]
    row8 = lax.broadcasted_iota(jnp.int32, c8.shape, 0)
    head = jnp.zeros_like(c8)
    shifted = _dot(shift_ref[...], x16)
    for k in range(CONV_K - 1):
        lag = CONV_K - 1 - k
        y = y + shifted[k * tc:(k + 1) * tc] * w_ref[k:k + 1, :]
        head = head + jnp.where(row8 < lag, pltpu.roll(c8, shift=lag, axis=0), 0.0) * w_ref[k:k + 1, :]
    carry_ref[...] = xf[tc - 8:tc]
    return jnp.concatenate([y[0:8] + head, y[8:]], axis=0)


def _ssd_kernel(xs_ref, b_ref, c_ref, z_ref, sm_ref, cwx_ref, cwb_ref, cwc_ref, cbx_ref, cbb_ref,
                cbc_ref, dtb_ref, alog_ref, dsk_ref, nw_ref, sh_ref, o_ref,
                xbuf, bbuf, cbuf, state_ref, acst_ref, *, tc):
    g = pl.program_id(0)
    L = SSM_CHUNK

    @pl.when(pl.program_id(1) == 0)
    def _():
        xbuf[...] = jnp.zeros_like(xbuf)
        bbuf[...] = jnp.zeros_like(bbuf)
        cbuf[...] = jnp.zeros_like(cbuf)
        state_ref[...] = jnp.zeros_like(state_ref)

    xs_all = _silu(_conv_block(xs_ref, xbuf, cwx_ref, sh_ref, tc) + cbx_ref[...])
    b_all = _silu(_conv_block(b_ref, bbuf, cwb_ref, sh_ref, tc) + cbb_ref[...])
    c_all = _silu(_conv_block(c_ref, cbuf, cwc_ref, sh_ref, tc) + cbc_ref[...])

    row = lax.broadcasted_iota(jnp.int32, (L, L), 0)
    col = lax.broadcasted_iota(jnp.int32, (L, L), 1)
    causal = row >= col
    tri = causal.astype(F32)
    lo_half = col < SSM_HEAD_DIM
    a_row = -jnp.exp(alog_ref[...])

    for c in range(tc // L):
        sl = slice(c * L, (c + 1) * L)
        xs, bm, cm = xs_all[sl], b_all[sl], c_all[sl]
        dt_full = _softplus(sm_ref[sl, :] + dtb_ref[...])
        acs_full = _dot(tri, dt_full * a_row, precision=HIGHEST)
        acst_ref[...] = acs_full.T
        bm16, cm16 = bm.astype(BF16), cm.astype(BF16)
        cb = _dot_nt(cm16, bm16)
        ys = []
        for p in range(SSM_HPG // 2):
            acs_bc, dt_bc, scores = [], [], []
            for r in (2 * p, 2 * p + 1):
                j = g * SSM_HPG + r
                sel = col == j
                a_c = jnp.sum(jnp.where(sel, acs_full, 0.0), axis=1, keepdims=True)
                d_c = jnp.sum(jnp.where(sel, dt_full, 0.0), axis=1, keepdims=True)
                a_b = jnp.broadcast_to(a_c, (L, L))
                acs_bc.append(a_b)
                dt_bc.append(jnp.broadcast_to(d_c, (L, L)))
                a_r = acst_ref[pl.ds(j, 1), :]
                seg = jnp.where(causal, a_b - a_r, -jnp.inf)
                scores.append((cb * jnp.exp(seg)).astype(BF16))
            acs_e = jnp.where(lo_half, acs_bc[0], acs_bc[1])
            dt_e = jnp.where(lo_half, dt_bc[0], dt_bc[1])
            lanes = slice(p * LANES, (p + 1) * LANES)
            x_p = xs[:, lanes]
            xdt = x_p * dt_e
            last = acs_e[L - 1:L, :]
            xdtw = (xdt * jnp.exp(last - acs_e)).astype(BF16)
            s2 = jnp.concatenate(scores, axis=1)
            x2 = jnp.concatenate([jnp.where(lo_half, xdt, 0.0), jnp.where(lo_half, 0.0, xdt)],
                                 axis=0).astype(BF16)
            y_diag = _dot(s2, x2)
            st = state_ref[:, lanes]
            y_off = _dot(cm16, st.astype(BF16)) * jnp.exp(acs_e)
            state_ref[:, lanes] = st * jnp.exp(last) + _dot_tn(bm16, xdtw)
            ys.append(y_diag + y_off + x_p * dsk_ref[:, lanes])
        y = jnp.concatenate(ys, axis=1) * _silu(z_ref[sl, :].astype(F32))
        y = y * lax.rsqrt(jnp.mean(y * y, axis=-1, keepdims=True) + EPS)
        o_ref[sl, :] = (y * nw_ref[...]).astype(BF16)


def _ssd(proj, small, cwx, cwb, cwc, cbx, cbb, cbc, dtb_row, alog_row, dsk_row, nw_row, tc=256):
    t = proj.shape[0]
    tc = min(tc, t)
    gw, ns = SSM_GROUP_W, SSM_D_STATE
    xs_blk = COL_SSM_XBC // gw
    b_blk = (COL_SSM_XBC + SSM_D_INNER) // ns
    c_blk = b_blk + SSM_N_GROUPS
    z_blk = COL_SSM_Z // gw
    kern = functools.partial(_ssd_kernel, tc=tc)
    return pl.pallas_call(
        kern,
        grid=(SSM_N_GROUPS, t // tc),
        in_specs=[
            pl.BlockSpec((tc, gw), lambda g, i: (i, xs_blk + g)),
            pl.BlockSpec((tc, ns), lambda g, i: (i, b_blk + g)),
            pl.BlockSpec((tc, ns), lambda g, i: (i, c_blk + g)),
            pl.BlockSpec((tc, gw), lambda g, i: (i, z_blk + g)),
            pl.BlockSpec((tc, LANES), lambda g, i: (i, 0)),
            pl.BlockSpec((CONV_K, gw), lambda g, i: (0, g)),
            pl.BlockSpec((CONV_K, ns), lambda g, i: (0, g)),
            pl.BlockSpec((CONV_K, ns), lambda g, i: (0, g)),
            pl.BlockSpec((1, gw), lambda g, i: (0, g)),
            pl.BlockSpec((1, ns), lambda g, i: (0, g)),
            pl.BlockSpec((1, ns), lambda g, i: (0, g)),
            pl.BlockSpec((1, LANES), lambda g, i: (0, 0)),
            pl.BlockSpec((1, LANES), lambda g, i: (0, 0)),
            pl.BlockSpec((1, gw), lambda g, i: (0, g)),
            pl.BlockSpec((1, gw), lambda g, i: (0, g)),
            pl.BlockSpec(((CONV_K - 1) * tc, tc), lambda g, i: (0, 0)),
        ],
        out_specs=pl.BlockSpec((tc, gw), lambda g, i: (i, g)),
        out_shape=jax.ShapeDtypeStruct((t, SSM_D_INNER), BF16),
        scratch_shapes=[
            pltpu.VMEM((8, gw), F32),
            pltpu.VMEM((8, ns), F32),
            pltpu.VMEM((8, ns), F32),
            pltpu.VMEM((ns, gw), F32),
            pltpu.VMEM((LANES, SSM_CHUNK), F32),
        ],
        compiler_params=_params(2),
        name="ssd",
    )(proj, proj, proj, proj, small, cwx, cwb, cwc, cbx, cbb, cbc, dtb_row, alog_row, dsk_row, nw_row,
      _shift_mats(tc))


def _gdn_kernel(q_ref, k_ref, v_ref, z_ref, sm_ref, cwq_ref, cwk_ref, cwv_ref, dtb_ref, alog_ref,
                nw_ref, sh_ref, o_ref, qbuf, kbuf, vbuf, s_ref, *, tc, hps):
    h0 = pl.program_id(0) * hps
    C, DK, DV = GDN_CHUNK, GDN_HEAD_K, GDN_HEAD_V

    @pl.when(pl.program_id(1) == 0)
    def _():
        qbuf[...] = jnp.zeros_like(qbuf)
        kbuf[...] = jnp.zeros_like(kbuf)
        vbuf[...] = jnp.zeros_like(vbuf)
        s_ref[...] = jnp.zeros_like(s_ref)

    q_cv = _silu(_conv_block(q_ref, qbuf, cwq_ref, sh_ref, tc))
    k_cv = _silu(_conv_block(k_ref, kbuf, cwk_ref, sh_ref, tc))
    v_cv = _silu(_conv_block(v_ref, vbuf, cwv_ref, sh_ref, tc))

    sm = sm_ref[...]
    lane = lax.broadcasted_iota(jnp.int32, sm.shape, 1)
    beta_full = jax.nn.sigmoid(sm)
    g_full = -jnp.exp(alog_ref[...]) * _softplus(sm + dtb_ref[...])
    q_hd, k_hd, v_hd, beta_hd, g_hd = [], [], [], [], []
    for hd in range(hps):
        q = q_cv[:, hd * DK:(hd + 1) * DK]
        k = k_cv[:, hd * DK:(hd + 1) * DK]
        q_hd.append(q * lax.rsqrt(jnp.sum(q * q, axis=-1, keepdims=True) + EPS) * (DK ** -0.5))
        k_hd.append(k * lax.rsqrt(jnp.sum(k * k, axis=-1, keepdims=True) + EPS))
        v_hd.append(v_cv[:, hd * DV:(hd + 1) * DV])
        beta_hd.append(jnp.sum(jnp.where(lane == SMALL_BETA + h0 + hd, beta_full, 0.0),
                               axis=1, keepdims=True))
        g_hd.append(jnp.sum(jnp.where(lane == SMALL_ALPHA + h0 + hd, g_full, 0.0),
                            axis=1, keepdims=True))

    row = lax.broadcasted_iota(jnp.int32, (C, C), 0)
    col = lax.broadcasted_iota(jnp.int32, (C, C), 1)
    causal = row >= col
    strict = row > col
    tri = causal.astype(F32)
    eye = (row == col).astype(F32)

    nc = tc // C
    items = [(hd, slice(c * C, (c + 1) * C)) for hd in range(hps) for c in range(nc)]
    rng = range(len(items))
    strict_w = (lax.broadcasted_iota(jnp.int32, (C, LANES), 0)
                > lax.broadcasted_iota(jnp.int32, (C, LANES), 1))
    qs = [q_hd[hd][sl] for hd, sl in items]
    ks = [k_hd[hd][sl] for hd, sl in items]
    vs = [v_hd[hd][sl] for hd, sl in items]
    betas = [beta_hd[hd][sl] for hd, sl in items]
    g_bc = [jnp.broadcast_to(g_hd[hd][sl], (C, LANES)) for hd, sl in items]
    cums = [_dot(tri, jnp.concatenate([g_bc[c], jnp.where(strict_w, g_bc[c], 0.0)], axis=1),
                 precision=HIGHEST) for c in rng]
    gc_bc = [x[:, :LANES] for x in cums]
    decay = [jnp.exp(jnp.where(causal, x[:, LANES:LANES + C], -jnp.inf)) for x in cums]
    k16 = [k.astype(BF16) for k in ks]
    q16 = [q.astype(BF16) for q in qs]
    kk = [_dot_nt(k16[c], k16[c]) for c in rng]
    qk = [_dot_nt(q16[c], k16[c]) for c in rng]
    p = [-jnp.where(strict, kk[c] * betas[c] * decay[c], 0.0) for c in rng]
    x = [eye + p[c] for c in rng]
    for _ in range(C.bit_length() - 2):
        p16 = [pp.astype(BF16) for pp in p]
        p = [_dot(p16[c], p16[c]) for c in rng]
        x = [x[c] + _dot(x[c].astype(BF16), p[c].astype(BF16)) for c in rng]
    t_inv = [xx.astype(BF16) for xx in x]
    egc = [jnp.exp(gc) for gc in gc_bc]
    u = [_dot(t_inv[c], (vs[c] * betas[c]).astype(BF16)).astype(BF16) for c in rng]
    w = [_dot(t_inv[c], (ks[c] * (betas[c] * egc[c])).astype(BF16)).astype(BF16) for c in rng]
    attn = [jnp.where(causal, qk[c] * decay[c], 0.0).astype(BF16) for c in rng]
    gc_last = [gc[C - 1:C, :] for gc in gc_bc]
    k_dec = [(ks[c] * jnp.exp(gc_last[c] - gc_bc[c])).astype(BF16) for c in rng]
    c_dec = [jnp.exp(gl[:, :1]) for gl in gc_last]
    a_mat = [_dot_tn(k_dec[c], w[c]).astype(BF16) for c in rng]
    n_mat = [_dot_tn(k_dec[c], u[c]) for c in rng]
    q_eff = [(qs[c] * egc[c] - _dot(attn[c], w[c])).astype(BF16) for c in rng]
    o_u = [_dot(attn[c], u[c]) for c in rng]

    s = [s_ref[hd] for hd in range(hps)]
    for c in range(nc):
        sl = slice(c * C, (c + 1) * C)
        for hd in range(hps):
            it = hd * nc + c
            s16 = s[hd].astype(BF16)
            o = _dot(q_eff[it], s16) + o_u[it]
            s[hd] = s[hd] * c_dec[it] - _dot(a_mat[it], s16) + n_mat[it]
            o = o * lax.rsqrt(jnp.mean(o * o, axis=-1, keepdims=True) + EPS) * nw_ref[...]
            z = z_ref[sl, hd * DV:(hd + 1) * DV].astype(F32)
            o_ref[sl, hd * DV:(hd + 1) * DV] = (o * _silu(z)).astype(BF16)
    for hd in range(hps):
        s_ref[hd] = s[hd]


def _gdn(proj, small, cwq, cwk, cwv, dtb_row, alog_row, nw_row, tc=256, hps=4):
    t = proj.shape[0]
    tc = min(tc, t)
    dk, dv = hps * GDN_HEAD_K, hps * GDN_HEAD_V
    q_blk = COL_GDN_QKV // dk
    k_blk = q_blk + GDN_N_HEADS // hps
    v_blk = (COL_GDN_QKV + 2 * GDN_KEY_DIM) // dv
    z_blk = COL_GDN_Z // dv
    kern = functools.partial(_gdn_kernel, tc=tc, hps=hps)
    return pl.pallas_call(
        kern,
        grid=(GDN_N_HEADS // hps, t // tc),
        in_specs=[
            pl.BlockSpec((tc, dk), lambda h, i: (i, q_blk + h)),
            pl.BlockSpec((tc, dk), lambda h, i: (i, k_blk + h)),
            pl.BlockSpec((tc, dv), lambda h, i: (i, v_blk + h)),
            pl.BlockSpec((tc, dv), lambda h, i: (i, z_blk + h)),
            pl.BlockSpec((tc, LANES), lambda h, i: (i, 0)),
            pl.BlockSpec((CONV_K, dk), lambda h, i: (0, h)),
            pl.BlockSpec((CONV_K, dk), lambda h, i: (0, h)),
            pl.BlockSpec((CONV_K, dv), lambda h, i: (0, h)),
            pl.BlockSpec((1, LANES), lambda h, i: (0, 0)),
            pl.BlockSpec((1, LANES), lambda h, i: (0, 0)),
            pl.BlockSpec((1, GDN_HEAD_V), lambda h, i: (0, 0)),
            pl.BlockSpec(((CONV_K - 1) * tc, tc), lambda h, i: (0, 0)),
        ],
        out_specs=pl.BlockSpec((tc, dv), lambda h, i: (i, h)),
        out_shape=jax.ShapeDtypeStruct((t, GDN_VAL_DIM), BF16),
        scratch_shapes=[
            pltpu.VMEM((8, dk), F32),
            pltpu.VMEM((8, dk), F32),
            pltpu.VMEM((8, dv), F32),
            pltpu.VMEM((hps, GDN_HEAD_K, GDN_HEAD_V), F32),
        ],
        compiler_params=_params(2),
        name="gdn",
    )(proj, proj, proj, proj, small, cwq, cwk, cwv, dtb_row, alog_row, nw_row, _shift_mats(tc))


def _mix_kernel(ys_ref, yg_ref, g0_ref, g1_ref, gb_ref, ws_ref, wg_ref, o_ref):
    a = _dot(ys_ref[...], ws_ref[...])
    b = _dot(yg_ref[...], wg_ref[...])
    g0 = jax.nn.sigmoid(g0_ref[...].astype(F32) + gb_ref[0:1, :])
    g1 = jax.nn.sigmoid(g1_ref[...].astype(F32) + gb_ref[1:2, :])
    o_ref[...] = (g0 * a + g1 * b).astype(BF16)


def _mix(y_ssm, y_gdn, proj, gate_b, w_s, w_g, tm=512, tn=512):
    t = y_ssm.shape[0]
    tm = min(tm, t)
    g0_blk = COL_GATE // tn
    g1_blk = (COL_GATE + D_MODEL) // tn
    return pl.pallas_call(
        _mix_kernel,
        grid=(D_MODEL // tn, t // tm),
        in_specs=[
            pl.BlockSpec((tm, SSM_D_INNER), lambda n, i: (i, 0)),
            pl.BlockSpec((tm, GDN_VAL_DIM), lambda n, i: (i, 0)),
            pl.BlockSpec((tm, tn), lambda n, i: (i, g0_blk + n)),
            pl.BlockSpec((tm, tn), lambda n, i: (i, g1_blk + n)),
            pl.BlockSpec((2, tn), lambda n, i: (0, n)),
            pl.BlockSpec((SSM_D_INNER, tn), lambda n, i: (0, n)),
            pl.BlockSpec((GDN_VAL_DIM, tn), lambda n, i: (0, n)),
        ],
        out_specs=pl.BlockSpec((tm, tn), lambda n, i: (i, n)),
        out_shape=jax.ShapeDtypeStruct((t, D_MODEL), BF16),
        compiler_params=_params(2),
        name="mix",
    )(y_ssm, y_gdn, proj, proj, gate_b, w_s, w_g)


_N_RANK = PEER_TOPK + 1
_CAND_PAIRS = [(i, j) for i in range(_N_RANK) for j in range(_N_RANK) if (i + 1) * (j + 1) <= _N_RANK]
_CAND_ROWS = -(-len(_CAND_PAIRS) // 8) * 8


def _top_desc(s, n):
    out = []
    for _ in range(n):
        m = jnp.max(s, axis=0, keepdims=True)
        out.append(m)
        s = jnp.where(s == m, -jnp.inf, s)
    return out


def _post_kernel(x_ref, mix_ref, wo_ref, fw_ref, wq_ref, sk_ref,
                 h1_ref, hnt_ref, s2_ref, e2_ref, th_ref, e1_ref, cand_ref, *, tm):
    h1 = x_ref[...] + _dot(mix_ref[...], wo_ref[...])
    h1_ref[...] = h1
    hn = h1 * lax.rsqrt(jnp.mean(h1 * h1, axis=-1, keepdims=True) + EPS) * fw_ref[...]
    hnt_ref[...] = hn.T.astype(BF16)
    qv = _dot(hn.astype(BF16), wq_ref[...]).astype(BF16)
    nchunk = tm // LANES
    for h in range(PEER_HEADS):
        s1 = _dot_nt(sk_ref[2 * h], qv[:, (2 * h) * LANES:(2 * h + 1) * LANES])
        s2 = _dot_nt(sk_ref[2 * h + 1], qv[:, (2 * h + 1) * LANES:(2 * h + 2) * LANES])
        a1 = _top_desc(s1, _N_RANK)
        a2 = _top_desc(s2, _N_RANK)
        cand_ref[...] = jnp.full(cand_ref.shape, -jnp.inf, F32)
        for r, (i, j) in enumerate(_CAND_PAIRS):
            cand_ref[pl.ds(r, 1), :] = a1[i] + a2[j]
        cand = cand_ref[...]
        c = cand
        n_removed = jnp.zeros((1, tm), F32)
        v16 = jnp.full((1, tm), -jnp.inf, F32)
        v17 = jnp.full((1, tm), -jnp.inf, F32)
        for _ in range(_N_RANK):
            m = jnp.max(c, axis=0, keepdims=True)
            eq = c == m
            cnt = jnp.sum(eq.astype(F32), axis=0, keepdims=True)
            v16 = jnp.where(n_removed < PEER_TOPK, m, v16)
            v17 = jnp.where(n_removed < _N_RANK, m, v17)
            n_removed = n_removed + cnt
            c = jnp.where(eq, -jnp.inf, c)
        tau = 0.5 * (v16 + v17)
        tau = jnp.where(v17 == -jnp.inf, v16, tau)
        m_tot = a1[0] + a2[0]
        z = jnp.sum(jnp.where(cand >= tau, jnp.exp(cand - m_tot), 0.0), axis=0, keepdims=True)
        e2 = jnp.exp(s2 - a2[0]) / z
        e1 = jnp.exp(s1 - a1[0])
        th = tau - s1
        for cc in range(nchunk):
            ls = slice(cc * LANES, (cc + 1) * LANES)
            s2_ref[cc, h] = s2[:, ls]
            e2_ref[cc, h] = e2[:, ls]
            th_ref[cc, h] = th[:, ls]
            e1_ref[cc, h] = e1[:, ls]


def _post(x, mix, w_out, ffn_w, w_q, sub_keys, tm=256):
    t, d = x.shape
    tm = min(tm, t)
    nchunk = tm // LANES
    stat = jax.ShapeDtypeStruct((t // LANES, PEER_HEADS, PEER_N_KEYS, LANES), F32)
    stat_spec = pl.BlockSpec((nchunk, PEER_HEADS, PEER_N_KEYS, LANES), lambda i: (i, 0, 0, 0))
    kern = functools.partial(_post_kernel, tm=tm)
    return pl.pallas_call(
        kern,
        grid=(t // tm,),
        in_specs=[
            pl.BlockSpec((tm, d), lambda i: (i, 0)),
            pl.BlockSpec((tm, d), lambda i: (i, 0)),
            pl.BlockSpec((d, d), lambda i: (0, 0)),
            pl.BlockSpec((1, d), lambda i: (0, 0)),
            pl.BlockSpec((d, d), lambda i: (0, 0)),
            pl.BlockSpec((2 * PEER_HEADS, PEER_N_KEYS, LANES), lambda i: (0, 0, 0)),
        ],
        out_specs=[
            pl.BlockSpec((tm, d), lambda i: (i, 0)),
            pl.BlockSpec((d, tm), lambda i: (0, i)),
            stat_spec, stat_spec, stat_spec, stat_spec,
        ],
        out_shape=[
            jax.ShapeDtypeStruct((t, d), F32),
            jax.ShapeDtypeStruct((d, t), BF16),
            stat, stat, stat, stat,
        ],
        scratch_shapes=[pltpu.VMEM((_CAND_ROWS, tm), F32)],
        compiler_params=_params(1),
        name="post",
    )(x, mix, w_out, ffn_w, w_q, sub_keys)


_ROW_TILE = 32


def _gelu(x):
    return 0.5 * x * (1.0 + lax.erf(x * (2.0 ** -0.5)))


def _peer_kernel(hnt_ref, u0_ref, u1_ref, vt0_ref, vt1_ref, s2_ref, e2_ref, th_ref, e1_ref, acc_ref,
                 at0_ref, at1_ref, pt0_ref, pt1_ref, *, tb, eb):
    n1 = eb // PEER_N_KEYS
    nchunk = tb // LANES
    nrt = PEER_N_KEYS // _ROW_TILE
    d = acc_ref.shape[0]

    @pl.when(pl.program_id(1) == 0)
    def _():
        acc_ref[...] = jnp.zeros_like(acc_ref)

    hnt = hnt_ref[...]
    at0_ref[...] = _dot(u0_ref[...], hnt)
    at1_ref[...] = jnp.zeros_like(at1_ref)

    def tile(at_ref, pt_ref, row0, cc, rt):
        ls = slice(cc * LANES, (cc + 1) * LANES)
        rs = pl.ds(pl.multiple_of(rt * _ROW_TILE, _ROW_TILE), _ROW_TILE)
        w = [jnp.zeros((_ROW_TILE, LANES), F32) for _ in range(n1)]
        for h in range(PEER_HEADS):
            s2 = s2_ref[cc, h, rs, :]
            e2 = e2_ref[cc, h, rs, :]
            for i in range(n1):
                th = th_ref[cc, h, row0 + i:row0 + i + 1, :]
                e1 = e1_ref[cc, h, row0 + i:row0 + i + 1, :]
                w[i] = w[i] + jnp.where(s2 >= th, e2, 0.0) * e1
        for i in range(n1):
            r = pl.ds(pl.multiple_of(i * PEER_N_KEYS + rt * _ROW_TILE, _ROW_TILE), _ROW_TILE)
            pt_ref[r, ls] = (w[i] * _gelu(at_ref[r, ls])).astype(BF16)

    c_rows = d // nrt

    k_rows = hnt_ref.shape[0] // nrt

    def body0(rt, carry):
        for cc in range(nchunk):
            tile(at0_ref, pt0_ref, 0, cc, rt)
        r = pl.ds(pl.multiple_of(rt * k_rows, k_rows), k_rows)
        at1_ref[...] += _dot(u1_ref[rt], hnt_ref[r, :])
        return carry
    lax.fori_loop(0, nrt, body0, 0)

    def body1(rt, carry):
        for cc in range(nchunk):
            tile(at1_ref, pt1_ref, n1, cc, rt)
        r = pl.ds(pl.multiple_of(rt * c_rows, c_rows), c_rows)
        acc_ref[r, :] += _dot(vt0_ref[r, :], pt0_ref[...])
        return carry
    lax.fori_loop(0, nrt, body1, 0)

    acc_ref[...] += _dot(vt1_ref[...], pt1_ref[...])


def _peer(hnt, u16, v16, s2, e2, th, e1, tb=512, eb=512):
    d, t = hnt.shape
    tb = min(tb, t)
    nchunk = tb // LANES
    n1 = eb // PEER_N_KEYS
    nk = PEER_N_EXPERTS // (2 * eb)
    nrt = PEER_N_KEYS // _ROW_TILE
    u_odd = jnp.swapaxes(u16.reshape(nk, 2, eb, nrt, d // nrt)[:, 1], 1, 2)
    vt_slabs = jnp.swapaxes(v16.reshape(2 * nk, eb, d), 1, 2)
    assert 2 * n1 == 8, "a pair of expert blocks spans one 8-row group of first-key statistics"
    stat_spec = pl.BlockSpec((nchunk, PEER_HEADS, PEER_N_KEYS, LANES), lambda i, k: (i, 0, 0, 0))
    row_spec = pl.BlockSpec((nchunk, PEER_HEADS, 8, LANES), lambda i, k: (i, 0, k, 0))
    kern = functools.partial(_peer_kernel, tb=tb, eb=eb)
    return pl.pallas_call(
        kern,
        grid=(t // tb, nk),
        in_specs=[
            pl.BlockSpec((d, tb), lambda i, k: (0, i)),
            pl.BlockSpec((eb, d), lambda i, k: (2 * k, 0)),
            pl.BlockSpec((None, nrt, eb, d // nrt), lambda i, k: (k, 0, 0, 0)),
            pl.BlockSpec((None, d, eb), lambda i, k: (2 * k, 0, 0)),
            pl.BlockSpec((None, d, eb), lambda i, k: (2 * k + 1, 0, 0)),
            stat_spec, stat_spec, row_spec, row_spec,
        ],
        out_specs=pl.BlockSpec((d, tb), lambda i, k: (0, i)),
        out_shape=jax.ShapeDtypeStruct((d, t), F32),
        scratch_shapes=[pltpu.VMEM((eb, tb), F32), pltpu.VMEM((eb, tb), F32),
                        pltpu.VMEM((eb, tb), BF16), pltpu.VMEM((eb, tb), BF16)],
        compiler_params=_params(2),
        name="peer",
    )(hnt, u16, u_odd, vt_slabs, vt_slabs, s2, e2, th, e1)


def _final_kernel(h_ref, pt_ref, w_ref, o_ref):
    hf = h_ref[...] + pt_ref[...].T
    o_ref[...] = hf * lax.rsqrt(jnp.mean(hf * hf, axis=-1, keepdims=True) + EPS) * w_ref[...]


def _final(h1, peer_out_t, w_row, tm=512):
    t, d = h1.shape
    tm = min(tm, t)
    spec = pl.BlockSpec((tm, d), lambda i: (i, 0))
    return pl.pallas_call(
        _final_kernel,
        grid=(t // tm,),
        in_specs=[spec, pl.BlockSpec((d, tm), lambda i: (0, i)), pl.BlockSpec((1, d), lambda i: (0, 0))],
        out_specs=spec,
        out_shape=jax.ShapeDtypeStruct((t, d), F32),
        compiler_params=_params(1),
        name="final",
    )(h1, peer_out_t, w_row)


def _row(v, width=None):
    v = v.astype(F32).reshape(1, -1)
    if width is not None and v.shape[1] < width:
        v = jnp.pad(v, ((0, 0), (0, width - v.shape[1])))
    return v


def _layer(h, mix_norm_w, w_in, gate_b, ssm_conv_w, ssm_conv_b, ssm_dt_bias, ssm_a_log, ssm_d,
           ssm_norm_w, gdn_conv_w, gdn_dt_bias, gdn_a_log, gdn_norm_w, w_branch_ssm, w_branch_gdn,
           w_out, ffn_norm_w, peer_w_q, peer_sub_keys, peer_u, peer_v, out_norm_w):
    ssm_proj = SSM_D_INNER + (SSM_D_INNER + 2 * SSM_N_GROUPS * SSM_D_STATE) + SSM_N_HEADS
    gdn_conv_dim = 2 * GDN_KEY_DIM + GDN_VAL_DIM
    o_dt = ssm_proj - SSM_N_HEADS
    o_qkv = ssm_proj
    o_gz = o_qkv + gdn_conv_dim
    o_beta = o_gz + GDN_VAL_DIM
    o_gate = o_beta + 2 * GDN_N_HEADS
    w_main = jnp.concatenate(
        [w_in[:, :o_dt], w_in[:, o_qkv:o_beta], w_in[:, o_gate:]], axis=1).astype(BF16)
    w_small = jnp.concatenate(
        [w_in[:, o_dt:o_qkv], w_in[:, o_beta:o_gate],
         jnp.zeros((D_MODEL, LANES - SSM_N_HEADS - 2 * GDN_N_HEADS), w_in.dtype)], axis=1).astype(BF16)

    proj, small = _inproj(h, _row(mix_norm_w), w_main, w_small)

    cw = ssm_conv_w.astype(F32)
    cbias = _row(ssm_conv_b)
    nx = SSM_D_INNER
    nb = SSM_N_GROUPS * SSM_D_STATE
    y_ssm = _ssd(proj, small,
                 cw[:, :nx], cw[:, nx:nx + nb], cw[:, nx + nb:],
                 cbias[:, :nx], cbias[:, nx:nx + nb], cbias[:, nx + nb:],
                 _row(ssm_dt_bias, LANES), _row(ssm_a_log, LANES),
                 _row(jnp.repeat(ssm_d, SSM_HEAD_DIM)), _row(ssm_norm_w))

    gw = gdn_conv_w.astype(F32)
    zeros_b = jnp.zeros((SMALL_ALPHA,), F32)
    y_gdn = _gdn(proj, small,
                 gw[:, :GDN_KEY_DIM], gw[:, GDN_KEY_DIM:2 * GDN_KEY_DIM], gw[:, 2 * GDN_KEY_DIM:],
                 _row(jnp.concatenate([zeros_b, gdn_dt_bias.astype(F32)]), LANES),
                 _row(jnp.concatenate([zeros_b, gdn_a_log.astype(F32)]), LANES),
                 _row(gdn_norm_w))

    mix = _mix(y_ssm, y_gdn, proj, gate_b.astype(F32),
               w_branch_ssm.astype(BF16), w_branch_gdn.astype(BF16))

    sk = peer_sub_keys.reshape(2 * PEER_HEADS, PEER_N_KEYS, LANES).astype(BF16)
    h1, hnt, s2, e2, th, e1 = _post(h, mix, w_out.astype(BF16), _row(ffn_norm_w),
                                    peer_w_q.astype(BF16), sk)

    peer_out_t = _peer(hnt, peer_u.astype(BF16), peer_v.astype(BF16), s2, e2, th, e1)
    return _final(h1, peer_out_t, _row(out_norm_w))


def kernel(x, mix_norm_w, w_in, gate_b, ssm_conv_w, ssm_conv_b, ssm_dt_bias, ssm_a_log, ssm_d,
           ssm_norm_w, gdn_conv_w, gdn_dt_bias, gdn_a_log, gdn_norm_w, w_branch_ssm, w_branch_gdn,
           w_out, ffn_norm_w, peer_w_q, peer_sub_keys, peer_u, peer_v, final_norm_w):
    b, t, d = x.shape
    assert b == 1 and mix_norm_w.shape[0] == 1, "single sequence, single layer"
    out = _layer(x[0], mix_norm_w[0], w_in[0], gate_b[0], ssm_conv_w[0], ssm_conv_b[0],
                 ssm_dt_bias[0], ssm_a_log[0], ssm_d[0], ssm_norm_w[0], gdn_conv_w[0],
                 gdn_dt_bias[0], gdn_a_log[0], gdn_norm_w[0], w_branch_ssm[0], w_branch_gdn[0],
                 w_out[0], ffn_norm_w[0], peer_w_q[0], peer_sub_keys[0], peer_u[0], peer_v[0],
                 final_norm_w)
    return out[None]
```

```python
import functools

import jax
import jax.numpy as jnp
from jax import lax
from jax.experimental import pallas as pl
from jax.experimental.pallas import tpu as pltpu

F32 = jnp.float32
BF16 = jnp.bfloat16
HIGHEST = lax.Precision.HIGHEST

EPS = 1e-6
D_MODEL = 2048
LANES = 128
SSM_D_INNER = 4096
SSM_HEAD_DIM = 64
SSM_N_HEADS = 64
SSM_N_GROUPS = 8
SSM_HPG = 8
SSM_D_STATE = 128
SSM_CHUNK = 128
SSM_GROUP_W = SSM_HPG * SSM_HEAD_DIM
GDN_N_HEADS = 16
GDN_HEAD_K = 128
GDN_HEAD_V = 256
GDN_CHUNK = 64
GDN_KEY_DIM = GDN_N_HEADS * GDN_HEAD_K
GDN_VAL_DIM = GDN_N_HEADS * GDN_HEAD_V
CONV_K = 4
PEER_HEADS = 8
PEER_N_KEYS = 128
PEER_TOPK = 16
PEER_N_EXPERTS = PEER_N_KEYS * PEER_N_KEYS

COL_SSM_Z = 0
COL_SSM_XBC = 4096
COL_GDN_QKV = 10240
COL_GDN_Z = 18432
COL_GATE = 22528
N_MAIN = 26624
SMALL_BETA = 64
SMALL_ALPHA = 80

VMEM_LIMIT = 56 * 1024 * 1024


def _softplus(x):
    return jnp.maximum(x, 0.0) + jnp.log1p(jnp.exp(-jnp.abs(x)))


def _silu(x):
    return x * jax.nn.sigmoid(x)


def _dot(a, b, **kw):
    return jnp.dot(a, b, preferred_element_type=F32, **kw)


def _dot_nt(a, b, **kw):
    return lax.dot_general(a, b, (((1,), (1,)), ((), ())), preferred_element_type=F32, **kw)


def _dot_tn(a, b, **kw):
    return lax.dot_general(a, b, (((0,), (0,)), ((), ())), preferred_element_type=F32, **kw)


def _params(n_grid):
    return pltpu.CompilerParams(dimension_semantics=("arbitrary",) * n_grid,
                                vmem_limit_bytes=VMEM_LIMIT)


def _inproj_kernel(x_ref, nw_ref, w_ref, ws_ref, o_ref, os_ref, xn_ref):
    @pl.when(pl.program_id(1) == 0)
    def _():
        x = x_ref[...]
        ms = jnp.mean(x * x, axis=-1, keepdims=True)
        xn = (x * lax.rsqrt(ms + EPS) * nw_ref[...]).astype(BF16)
        xn_ref[...] = xn
        os_ref[...] = _dot(xn, ws_ref[...])

    o_ref[...] = _dot(xn_ref[...], w_ref[...]).astype(BF16)


def _inproj(x, norm_w, w_main, w_small, tm=1024, tn=1024):
    t, d = x.shape
    n = w_main.shape[1]
    tm = min(tm, t)
    return pl.pallas_call(
        _inproj_kernel,
        grid=(t // tm, n // tn),
        in_specs=[
            pl.BlockSpec((tm, d), lambda i, j: (i, 0)),
            pl.BlockSpec((1, d), lambda i, j: (0, 0)),
            pl.BlockSpec((d, tn), lambda i, j: (0, j)),
            pl.BlockSpec((d, LANES), lambda i, j: (0, 0)),
        ],
        out_specs=[
            pl.BlockSpec((tm, tn), lambda i, j: (i, j)),
            pl.BlockSpec((tm, LANES), lambda i, j: (i, 0)),
        ],
        out_shape=[jax.ShapeDtypeStruct((t, n), BF16), jax.ShapeDtypeStruct((t, LANES), F32)],
        scratch_shapes=[pltpu.VMEM((tm, d), BF16)],
        compiler_params=_params(2),
        name="inproj",
    )(x, norm_w, w_main, w_small)


def _shift_mats(tc):
    return jnp.concatenate([jnp.eye(tc, k=-(CONV_K - 1 - k), dtype=BF16) for k in range(CONV_K - 1)], axis=0)


def _conv_block(raw_ref, carry_ref, w_ref, shift_ref, tc):
    x16 = raw_ref[...]
    xf = x16.astype(F32)
    y = xf * w_ref[CONV_K - 1:CONV_K, :]
    c8 = carry_ref[...]
    row8 = lax.broadcasted_iota(jnp.int32, c8.shape, 0)
    head = jnp.zeros_like(c8)
    shifted = _dot(shift_ref[...], x16)
    for k in range(CONV_K - 1):
        lag = CONV_K - 1 - k
        y = y + shifted[k * tc:(k + 1) * tc] * w_ref[k:k + 1, :]
        head = head + jnp.where(row8 < lag, pltpu.roll(c8, shift=lag, axis=0), 0.0) * w_ref[k:k + 1, :]
    carry_ref[...] = xf[tc - 8:tc]
    return jnp.concatenate([y[0:8] + head, y[8:]], axis=0)


def _ssd_kernel(xs_ref, b_ref, c_ref, z_ref, sm_ref, cwx_ref, cwb_ref, cwc_ref, cbx_ref, cbb_ref,
                cbc_ref, dtb_ref, alog_ref, dsk_ref, nw_ref, sh_ref, o_ref,
                xbuf, bbuf, cbuf, state_ref, acst_ref, *, tc):
    g = pl.program_id(0)
    L = SSM_CHUNK

    @pl.when(pl.program_id(1) == 0)
    def _():
        xbuf[...] = jnp.zeros_like(xbuf)
        bbuf[...] = jnp.zeros_like(bbuf)
        cbuf[...] = jnp.zeros_like(cbuf)
        state_ref[...] = jnp.zeros_like(state_ref)

    xs_all = _silu(_conv_block(xs_ref, xbuf, cwx_ref, sh_ref, tc) + cbx_ref[...])
    b_all = _silu(_conv_block(b_ref, bbuf, cwb_ref, sh_ref, tc) + cbb_ref[...])
    c_all = _silu(_conv_block(c_ref, cbuf, cwc_ref, sh_ref, tc) + cbc_ref[...])

    row = lax.broadcasted_iota(jnp.int32, (L, L), 0)
    col = lax.broadcasted_iota(jnp.int32, (L, L), 1)
    causal = row >= col
    tri = causal.astype(F32)
    lo_half = col < SSM_HEAD_DIM
    a_row = -jnp.exp(alog_ref[...])

    for c in range(tc // L):
        sl = slice(c * L, (c + 1) * L)
        xs, bm, cm = xs_all[sl], b_all[sl], c_all[sl]
        dt_full = _softplus(sm_ref[sl, :] + dtb_ref[...])
        acs_full = _dot(tri, dt_full * a_row, precision=HIGHEST)
        acst_ref[...] = acs_full.T
        bm16, cm16 = bm.astype(BF16), cm.astype(BF16)
        cb = _dot_nt(cm16, bm16)
        ys = []
        for p in range(SSM_HPG // 2):
            acs_bc, dt_bc, scores = [], [], []
            for r in (2 * p, 2 * p + 1):
                j = g * SSM_HPG + r
                sel = col == j
                a_c = jnp.sum(jnp.where(sel, acs_full, 0.0), axis=1, keepdims=True)
                d_c = jnp.sum(jnp.where(sel, dt_full, 0.0), axis=1, keepdims=True)
                a_b = jnp.broadcast_to(a_c, (L, L))
                acs_bc.append(a_b)
                dt_bc.append(jnp.broadcast_to(d_c, (L, L)))
                a_r = acst_ref[pl.ds(j, 1), :]
                seg = jnp.where(causal, a_b - a_r, -jnp.inf)
                scores.append((cb * jnp.exp(seg)).astype(BF16))
            acs_e = jnp.where(lo_half, acs_bc[0], acs_bc[1])
            dt_e = jnp.where(lo_half, dt_bc[0], dt_bc[1])
            lanes = slice(p * LANES, (p + 1) * LANES)
            x_p = xs[:, lanes]
            xdt = x_p * dt_e
            last = acs_e[L - 1:L, :]
            xdtw = (xdt * jnp.exp(last - acs_e)).astype(BF16)
            s2 = jnp.concatenate(scores, axis=1)
            x2 = jnp.concatenate([jnp.where(lo_half, xdt, 0.0), jnp.where(lo_half, 0.0, xdt)],
                                 axis=0).astype(BF16)
            y_diag = _dot(s2, x2)
            st = state_ref[:, lanes]
            y_off = _dot(cm16, st.astype(BF16)) * jnp.exp(acs_e)
            state_ref[:, lanes] = st * jnp.exp(last) + _dot_tn(bm16, xdtw)
            ys.append(y_diag + y_off + x_p * dsk_ref[:, lanes])
        y = jnp.concatenate(ys, axis=1) * _silu(z_ref[sl, :].astype(F32))
        y = y * lax.rsqrt(jnp.mean(y * y, axis=-1, keepdims=True) + EPS)
        o_ref[sl, :] = (y * nw_ref[...]).astype(BF16)


def _ssd(proj, small, cwx, cwb, cwc, cbx, cbb, cbc, dtb_row, alog_row, dsk_row, nw_row, tc=256):
    t = proj.shape[0]
    tc = min(tc, t)
    gw, ns = SSM_GROUP_W, SSM_D_STATE
    xs_blk = COL_SSM_XBC // gw
    b_blk = (COL_SSM_XBC + SSM_D_INNER) // ns
    c_blk = b_blk + SSM_N_GROUPS
    z_blk = COL_SSM_Z // gw
    kern = functools.partial(_ssd_kernel, tc=tc)
    return pl.pallas_call(
        kern,
        grid=(SSM_N_GROUPS, t // tc),
        in_specs=[
            pl.BlockSpec((tc, gw), lambda g, i: (i, xs_blk + g)),
            pl.BlockSpec((tc, ns), lambda g, i: (i, b_blk + g)),
            pl.BlockSpec((tc, ns), lambda g, i: (i, c_blk + g)),
            pl.BlockSpec((tc, gw), lambda g, i: (i, z_blk + g)),
            pl.BlockSpec((tc, LANES), lambda g, i: (i, 0)),
            pl.BlockSpec((CONV_K, gw), lambda g, i: (0, g)),
            pl.BlockSpec((CONV_K, ns), lambda g, i: (0, g)),
            pl.BlockSpec((CONV_K, ns), lambda g, i: (0, g)),
            pl.BlockSpec((1, gw), lambda g, i: (0, g)),
            pl.BlockSpec((1, ns), lambda g, i: (0, g)),
            pl.BlockSpec((1, ns), lambda g, i: (0, g)),
            pl.BlockSpec((1, LANES), lambda g, i: (0, 0)),
            pl.BlockSpec((1, LANES), lambda g, i: (0, 0)),
            pl.BlockSpec((1, gw), lambda g, i: (0, g)),
            pl.BlockSpec((1, gw), lambda g, i: (0, g)),
            pl.BlockSpec(((CONV_K - 1) * tc, tc), lambda g, i: (0, 0)),
        ],
        out_specs=pl.BlockSpec((tc, gw), lambda g, i: (i, g)),
        out_shape=jax.ShapeDtypeStruct((t, SSM_D_INNER), BF16),
        scratch_shapes=[
            pltpu.VMEM((8, gw), F32),
            pltpu.VMEM((8, ns), F32),
            pltpu.VMEM((8, ns), F32),
            pltpu.VMEM((ns, gw), F32),
            pltpu.VMEM((LANES, SSM_CHUNK), F32),
        ],
        compiler_params=_params(2),
        name="ssd",
    )(proj, proj, proj, proj, small, cwx, cwb, cwc, cbx, cbb, cbc, dtb_row, alog_row, dsk_row, nw_row,
      _shift_mats(tc))


def _gdn_kernel(q_ref, k_ref, v_ref, z_ref, sm_ref, cwq_ref, cwk_ref, cwv_ref, dtb_ref, alog_ref,
                nw_ref, sh_ref, o_ref, qbuf, kbuf, vbuf, s_ref, *, tc, hps):
    h0 = pl.program_id(0) * hps
    C, DK, DV = GDN_CHUNK, GDN_HEAD_K, GDN_HEAD_V

    @pl.when(pl.program_id(1) == 0)
    def _():
        qbuf[...] = jnp.zeros_like(qbuf)
        kbuf[...] = jnp.zeros_like(kbuf)
        vbuf[...] = jnp.zeros_like(vbuf)
        s_ref[...] = jnp.zeros_like(s_ref)

    q_cv = _silu(_conv_block(q_ref, qbuf, cwq_ref, sh_ref, tc))
    k_cv = _silu(_conv_block(k_ref, kbuf, cwk_ref, sh_ref, tc))
    v_cv = _silu(_conv_block(v_ref, vbuf, cwv_ref, sh_ref, tc))

    sm = sm_ref[...]
    lane = lax.broadcasted_iota(jnp.int32, sm.shape, 1)
    beta_full = jax.nn.sigmoid(sm)
    g_full = -jnp.exp(alog_ref[...]) * _softplus(sm + dtb_ref[...])
    q_hd, k_hd, v_hd, beta_hd, g_hd = [], [], [], [], []
    for hd in range(hps):
        q = q_cv[:, hd * DK:(hd + 1) * DK]
        k = k_cv[:, hd * DK:(hd + 1) * DK]
        q_hd.append(q * lax.rsqrt(jnp.sum(q * q, axis=-1, keepdims=True) + EPS) * (DK ** -0.5))
        k_hd.append(k * lax.rsqrt(jnp.sum(k * k, axis=-1, keepdims=True) + EPS))
        v_hd.append(v_cv[:, hd * DV:(hd + 1) * DV])
        beta_hd.append(jnp.sum(jnp.where(lane == SMALL_BETA + h0 + hd, beta_full, 0.0),
                               axis=1, keepdims=True))
        g_hd.append(jnp.sum(jnp.where(lane == SMALL_ALPHA + h0 + hd, g_full, 0.0),
                            axis=1, keepdims=True))

    row = lax.broadcasted_iota(jnp.int32, (C, C), 0)
    col = lax.broadcasted_iota(jnp.int32, (C, C), 1)
    causal = row >= col
    strict = row > col
    tri = causal.astype(F32)
    eye = (row == col).astype(F32)

    nc = tc // C
    items = [(hd, slice(c * C, (c + 1) * C)) for hd in range(hps) for c in range(nc)]
    rng = range(len(items))
    strict_w = (lax.broadcasted_iota(jnp.int32, (C, LANES), 0)
                > lax.broadcasted_iota(jnp.int32, (C, LANES), 1))
    qs = [q_hd[hd][sl] for hd, sl in items]
    ks = [k_hd[hd][sl] for hd, sl in items]
    vs = [v_hd[hd][sl] for hd, sl in items]
    betas = [beta_hd[hd][sl] for hd, sl in items]
    g_bc = [jnp.broadcast_to(g_hd[hd][sl], (C, LANES)) for hd, sl in items]
    cums = [_dot(tri, jnp.concatenate([g_bc[c], jnp.where(strict_w, g_bc[c], 0.0)], axis=1),
                 precision=HIGHEST) for c in rng]
    gc_bc = [x[:, :LANES] for x in cums]
    decay = [jnp.exp(jnp.where(causal, x[:, LANES:LANES + C], -jnp.inf)) for x in cums]
    k16 = [k.astype(BF16) for k in ks]
    q16 = [q.astype(BF16) for q in qs]
    kk = [_dot_nt(k16[c], k16[c]) for c in rng]
    qk = [_dot_nt(q16[c], k16[c]) for c in rng]
    p = [-jnp.where(strict, kk[c] * betas[c] * decay[c], 0.0) for c in rng]
    x = [eye + p[c] for c in rng]
    for _ in range(C.bit_length() - 2):
        p16 = [pp.astype(BF16) for pp in p]
        p = [_dot(p16[c], p16[c]) for c in rng]
        x = [x[c] + _dot(x[c].astype(BF16), p[c].astype(BF16)) for c in rng]
    t_inv = [xx.astype(BF16) for xx in x]
    egc = [jnp.exp(gc) for gc in gc_bc]
    u = [_dot(t_inv[c], (vs[c] * betas[c]).astype(BF16)).astype(BF16) for c in rng]
    w = [_dot(t_inv[c], (ks[c] * (betas[c] * egc[c])).astype(BF16)).astype(BF16) for c in rng]
    attn = [jnp.where(causal, qk[c] * decay[c], 0.0).astype(BF16) for c in rng]
    gc_last = [gc[C - 1:C, :] for gc in gc_bc]
    k_dec = [(ks[c] * jnp.exp(gc_last[c] - gc_bc[c])).astype(BF16) for c in rng]
    c_dec = [jnp.exp(gl[:, :1]) for gl in gc_last]
    a_mat = [_dot_tn(k_dec[c], w[c]).astype(BF16) for c in rng]
    n_mat = [_dot_tn(k_dec[c], u[c]) for c in rng]
    q_eff = [(qs[c] * egc[c] - _dot(attn[c], w[c])).astype(BF16) for c in rng]
    o_u = [_dot(attn[c], u[c]) for c in rng]

    s = [s_ref[hd] for hd in range(hps)]
    for c in range(nc):
        sl = slice(c * C, (c + 1) * C)
        for hd in range(hps):
            it = hd * nc + c
            s16 = s[hd].astype(BF16)
            o = _dot(q_eff[it], s16) + o_u[it]
            s[hd] = s[hd] * c_dec[it] - _dot(a_mat[it], s16) + n_mat[it]
            o = o * lax.rsqrt(jnp.mean(o * o, axis=-1, keepdims=True) + EPS) * nw_ref[...]
            z = z_ref[sl, hd * DV:(hd + 1) * DV].astype(F32)
            o_ref[sl, hd * DV:(hd + 1) * DV] = (o * _silu(z)).astype(BF16)
    for hd in range(hps):
        s_ref[hd] = s[hd]


def _gdn(proj, small, cwq, cwk, cwv, dtb_row, alog_row, nw_row, tc=256, hps=4):
    t = proj.shape[0]
    tc = min(tc, t)
    dk, dv = hps * GDN_HEAD_K, hps * GDN_HEAD_V
    q_blk = COL_GDN_QKV // dk
    k_blk = q_blk + GDN_N_HEADS // hps
    v_blk = (COL_GDN_QKV + 2 * GDN_KEY_DIM) // dv
    z_blk = COL_GDN_Z // dv
    kern = functools.partial(_gdn_kernel, tc=tc, hps=hps)
    return pl.pallas_call(
        kern,
        grid=(GDN_N_HEADS // hps, t // tc),
        in_specs=[
            pl.BlockSpec((tc, dk), lambda h, i: (i, q_blk + h)),
            pl.BlockSpec((tc, dk), lambda h, i: (i, k_blk + h)),
            pl.BlockSpec((tc, dv), lambda h, i: (i, v_blk + h)),
            pl.BlockSpec((tc, dv), lambda h, i: (i, z_blk + h)),
            pl.BlockSpec((tc, LANES), lambda h, i: (i, 0)),
            pl.BlockSpec((CONV_K, dk), lambda h, i: (0, h)),
            pl.BlockSpec((CONV_K, dk), lambda h, i: (0, h)),
            pl.BlockSpec((CONV_K, dv), lambda h, i: (0, h)),
            pl.BlockSpec((1, LANES), lambda h, i: (0, 0)),
            pl.BlockSpec((1, LANES), lambda h, i: (0, 0)),
            pl.BlockSpec((1, GDN_HEAD_V), lambda h, i: (0, 0)),
            pl.BlockSpec(((CONV_K - 1) * tc, tc), lambda h, i: (0, 0)),
        ],
        out_specs=pl.BlockSpec((tc, dv), lambda h, i: (i, h)),
        out_shape=jax.ShapeDtypeStruct((t, GDN_VAL_DIM), BF16),
        scratch_shapes=[
            pltpu.VMEM((8, dk), F32),
            pltpu.VMEM((8, dk), F32),
            pltpu.VMEM((8, dv), F32),
            pltpu.VMEM((hps, GDN_HEAD_K, GDN_HEAD_V), F32),
        ],
        compiler_params=_params(2),
        name="gdn",
    )(proj, proj, proj, proj, small, cwq, cwk, cwv, dtb_row, alog_row, nw_row, _shift_mats(tc))


def _mix_kernel(ys_ref, yg_ref, g0_ref, g1_ref, gb_ref, ws_ref, wg_ref, o_ref):
    a = _dot(ys_ref[...], ws_ref[...])
    b = _dot(yg_ref[...], wg_ref[...])
    g0 = jax.nn.sigmoid(g0_ref[...].astype(F32) + gb_ref[0:1, :])
    g1 = jax.nn.sigmoid(g1_ref[...].astype(F32) + gb_ref[1:2, :])
    o_ref[...] = (g0 * a + g1 * b).astype(BF16)


def _mix(y_ssm, y_gdn, proj, gate_b, w_s, w_g, tm=512, tn=512):
    t = y_ssm.shape[0]
    tm = min(tm, t)
    g0_blk = COL_GATE // tn
    g1_blk = (COL_GATE + D_MODEL) // tn
    return pl.pallas_call(
        _mix_kernel,
        grid=(D_MODEL // tn, t // tm),
        in_specs=[
            pl.BlockSpec((tm, SSM_D_INNER), lambda n, i: (i, 0)),
            pl.BlockSpec((tm, GDN_VAL_DIM), lambda n, i: (i, 0)),
            pl.BlockSpec((tm, tn), lambda n, i: (i, g0_blk + n)),
            pl.BlockSpec((tm, tn), lambda n, i: (i, g1_blk + n)),
            pl.BlockSpec((2, tn), lambda n, i: (0, n)),
            pl.BlockSpec((SSM_D_INNER, tn), lambda n, i: (0, n)),
            pl.BlockSpec((GDN_VAL_DIM, tn), lambda n, i: (0, n)),
        ],
        out_specs=pl.BlockSpec((tm, tn), lambda n, i: (i, n)),
        out_shape=jax.ShapeDtypeStruct((t, D_MODEL), BF16),
        compiler_params=_params(2),
        name="mix",
    )(y_ssm, y_gdn, proj, proj, gate_b, w_s, w_g)


_N_RANK = PEER_TOPK + 1
_CAND_PAIRS = [(i, j) for i in range(_N_RANK) for j in range(_N_RANK) if (i + 1) * (j + 1) <= _N_RANK]
_CAND_ROWS = -(-len(_CAND_PAIRS) // 8) * 8


def _top_desc(s, n):
    out = []
    for _ in range(n):
        m = jnp.max(s, axis=0, keepdims=True)
        out.append(m)
        s = jnp.where(s == m, -jnp.inf, s)
    return out


def _post_kernel(x_ref, mix_ref, wo_ref, fw_ref, wq_ref, sk_ref,
                 h1_ref, hnt_ref, s2_ref, e2_ref, th_ref, e1_ref, cand_ref, *, tm):
    h1 = x_ref[...] + _dot(mix_ref[...], wo_ref[...])
    h1_ref[...] = h1
    hn = h1 * lax.rsqrt(jnp.mean(h1 * h1, axis=-1, keepdims=True) + EPS) * fw_ref[...]
    hnt_ref[...] = hn.T.astype(BF16)
    qv = _dot(hn.astype(BF16), wq_ref[...]).astype(BF16)
    nchunk = tm // LANES
    for h in range(PEER_HEADS):
        s1 = _dot_nt(sk_ref[2 * h], qv[:, (2 * h) * LANES:(2 * h + 1) * LANES])
        s2 = _dot_nt(sk_ref[2 * h + 1], qv[:, (2 * h + 1) * LANES:(2 * h + 2) * LANES])
        a1 = _top_desc(s1, _N_RANK)
        a2 = _top_desc(s2, _N_RANK)
        cand_ref[...] = jnp.full(cand_ref.shape, -jnp.inf, F32)
        for r, (i, j) in enumerate(_CAND_PAIRS):
            cand_ref[pl.ds(r, 1), :] = a1[i] + a2[j]
        cand = cand_ref[...]
        c = cand
        n_removed = jnp.zeros((1, tm), F32)
        v16 = jnp.full((1, tm), -jnp.inf, F32)
        v17 = jnp.full((1, tm), -jnp.inf, F32)
        for _ in range(_N_RANK):
            m = jnp.max(c, axis=0, keepdims=True)
            eq = c == m
            cnt = jnp.sum(eq.astype(F32), axis=0, keepdims=True)
            v16 = jnp.where(n_removed < PEER_TOPK, m, v16)
            v17 = jnp.where(n_removed < _N_RANK, m, v17)
            n_removed = n_removed + cnt
            c = jnp.where(eq, -jnp.inf, c)
        tau = 0.5 * (v16 + v17)
        tau = jnp.where(v17 == -jnp.inf, v16, tau)
        m_tot = a1[0] + a2[0]
        z = jnp.sum(jnp.where(cand >= tau, jnp.exp(cand - m_tot), 0.0), axis=0, keepdims=True)
        e2 = jnp.exp(s2 - a2[0]) / z
        e1 = jnp.exp(s1 - a1[0])
        th = tau - s1
        for cc in range(nchunk):
            ls = slice(cc * LANES, (cc + 1) * LANES)
            s2_ref[cc, h] = s2[:, ls]
            e2_ref[cc, h] = e2[:, ls]
            th_ref[cc, h] = th[:, ls]
            e1_ref[cc, h] = e1[:, ls]


def _post(x, mix, w_out, ffn_w, w_q, sub_keys, tm=256):
    t, d = x.shape
    tm = min(tm, t)
    nchunk = tm // LANES
    stat = jax.ShapeDtypeStruct((t // LANES, PEER_HEADS, PEER_N_KEYS, LANES), F32)
    stat_spec = pl.BlockSpec((nchunk, PEER_HEADS, PEER_N_KEYS, LANES), lambda i: (i, 0, 0, 0))
    kern = functools.partial(_post_kernel, tm=tm)
    return pl.pallas_call(
        kern,
        grid=(t // tm,),
        in_specs=[
            pl.BlockSpec((tm, d), lambda i: (i, 0)),
            pl.BlockSpec((tm, d), lambda i: (i, 0)),
            pl.BlockSpec((d, d), lambda i: (0, 0)),
            pl.BlockSpec((1, d), lambda i: (0, 0)),
            pl.BlockSpec((d, d), lambda i: (0, 0)),
            pl.BlockSpec((2 * PEER_HEADS, PEER_N_KEYS, LANES), lambda i: (0, 0, 0)),
        ],
        out_specs=[
            pl.BlockSpec((tm, d), lambda i: (i, 0)),
            pl.BlockSpec((d, tm), lambda i: (0, i)),
            stat_spec, stat_spec, stat_spec, stat_spec,
        ],
        out_shape=[
            jax.ShapeDtypeStruct((t, d), F32),
            jax.ShapeDtypeStruct((d, t), BF16),
            stat, stat, stat, stat,
        ],
        scratch_shapes=[pltpu.VMEM((_CAND_ROWS, tm), F32)],
        compiler_params=_params(1),
        name="post",
    )(x, mix, w_out, ffn_w, w_q, sub_keys)


_ROW_TILE = 32


def _gelu(x):
    return 0.5 * x * (1.0 + lax.erf(x * (2.0 ** -0.5)))


def _peer_kernel(hnt_ref, u0_ref, u1_ref, vt0_ref, vtp_ref, vtl_ref, s2_ref, e2_ref, th_ref, e1_ref, acc_ref,
                 at0_ref, at1_ref, pt0_ref, pt1_ref, *, tb, eb):
    n1 = eb // PEER_N_KEYS
    nchunk = tb // LANES
    nrt = PEER_N_KEYS // _ROW_TILE
    d = acc_ref.shape[0]

    @pl.when(pl.program_id(1) == 0)
    def _():
        acc_ref[...] = jnp.zeros_like(acc_ref)
        pt1_ref[...] = jnp.zeros_like(pt1_ref)

    hnt = hnt_ref[...]
    at0_ref[...] = _dot(u0_ref[...], hnt)
    at1_ref[...] = _dot(u1_ref[...], hnt)

    def tile(at_ref, pt_ref, row0, cc, rt):
        ls = slice(cc * LANES, (cc + 1) * LANES)
        rs = pl.ds(pl.multiple_of(rt * _ROW_TILE, _ROW_TILE), _ROW_TILE)
        w = [jnp.zeros((_ROW_TILE, LANES), F32) for _ in range(n1)]
        for h in range(PEER_HEADS):
            s2 = s2_ref[cc, h, rs, :]
            e2 = e2_ref[cc, h, rs, :]
            for i in range(n1):
                th = th_ref[cc, h, row0 + i:row0 + i + 1, :]
                e1 = e1_ref[cc, h, row0 + i:row0 + i + 1, :]
                w[i] = w[i] + jnp.where(s2 >= th, e2, 0.0) * e1
        for i in range(n1):
            r = pl.ds(pl.multiple_of(i * PEER_N_KEYS + rt * _ROW_TILE, _ROW_TILE), _ROW_TILE)
            pt_ref[r, ls] = (w[i] * _gelu(at_ref[r, ls])).astype(BF16)

    c_rows = d // nrt

    def body0(rt, carry):
        for cc in range(nchunk):
            tile(at0_ref, pt0_ref, 0, cc, rt)
        r = pl.ds(pl.multiple_of(rt * c_rows, c_rows), c_rows)
        acc_ref[r, :] += _dot(vtp_ref[r, :], pt1_ref[...])
        return carry
    lax.fori_loop(0, nrt, body0, 0)

    def body1(rt, carry):
        for cc in range(nchunk):
            tile(at1_ref, pt1_ref, n1, cc, rt)
        r = pl.ds(pl.multiple_of(rt * c_rows, c_rows), c_rows)
        acc_ref[r, :] += _dot(vt0_ref[r, :], pt0_ref[...])
        return carry
    lax.fori_loop(0, nrt, body1, 0)

    @pl.when(pl.program_id(1) == pl.num_programs(1) - 1)
    def _():
        acc_ref[...] += _dot(vtl_ref[...], pt1_ref[...])


def _peer(hnt, u16, v16, s2, e2, th, e1, tb=512, eb=512):
    d, t = hnt.shape
    tb = min(tb, t)
    nchunk = tb // LANES
    n1 = eb // PEER_N_KEYS
    nk = PEER_N_EXPERTS // (2 * eb)
    vt_slabs = jnp.swapaxes(v16.reshape(2 * nk, eb, d), 1, 2)
    assert 2 * n1 == 8, "a pair of expert blocks spans one 8-row group of first-key statistics"
    stat_spec = pl.BlockSpec((nchunk, PEER_HEADS, PEER_N_KEYS, LANES), lambda i, k: (i, 0, 0, 0))
    row_spec = pl.BlockSpec((nchunk, PEER_HEADS, 8, LANES), lambda i, k: (i, 0, k, 0))
    kern = functools.partial(_peer_kernel, tb=tb, eb=eb)
    return pl.pallas_call(
        kern,
        grid=(t // tb, nk),
        in_specs=[
            pl.BlockSpec((d, tb), lambda i, k: (0, i)),
            pl.BlockSpec((eb, d), lambda i, k: (2 * k, 0)),
            pl.BlockSpec((eb, d), lambda i, k: (2 * k + 1, 0)),
            pl.BlockSpec((None, d, eb), lambda i, k: (2 * k, 0, 0)),
            pl.BlockSpec((None, d, eb), lambda i, k: (2 * jnp.maximum(k - 1, 0) + 1, 0, 0)),
            pl.BlockSpec((None, d, eb), lambda i, k: (2 * nk - 1, 0, 0)),
            stat_spec, stat_spec, row_spec, row_spec,
        ],
        out_specs=pl.BlockSpec((d, tb), lambda i, k: (0, i)),
        out_shape=jax.ShapeDtypeStruct((d, t), F32),
        scratch_shapes=[pltpu.VMEM((eb, tb), F32), pltpu.VMEM((eb, tb), F32),
                        pltpu.VMEM((eb, tb), BF16), pltpu.VMEM((eb, tb), BF16)],
        compiler_params=_params(2),
        name="peer",
    )(hnt, u16, u16, vt_slabs, vt_slabs, vt_slabs, s2, e2, th, e1)


def _final_kernel(h_ref, pt_ref, w_ref, o_ref):
    hf = h_ref[...] + pt_ref[...].T
    o_ref[...] = hf * lax.rsqrt(jnp.mean(hf * hf, axis=-1, keepdims=True) + EPS) * w_ref[...]


def _final(h1, peer_out_t, w_row, tm=512):
    t, d = h1.shape
    tm = min(tm, t)
    spec = pl.BlockSpec((tm, d), lambda i: (i, 0))
    return pl.pallas_call(
        _final_kernel,
        grid=(t // tm,),
        in_specs=[spec, pl.BlockSpec((d, tm), lambda i: (0, i)), pl.BlockSpec((1, d), lambda i: (0, 0))],
        out_specs=spec,
        out_shape=jax.ShapeDtypeStruct((t, d), F32),
        compiler_params=_params(1),
        name="final",
    )(h1, peer_out_t, w_row)


def _row(v, width=None):
    v = v.astype(F32).reshape(1, -1)
    if width is not None and v.shape[1] < width:
        v = jnp.pad(v, ((0, 0), (0, width - v.shape[1])))
    return v


def _layer(h, mix_norm_w, w_in, gate_b, ssm_conv_w, ssm_conv_b, ssm_dt_bias, ssm_a_log, ssm_d,
           ssm_norm_w, gdn_conv_w, gdn_dt_bias, gdn_a_log, gdn_norm_w, w_branch_ssm, w_branch_gdn,
           w_out, ffn_norm_w, peer_w_q, peer_sub_keys, peer_u, peer_v, out_norm_w):
    ssm_proj = SSM_D_INNER + (SSM_D_INNER + 2 * SSM_N_GROUPS * SSM_D_STATE) + SSM_N_HEADS
    gdn_conv_dim = 2 * GDN_KEY_DIM + GDN_VAL_DIM
    o_dt = ssm_proj - SSM_N_HEADS
    o_qkv = ssm_proj
    o_gz = o_qkv + gdn_conv_dim
    o_beta = o_gz + GDN_VAL_DIM
    o_gate = o_beta + 2 * GDN_N_HEADS
    w_main = jnp.concatenate(
        [w_in[:, :o_dt], w_in[:, o_qkv:o_beta], w_in[:, o_gate:]], axis=1).astype(BF16)
    w_small = jnp.concatenate(
        [w_in[:, o_dt:o_qkv], w_in[:, o_beta:o_gate],
         jnp.zeros((D_MODEL, LANES - SSM_N_HEADS - 2 * GDN_N_HEADS), w_in.dtype)], axis=1).astype(BF16)

    proj, small = _inproj(h, _row(mix_norm_w), w_main, w_small)

    cw = ssm_conv_w.astype(F32)
    cbias = _row(ssm_conv_b)
    nx = SSM_D_INNER
    nb = SSM_N_GROUPS * SSM_D_STATE
    y_ssm = _ssd(proj, small,
                 cw[:, :nx], cw[:, nx:nx + nb], cw[:, nx + nb:],
                 cbias[:, :nx], cbias[:, nx:nx + nb], cbias[:, nx + nb:],
                 _row(ssm_dt_bias, LANES), _row(ssm_a_log, LANES),
                 _row(jnp.repeat(ssm_d, SSM_HEAD_DIM)), _row(ssm_norm_w))

    gw = gdn_conv_w.astype(F32)
    zeros_b = jnp.zeros((SMALL_ALPHA,), F32)
    y_gdn = _gdn(proj, small,
                 gw[:, :GDN_KEY_DIM], gw[:, GDN_KEY_DIM:2 * GDN_KEY_DIM], gw[:, 2 * GDN_KEY_DIM:],
                 _row(jnp.concatenate([zeros_b, gdn_dt_bias.astype(F32)]), LANES),
                 _row(jnp.concatenate([zeros_b, gdn_a_log.astype(F32)]), LANES),
                 _row(gdn_norm_w))

    mix = _mix(y_ssm, y_gdn, proj, gate_b.astype(F32),
               w_branch_ssm.astype(BF16), w_branch_gdn.astype(BF16))

    sk = peer_sub_keys.reshape(2 * PEER_HEADS, PEER_N_KEYS, LANES).astype(BF16)
    h1, hnt, s2, e2, th, e1 = _post(h, mix, w_out.astype(BF16), _row(ffn_norm_w),
                                    peer_w_q.astype(BF16), sk)

    peer_out_t = _peer(hnt, peer_u.astype(BF16), peer_v.astype(BF16), s2, e2, th, e1)
    return _final(h1, peer_out_t, _row(out_norm_w))


def kernel(x, mix_norm_w, w_in, gate_b, ssm_conv_w, ssm_conv_b, ssm_dt_bias, ssm_a_log, ssm_d,
           ssm_norm_w, gdn_conv_w, gdn_dt_bias, gdn_a_log, gdn_norm_w, w_branch_ssm, w_branch_gdn,
           w_out, ffn_norm_w, peer_w_q, peer_sub_keys, peer_u, peer_v, final_norm_w):
    b, t, d = x.shape
    assert b == 1 and mix_norm_w.shape[0] == 1, "single sequence, single layer"
    out = _layer(x[0], mix_norm_w[0], w_in[0], gate_b[0], ssm_conv_w[0], ssm_conv_b[0],
                 ssm_dt_bias[0], ssm_a_log[0], ssm_d[0], ssm_norm_w[0], gdn_conv_w[0],
                 gdn_dt_bias[0], gdn_a_log[0], gdn_norm_w[0], w_branch_ssm[0], w_branch_gdn[0],
                 w_out[0], ffn_norm_w[0], peer_w_q[0], peer_sub_keys[0], peer_u[0], peer_v[0],
                 final_norm_w)
    return out[None]
```

```python
import functools

import jax
import jax.numpy as jnp
from jax import lax
from jax.experimental import pallas as pl
from jax.experimental.pallas import tpu as pltpu

F32 = jnp.float32
BF16 = jnp.bfloat16
HIGHEST = lax.Precision.HIGHEST

EPS = 1e-6
D_MODEL = 2048
LANES = 128
SSM_D_INNER = 4096
SSM_HEAD_DIM = 64
SSM_N_HEADS = 64
SSM_N_GROUPS = 8
SSM_HPG = 8
SSM_D_STATE = 128
SSM_CHUNK = 128
SSM_GROUP_W = SSM_HPG * SSM_HEAD_DIM
GDN_N_HEADS = 16
GDN_HEAD_K = 128
GDN_HEAD_V = 256
GDN_CHUNK = 64
GDN_KEY_DIM = GDN_N_HEADS * GDN_HEAD_K
GDN_VAL_DIM = GDN_N_HEADS * GDN_HEAD_V
CONV_K = 4
PEER_HEADS = 8
PEER_N_KEYS = 128
PEER_TOPK = 16
PEER_N_EXPERTS = PEER_N_KEYS * PEER_N_KEYS

COL_SSM_Z = 0
COL_SSM_XBC = 4096
COL_GDN_QKV = 10240
COL_GDN_Z = 18432
COL_GATE = 22528
N_MAIN = 26624
SMALL_BETA = 64
SMALL_ALPHA = 80

VMEM_LIMIT = 56 * 1024 * 1024


def _softplus(x):
    return jnp.maximum(x, 0.0) + jnp.log1p(jnp.exp(-jnp.abs(x)))


def _silu(x):
    return x * jax.nn.sigmoid(x)


def _dot(a, b, **kw):
    return jnp.dot(a, b, preferred_element_type=F32, **kw)


def _dot_nt(a, b, **kw):
    return lax.dot_general(a, b, (((1,), (1,)), ((), ())), preferred_element_type=F32, **kw)


def _dot_tn(a, b, **kw):
    return lax.dot_general(a, b, (((0,), (0,)), ((), ())), preferred_element_type=F32, **kw)


def _params(n_grid):
    return pltpu.CompilerParams(dimension_semantics=("arbitrary",) * n_grid,
                                vmem_limit_bytes=VMEM_LIMIT)


def _inproj_kernel(x_ref, nw_ref, w_ref, ws_ref, o_ref, os_ref, xn_ref):
    @pl.when(pl.program_id(1) == 0)
    def _():
        x = x_ref[...]
        ms = jnp.mean(x * x, axis=-1, keepdims=True)
        xn = (x * lax.rsqrt(ms + EPS) * nw_ref[...]).astype(BF16)
        xn_ref[...] = xn
        os_ref[...] = _dot(xn, ws_ref[...])

    o_ref[...] = _dot(xn_ref[...], w_ref[...]).astype(BF16)


def _inproj(x, norm_w, w_main, w_small, tm=1024, tn=1024):
    t, d = x.shape
    n = w_main.shape[1]
    tm = min(tm, t)
    return pl.pallas_call(
        _inproj_kernel,
        grid=(t // tm, n // tn),
        in_specs=[
            pl.BlockSpec((tm, d), lambda i, j: (i, 0)),
            pl.BlockSpec((1, d), lambda i, j: (0, 0)),
            pl.BlockSpec((d, tn), lambda i, j: (0, j)),
            pl.BlockSpec((d, LANES), lambda i, j: (0, 0)),
        ],
        out_specs=[
            pl.BlockSpec((tm, tn), lambda i, j: (i, j)),
            pl.BlockSpec((tm, LANES), lambda i, j: (i, 0)),
        ],
        out_shape=[jax.ShapeDtypeStruct((t, n), BF16), jax.ShapeDtypeStruct((t, LANES), F32)],
        scratch_shapes=[pltpu.VMEM((tm, d), BF16)],
        compiler_params=_params(2),
        name="inproj",
    )(x, norm_w, w_main, w_small)


def _shift_mats(tc):
    return jnp.concatenate([jnp.eye(tc, k=-(CONV_K - 1 - k), dtype=BF16) for k in range(CONV_K - 1)], axis=0)


def _conv_block(raw_ref, carry_ref, w_ref, shift_ref, tc):
    x16 = raw_ref[...]
    xf = x16.astype(F32)
    y = xf * w_ref[CONV_K - 1:CONV_K, :]
    c8 = carry_ref[...]
    row8 = lax.broadcasted_iota(jnp.int32, c8.shape, 0)
    head = jnp.zeros_like(c8)
    shifted = _dot(shift_ref[...], x16)
    for k in range(CONV_K - 1):
        lag = CONV_K - 1 - k
        y = y + shifted[k * tc:(k + 1) * tc] * w_ref[k:k + 1, :]
        head = head + jnp.where(row8 < lag, pltpu.roll(c8, shift=lag, axis=0), 0.0) * w_ref[k:k + 1, :]
    carry_ref[...] = xf[tc - 8:tc]
    return jnp.concatenate([y[0:8] + head, y[8:]], axis=0)


def _ssd_kernel(xs_ref, b_ref, c_ref, z_ref, sm_ref, cwx_ref, cwb_ref, cwc_ref, cbx_ref, cbb_ref,
                cbc_ref, dtb_ref, alog_ref, dsk_ref, nw_ref, sh_ref, o_ref,
                xbuf, bbuf, cbuf, state_ref, acst_ref, *, tc, gps):
    g0 = pl.program_id(0) * gps
    L, GW, NS = SSM_CHUNK, SSM_GROUP_W, SSM_D_STATE

    @pl.when(pl.program_id(1) == 0)
    def _():
        xbuf[...] = jnp.zeros_like(xbuf)
        bbuf[...] = jnp.zeros_like(bbuf)
        cbuf[...] = jnp.zeros_like(cbuf)
        state_ref[...] = jnp.zeros_like(state_ref)

    xs_all = _silu(_conv_block(xs_ref, xbuf, cwx_ref, sh_ref, tc) + cbx_ref[...])
    b_all = _silu(_conv_block(b_ref, bbuf, cwb_ref, sh_ref, tc) + cbb_ref[...])
    c_all = _silu(_conv_block(c_ref, cbuf, cwc_ref, sh_ref, tc) + cbc_ref[...])

    row = lax.broadcasted_iota(jnp.int32, (L, L), 0)
    col = lax.broadcasted_iota(jnp.int32, (L, L), 1)
    causal = row >= col
    tri = causal.astype(F32)
    lo_half = col < SSM_HEAD_DIM
    a_row = -jnp.exp(alog_ref[...])

    for c in range(tc // L):
        sl = slice(c * L, (c + 1) * L)
        dt_full = _softplus(sm_ref[sl, :] + dtb_ref[...])
        acs_full = _dot(tri, dt_full * a_row, precision=HIGHEST)
        acst_ref[c] = acs_full.T
        for gg in range(gps):
            xs = xs_all[sl, gg * GW:(gg + 1) * GW]
            bm16 = b_all[sl, gg * NS:(gg + 1) * NS].astype(BF16)
            cm16 = c_all[sl, gg * NS:(gg + 1) * NS].astype(BF16)
            cb = _dot_nt(cm16, bm16)
            ys = []
            for p in range(SSM_HPG // 2):
                acs_bc, dt_bc, scores = [], [], []
                for r in (2 * p, 2 * p + 1):
                    j = (g0 + gg) * SSM_HPG + r
                    sel = col == j
                    a_c = jnp.sum(jnp.where(sel, acs_full, 0.0), axis=1, keepdims=True)
                    d_c = jnp.sum(jnp.where(sel, dt_full, 0.0), axis=1, keepdims=True)
                    a_b = jnp.broadcast_to(a_c, (L, L))
                    acs_bc.append(a_b)
                    dt_bc.append(jnp.broadcast_to(d_c, (L, L)))
                    a_r = acst_ref[c, pl.ds(j, 1), :]
                    seg = jnp.where(causal, a_b - a_r, -jnp.inf)
                    scores.append((cb * jnp.exp(seg)).astype(BF16))
                acs_e = jnp.where(lo_half, acs_bc[0], acs_bc[1])
                dt_e = jnp.where(lo_half, dt_bc[0], dt_bc[1])
                x_p = xs[:, p * LANES:(p + 1) * LANES]
                lanes = slice(gg * GW + p * LANES, gg * GW + (p + 1) * LANES)
                xdt = x_p * dt_e
                last = acs_e[L - 1:L, :]
                xdtw = (xdt * jnp.exp(last - acs_e)).astype(BF16)
                s2 = jnp.concatenate(scores, axis=1)
                x2 = jnp.concatenate([jnp.where(lo_half, xdt, 0.0), jnp.where(lo_half, 0.0, xdt)],
                                     axis=0).astype(BF16)
                y_diag = _dot(s2, x2)
                st = state_ref[:, lanes]
                y_off = _dot(cm16, st.astype(BF16)) * jnp.exp(acs_e)
                state_ref[:, lanes] = st * jnp.exp(last) + _dot_tn(bm16, xdtw)
                ys.append(y_diag + y_off + x_p * dsk_ref[:, lanes])
            gl = slice(gg * GW, (gg + 1) * GW)
            y = jnp.concatenate(ys, axis=1) * _silu(z_ref[sl, gl].astype(F32))
            y = y * lax.rsqrt(jnp.mean(y * y, axis=-1, keepdims=True) + EPS)
            o_ref[sl, gl] = (y * nw_ref[:, gl]).astype(BF16)


def _ssd(proj, small, cwx, cwb, cwc, cbx, cbb, cbc, dtb_row, alog_row, dsk_row, nw_row, tc=256, gps=4):
    t = proj.shape[0]
    tc = min(tc, t)
    gw, ns = gps * SSM_GROUP_W, gps * SSM_D_STATE
    xs_blk = COL_SSM_XBC // gw
    b_blk = (COL_SSM_XBC + SSM_D_INNER) // ns
    c_blk = b_blk + SSM_N_GROUPS // gps
    z_blk = COL_SSM_Z // gw
    kern = functools.partial(_ssd_kernel, tc=tc, gps=gps)
    return pl.pallas_call(
        kern,
        grid=(SSM_N_GROUPS // gps, t // tc),
        in_specs=[
            pl.BlockSpec((tc, gw), lambda g, i: (i, xs_blk + g)),
            pl.BlockSpec((tc, ns), lambda g, i: (i, b_blk + g)),
            pl.BlockSpec((tc, ns), lambda g, i: (i, c_blk + g)),
            pl.BlockSpec((tc, gw), lambda g, i: (i, z_blk + g)),
            pl.BlockSpec((tc, LANES), lambda g, i: (i, 0)),
            pl.BlockSpec((CONV_K, gw), lambda g, i: (0, g)),
            pl.BlockSpec((CONV_K, ns), lambda g, i: (0, g)),
            pl.BlockSpec((CONV_K, ns), lambda g, i: (0, g)),
            pl.BlockSpec((1, gw), lambda g, i: (0, g)),
            pl.BlockSpec((1, ns), lambda g, i: (0, g)),
            pl.BlockSpec((1, ns), lambda g, i: (0, g)),
            pl.BlockSpec((1, LANES), lambda g, i: (0, 0)),
            pl.BlockSpec((1, LANES), lambda g, i: (0, 0)),
            pl.BlockSpec((1, gw), lambda g, i: (0, g)),
            pl.BlockSpec((1, gw), lambda g, i: (0, g)),
            pl.BlockSpec(((CONV_K - 1) * tc, tc), lambda g, i: (0, 0)),
        ],
        out_specs=pl.BlockSpec((tc, gw), lambda g, i: (i, g)),
        out_shape=jax.ShapeDtypeStruct((t, SSM_D_INNER), BF16),
        scratch_shapes=[
            pltpu.VMEM((8, gw), F32),
            pltpu.VMEM((8, ns), F32),
            pltpu.VMEM((8, ns), F32),
            pltpu.VMEM((SSM_D_STATE, gw), F32),
            pltpu.VMEM((tc // SSM_CHUNK, LANES, SSM_CHUNK), F32),
        ],
        compiler_params=_params(2),
        name="ssd",
    )(proj, proj, proj, proj, small, cwx, cwb, cwc, cbx, cbb, cbc, dtb_row, alog_row, dsk_row, nw_row,
      _shift_mats(tc))


def _gdn_kernel(q_ref, k_ref, v_ref, z_ref, sm_ref, cwq_ref, cwk_ref, cwv_ref, dtb_ref, alog_ref,
                nw_ref, sh_ref, o_ref, qbuf, kbuf, vbuf, s_ref, *, tc, hps):
    h0 = pl.program_id(0) * hps
    C, DK, DV = GDN_CHUNK, GDN_HEAD_K, GDN_HEAD_V

    @pl.when(pl.program_id(1) == 0)
    def _():
        qbuf[...] = jnp.zeros_like(qbuf)
        kbuf[...] = jnp.zeros_like(kbuf)
        vbuf[...] = jnp.zeros_like(vbuf)
        s_ref[...] = jnp.zeros_like(s_ref)

    q_cv = _silu(_conv_block(q_ref, qbuf, cwq_ref, sh_ref, tc))
    k_cv = _silu(_conv_block(k_ref, kbuf, cwk_ref, sh_ref, tc))
    v_cv = _silu(_conv_block(v_ref, vbuf, cwv_ref, sh_ref, tc))

    sm = sm_ref[...]
    lane = lax.broadcasted_iota(jnp.int32, sm.shape, 1)
    beta_full = jax.nn.sigmoid(sm)
    g_full = -jnp.exp(alog_ref[...]) * _softplus(sm + dtb_ref[...])
    q_hd, k_hd, v_hd, beta_hd, g_hd = [], [], [], [], []
    for hd in range(hps):
        q = q_cv[:, hd * DK:(hd + 1) * DK]
        k = k_cv[:, hd * DK:(hd + 1) * DK]
        q_hd.append(q * lax.rsqrt(jnp.sum(q * q, axis=-1, keepdims=True) + EPS) * (DK ** -0.5))
        k_hd.append(k * lax.rsqrt(jnp.sum(k * k, axis=-1, keepdims=True) + EPS))
        v_hd.append(v_cv[:, hd * DV:(hd + 1) * DV])
        beta_hd.append(jnp.sum(jnp.where(lane == SMALL_BETA + h0 + hd, beta_full, 0.0),
                               axis=1, keepdims=True))
        g_hd.append(jnp.sum(jnp.where(lane == SMALL_ALPHA + h0 + hd, g_full, 0.0),
                            axis=1, keepdims=True))

    row = lax.broadcasted_iota(jnp.int32, (C, C), 0)
    col = lax.broadcasted_iota(jnp.int32, (C, C), 1)
    causal = row >= col
    strict = row > col
    tri = causal.astype(F32)
    eye = (row == col).astype(F32)

    nc = tc // C
    items = [(hd, slice(c * C, (c + 1) * C)) for hd in range(hps) for c in range(nc)]
    rng = range(len(items))
    strict_w = (lax.broadcasted_iota(jnp.int32, (C, LANES), 0)
                > lax.broadcasted_iota(jnp.int32, (C, LANES), 1))
    qs = [q_hd[hd][sl] for hd, sl in items]
    ks = [k_hd[hd][sl] for hd, sl in items]
    vs = [v_hd[hd][sl] for hd, sl in items]
    betas = [beta_hd[hd][sl] for hd, sl in items]
    g_bc = [jnp.broadcast_to(g_hd[hd][sl], (C, LANES)) for hd, sl in items]
    cums = [_dot(tri, jnp.concatenate([g_bc[c], jnp.where(strict_w, g_bc[c], 0.0)], axis=1),
                 precision=HIGHEST) for c in rng]
    gc_bc = [x[:, :LANES] for x in cums]
    decay = [jnp.exp(jnp.where(causal, x[:, LANES:LANES + C], -jnp.inf)) for x in cums]
    k16 = [k.astype(BF16) for k in ks]
    q16 = [q.astype(BF16) for q in qs]
    kk = [_dot_nt(k16[c], k16[c]) for c in rng]
    qk = [_dot_nt(q16[c], k16[c]) for c in rng]
    p = [-jnp.where(strict, kk[c] * betas[c] * decay[c], 0.0) for c in rng]
    x = [eye + p[c] for c in rng]
    for _ in range(C.bit_length() - 2):
        p16 = [pp.astype(BF16) for pp in p]
        p = [_dot(p16[c], p16[c]) for c in rng]
        x = [x[c] + _dot(x[c].astype(BF16), p[c].astype(BF16)) for c in rng]
    t_inv = [xx.astype(BF16) for xx in x]
    egc = [jnp.exp(gc) for gc in gc_bc]
    u = [_dot(t_inv[c], (vs[c] * betas[c]).astype(BF16)).astype(BF16) for c in rng]
    w = [_dot(t_inv[c], (ks[c] * (betas[c] * egc[c])).astype(BF16)).astype(BF16) for c in rng]
    attn = [jnp.where(causal, qk[c] * decay[c], 0.0).astype(BF16) for c in rng]
    gc_last = [gc[C - 1:C, :] for gc in gc_bc]
    k_dec = [(ks[c] * jnp.exp(gc_last[c] - gc_bc[c])).astype(BF16) for c in rng]
    c_dec = [jnp.exp(gl[:, :1]) for gl in gc_last]
    a_mat = [_dot_tn(k_dec[c], w[c]).astype(BF16) for c in rng]
    n_mat = [_dot_tn(k_dec[c], u[c]) for c in rng]
    q_eff = [(qs[c] * egc[c] - _dot(attn[c], w[c])).astype(BF16) for c in rng]
    o_u = [_dot(attn[c], u[c]) for c in rng]

    s = [s_ref[hd] for hd in range(hps)]
    for c in range(nc):
        sl = slice(c * C, (c + 1) * C)
        for hd in range(hps):
            it = hd * nc + c
            s16 = s[hd].astype(BF16)
            o = _dot(q_eff[it], s16) + o_u[it]
            s[hd] = s[hd] * c_dec[it] - _dot(a_mat[it], s16) + n_mat[it]
            o = o * lax.rsqrt(jnp.mean(o * o, axis=-1, keepdims=True) + EPS) * nw_ref[...]
            z = z_ref[sl, hd * DV:(hd + 1) * DV].astype(F32)
            o_ref[sl, hd * DV:(hd + 1) * DV] = (o * _silu(z)).astype(BF16)
    for hd in range(hps):
        s_ref[hd] = s[hd]


def _gdn(proj, small, cwq, cwk, cwv, dtb_row, alog_row, nw_row, tc=256, hps=4):
    t = proj.shape[0]
    tc = min(tc, t)
    dk, dv = hps * GDN_HEAD_K, hps * GDN_HEAD_V
    q_blk = COL_GDN_QKV // dk
    k_blk = q_blk + GDN_N_HEADS // hps
    v_blk = (COL_GDN_QKV + 2 * GDN_KEY_DIM) // dv
    z_blk = COL_GDN_Z // dv
    kern = functools.partial(_gdn_kernel, tc=tc, hps=hps)
    return pl.pallas_call(
        kern,
        grid=(GDN_N_HEADS // hps, t // tc),
        in_specs=[
            pl.BlockSpec((tc, dk), lambda h, i: (i, q_blk + h)),
            pl.BlockSpec((tc, dk), lambda h, i: (i, k_blk + h)),
            pl.BlockSpec((tc, dv), lambda h, i: (i, v_blk + h)),
            pl.BlockSpec((tc, dv), lambda h, i: (i, z_blk + h)),
            pl.BlockSpec((tc, LANES), lambda h, i: (i, 0)),
            pl.BlockSpec((CONV_K, dk), lambda h, i: (0, h)),
            pl.BlockSpec((CONV_K, dk), lambda h, i: (0, h)),
            pl.BlockSpec((CONV_K, dv), lambda h, i: (0, h)),
            pl.BlockSpec((1, LANES), lambda h, i: (0, 0)),
            pl.BlockSpec((1, LANES), lambda h, i: (0, 0)),
            pl.BlockSpec((1, GDN_HEAD_V), lambda h, i: (0, 0)),
            pl.BlockSpec(((CONV_K - 1) * tc, tc), lambda h, i: (0, 0)),
        ],
        out_specs=pl.BlockSpec((tc, dv), lambda h, i: (i, h)),
        out_shape=jax.ShapeDtypeStruct((t, GDN_VAL_DIM), BF16),
        scratch_shapes=[
            pltpu.VMEM((8, dk), F32),
            pltpu.VMEM((8, dk), F32),
            pltpu.VMEM((8, dv), F32),
            pltpu.VMEM((hps, GDN_HEAD_K, GDN_HEAD_V), F32),
        ],
        compiler_params=_params(2),
        name="gdn",
    )(proj, proj, proj, proj, small, cwq, cwk, cwv, dtb_row, alog_row, nw_row, _shift_mats(tc))


def _mix_kernel(ys_ref, yg_ref, g0_ref, g1_ref, gb_ref, ws_ref, wg_ref, o_ref):
    a = _dot(ys_ref[...], ws_ref[...])
    b = _dot(yg_ref[...], wg_ref[...])
    g0 = jax.nn.sigmoid(g0_ref[...].astype(F32) + gb_ref[0:1, :])
    g1 = jax.nn.sigmoid(g1_ref[...].astype(F32) + gb_ref[1:2, :])
    o_ref[...] = (g0 * a + g1 * b).astype(BF16)


def _mix(y_ssm, y_gdn, proj, gate_b, w_s, w_g, tm=512, tn=512):
    t = y_ssm.shape[0]
    tm = min(tm, t)
    g0_blk = COL_GATE // tn
    g1_blk = (COL_GATE + D_MODEL) // tn
    return pl.pallas_call(
        _mix_kernel,
        grid=(D_MODEL // tn, t // tm),
        in_specs=[
            pl.BlockSpec((tm, SSM_D_INNER), lambda n, i: (i, 0)),
            pl.BlockSpec((tm, GDN_VAL_DIM), lambda n, i: (i, 0)),
            pl.BlockSpec((tm, tn), lambda n, i: (i, g0_blk + n)),
            pl.BlockSpec((tm, tn), lambda n, i: (i, g1_blk + n)),
            pl.BlockSpec((2, tn), lambda n, i: (0, n)),
            pl.BlockSpec((SSM_D_INNER, tn), lambda n, i: (0, n)),
            pl.BlockSpec((GDN_VAL_DIM, tn), lambda n, i: (0, n)),
        ],
        out_specs=pl.BlockSpec((tm, tn), lambda n, i: (i, n)),
        out_shape=jax.ShapeDtypeStruct((t, D_MODEL), BF16),
        compiler_params=_params(2),
        name="mix",
    )(y_ssm, y_gdn, proj, proj, gate_b, w_s, w_g)


_N_RANK = PEER_TOPK + 1
_CAND_PAIRS = [(i, j) for i in range(_N_RANK) for j in range(_N_RANK) if (i + 1) * (j + 1) <= _N_RANK]
_CAND_ROWS = -(-len(_CAND_PAIRS) // 8) * 8


def _top_desc(s, n):
    out = []
    for _ in range(n):
        m = jnp.max(s, axis=0, keepdims=True)
        out.append(m)
        s = jnp.where(s == m, -jnp.inf, s)
    return out


def _post_kernel(x_ref, mix_ref, wo_ref, fw_ref, wq_ref, sk_ref,
                 h1_ref, hnt_ref, s2_ref, e2_ref, th_ref, e1_ref, cand_ref, *, tm):
    h1 = x_ref[...] + _dot(mix_ref[...], wo_ref[...])
    h1_ref[...] = h1
    hn = h1 * lax.rsqrt(jnp.mean(h1 * h1, axis=-1, keepdims=True) + EPS) * fw_ref[...]
    hnt_ref[...] = hn.T.astype(BF16)
    qv = _dot(hn.astype(BF16), wq_ref[...]).astype(BF16)
    nchunk = tm // LANES
    for h in range(PEER_HEADS):
        s1 = _dot_nt(sk_ref[2 * h], qv[:, (2 * h) * LANES:(2 * h + 1) * LANES])
        s2 = _dot_nt(sk_ref[2 * h + 1], qv[:, (2 * h + 1) * LANES:(2 * h + 2) * LANES])
        a1 = _top_desc(s1, _N_RANK)
        a2 = _top_desc(s2, _N_RANK)
        cand_ref[...] = jnp.full(cand_ref.shape, -jnp.inf, F32)
        for r, (i, j) in enumerate(_CAND_PAIRS):
            cand_ref[pl.ds(r, 1), :] = a1[i] + a2[j]
        cand = cand_ref[...]
        c = cand
        n_removed = jnp.zeros((1, tm), F32)
        v16 = jnp.full((1, tm), -jnp.inf, F32)
        v17 = jnp.full((1, tm), -jnp.inf, F32)
        for _ in range(_N_RANK):
            m = jnp.max(c, axis=0, keepdims=True)
            eq = c == m
            cnt = jnp.sum(eq.astype(F32), axis=0, keepdims=True)
            v16 = jnp.where(n_removed < PEER_TOPK, m, v16)
            v17 = jnp.where(n_removed < _N_RANK, m, v17)
            n_removed = n_removed + cnt
            c = jnp.where(eq, -jnp.inf, c)
        tau = 0.5 * (v16 + v17)
        tau = jnp.where(v17 == -jnp.inf, v16, tau)
        m_tot = a1[0] + a2[0]
        z = jnp.sum(jnp.where(cand >= tau, jnp.exp(cand - m_tot), 0.0), axis=0, keepdims=True)
        e2 = jnp.exp(s2 - a2[0]) / z
        e1 = jnp.exp(s1 - a1[0])
        th = tau - s1
        for cc in range(nchunk):
            ls = slice(cc * LANES, (cc + 1) * LANES)
            s2_ref[cc, h] = s2[:, ls]
            e2_ref[cc, h] = e2[:, ls]
            th_ref[cc, h] = th[:, ls]
            e1_ref[cc, h] = e1[:, ls]


def _post(x, mix, w_out, ffn_w, w_q, sub_keys, tm=256):
    t, d = x.shape
    tm = min(tm, t)
    nchunk = tm // LANES
    stat = jax.ShapeDtypeStruct((t // LANES, PEER_HEADS, PEER_N_KEYS, LANES), F32)
    stat_spec = pl.BlockSpec((nchunk, PEER_HEADS, PEER_N_KEYS, LANES), lambda i: (i, 0, 0, 0))
    kern = functools.partial(_post_kernel, tm=tm)
    return pl.pallas_call(
        kern,
        grid=(t // tm,),
        in_specs=[
            pl.BlockSpec((tm, d), lambda i: (i, 0)),
            pl.BlockSpec((tm, d), lambda i: (i, 0)),
            pl.BlockSpec((d, d), lambda i: (0, 0)),
            pl.BlockSpec((1, d), lambda i: (0, 0)),
            pl.BlockSpec((d, d), lambda i: (0, 0)),
            pl.BlockSpec((2 * PEER_HEADS, PEER_N_KEYS, LANES), lambda i: (0, 0, 0)),
        ],
        out_specs=[
            pl.BlockSpec((tm, d), lambda i: (i, 0)),
            pl.BlockSpec((d, tm), lambda i: (0, i)),
            stat_spec, stat_spec, stat_spec, stat_spec,
        ],
        out_shape=[
            jax.ShapeDtypeStruct((t, d), F32),
            jax.ShapeDtypeStruct((d, t), BF16),
            stat, stat, stat, stat,
        ],
        scratch_shapes=[pltpu.VMEM((_CAND_ROWS, tm), F32)],
        compiler_params=_params(1),
        name="post",
    )(x, mix, w_out, ffn_w, w_q, sub_keys)


_ROW_TILE = 32


def _gelu(x):
    return 0.5 * x * (1.0 + lax.erf(x * (2.0 ** -0.5)))


def _peer_kernel(hnt_ref, u0_ref, u1_ref, vt0_ref, vtp_ref, vtl_ref, s2_ref, e2_ref, th_ref, e1_ref, acc_ref,
                 at0_ref, at1_ref, pt0_ref, pt1_ref, *, tb, eb):
    n1 = eb // PEER_N_KEYS
    nchunk = tb // LANES
    nrt = PEER_N_KEYS // _ROW_TILE
    d = acc_ref.shape[0]

    @pl.when(pl.program_id(1) == 0)
    def _():
        acc_ref[...] = jnp.zeros_like(acc_ref)
        pt1_ref[...] = jnp.zeros_like(pt1_ref)

    hnt = hnt_ref[...]
    at0_ref[...] = _dot(u0_ref[...], hnt)
    at1_ref[...] = _dot(u1_ref[...], hnt)

    def tile(at_ref, pt_ref, row0, cc, rt):
        ls = slice(cc * LANES, (cc + 1) * LANES)
        rs = pl.ds(pl.multiple_of(rt * _ROW_TILE, _ROW_TILE), _ROW_TILE)
        w = [jnp.zeros((_ROW_TILE, LANES), F32) for _ in range(n1)]
        for h in range(PEER_HEADS):
            s2 = s2_ref[cc, h, rs, :]
            e2 = e2_ref[cc, h, rs, :]
            for i in range(n1):
                th = th_ref[cc, h, row0 + i:row0 + i + 1, :]
                e1 = e1_ref[cc, h, row0 + i:row0 + i + 1, :]
                w[i] = w[i] + jnp.where(s2 >= th, e2, 0.0) * e1
        for i in range(n1):
            r = pl.ds(pl.multiple_of(i * PEER_N_KEYS + rt * _ROW_TILE, _ROW_TILE), _ROW_TILE)
            pt_ref[r, ls] = (w[i] * _gelu(at_ref[r, ls])).astype(BF16)

    c_rows = d // nrt

    def body0(rt, carry):
        for cc in range(nchunk):
            tile(at0_ref, pt0_ref, 0, cc, rt)
        r = pl.ds(pl.multiple_of(rt * c_rows, c_rows), c_rows)
        acc_ref[r, :] += _dot(vtp_ref[r, :], pt1_ref[...])
        return carry
    lax.fori_loop(0, nrt, body0, 0)

    def body1(rt, carry):
        for cc in range(nchunk):
            tile(at1_ref, pt1_ref, n1, cc, rt)
        r = pl.ds(pl.multiple_of(rt * c_rows, c_rows), c_rows)
        acc_ref[r, :] += _dot(vt0_ref[r, :], pt0_ref[...])
        return carry
    lax.fori_loop(0, nrt, body1, 0)

    @pl.when(pl.program_id(1) == pl.num_programs(1) - 1)
    def _():
        acc_ref[...] += _dot(vtl_ref[...], pt1_ref[...])


def _peer(hnt, u16, v16, s2, e2, th, e1, tb=512, eb=512):
    d, t = hnt.shape
    tb = min(tb, t)
    nchunk = tb // LANES
    n1 = eb // PEER_N_KEYS
    nk = PEER_N_EXPERTS // (2 * eb)
    vt_slabs = jnp.swapaxes(v16.reshape(2 * nk, eb, d), 1, 2)
    assert 2 * n1 == 8, "a pair of expert blocks spans one 8-row group of first-key statistics"
    stat_spec = pl.BlockSpec((nchunk, PEER_HEADS, PEER_N_KEYS, LANES), lambda i, k: (i, 0, 0, 0))
    row_spec = pl.BlockSpec((nchunk, PEER_HEADS, 8, LANES), lambda i, k: (i, 0, k, 0))
    kern = functools.partial(_peer_kernel, tb=tb, eb=eb)
    return pl.pallas_call(
        kern,
        grid=(t // tb, nk),
        in_specs=[
            pl.BlockSpec((d, tb), lambda i, k: (0, i)),
            pl.BlockSpec((eb, d), lambda i, k: (2 * k, 0)),
            pl.BlockSpec((eb, d), lambda i, k: (2 * k + 1, 0)),
            pl.BlockSpec((None, d, eb), lambda i, k: (2 * k, 0, 0)),
            pl.BlockSpec((None, d, eb), lambda i, k: (2 * jnp.maximum(k - 1, 0) + 1, 0, 0)),
            pl.BlockSpec((None, d, eb), lambda i, k: (2 * nk - 1, 0, 0)),
            stat_spec, stat_spec, row_spec, row_spec,
        ],
        out_specs=pl.BlockSpec((d, tb), lambda i, k: (0, i)),
        out_shape=jax.ShapeDtypeStruct((d, t), F32),
        scratch_shapes=[pltpu.VMEM((eb, tb), F32), pltpu.VMEM((eb, tb), F32),
                        pltpu.VMEM((eb, tb), BF16), pltpu.VMEM((eb, tb), BF16)],
        compiler_params=_params(2),
        name="peer",
    )(hnt, u16, u16, vt_slabs, vt_slabs, vt_slabs, s2, e2, th, e1)


def _final_kernel(h_ref, pt_ref, w_ref, o_ref):
    hf = h_ref[...] + pt_ref[...].T
    o_ref[...] = hf * lax.rsqrt(jnp.mean(hf * hf, axis=-1, keepdims=True) + EPS) * w_ref[...]


def _final(h1, peer_out_t, w_row, tm=512):
    t, d = h1.shape
    tm = min(tm, t)
    spec = pl.BlockSpec((tm, d), lambda i: (i, 0))
    return pl.pallas_call(
        _final_kernel,
        grid=(t // tm,),
        in_specs=[spec, pl.BlockSpec((d, tm), lambda i: (0, i)), pl.BlockSpec((1, d), lambda i: (0, 0))],
        out_specs=spec,
        out_shape=jax.ShapeDtypeStruct((t, d), F32),
        compiler_params=_params(1),
        name="final",
    )(h1, peer_out_t, w_row)


def _row(v, width=None):
    v = v.astype(F32).reshape(1, -1)
    if width is not None and v.shape[1] < width:
        v = jnp.pad(v, ((0, 0), (0, width - v.shape[1])))
    return v


def _layer(h, mix_norm_w, w_in, gate_b, ssm_conv_w, ssm_conv_b, ssm_dt_bias, ssm_a_log, ssm_d,
           ssm_norm_w, gdn_conv_w, gdn_dt_bias, gdn_a_log, gdn_norm_w, w_branch_ssm, w_branch_gdn,
           w_out, ffn_norm_w, peer_w_q, peer_sub_keys, peer_u, peer_v, out_norm_w):
    ssm_proj = SSM_D_INNER + (SSM_D_INNER + 2 * SSM_N_GROUPS * SSM_D_STATE) + SSM_N_HEADS
    gdn_conv_dim = 2 * GDN_KEY_DIM + GDN_VAL_DIM
    o_dt = ssm_proj - SSM_N_HEADS
    o_qkv = ssm_proj
    o_gz = o_qkv + gdn_conv_dim
    o_beta = o_gz + GDN_VAL_DIM
    o_gate = o_beta + 2 * GDN_N_HEADS
    w_main = jnp.concatenate(
        [w_in[:, :o_dt], w_in[:, o_qkv:o_beta], w_in[:, o_gate:]], axis=1).astype(BF16)
    w_small = jnp.concatenate(
        [w_in[:, o_dt:o_qkv], w_in[:, o_beta:o_gate],
         jnp.zeros((D_MODEL, LANES - SSM_N_HEADS - 2 * GDN_N_HEADS), w_in.dtype)], axis=1).astype(BF16)

    proj, small = _inproj(h, _row(mix_norm_w), w_main, w_small)

    cw = ssm_conv_w.astype(F32)
    cbias = _row(ssm_conv_b)
    nx = SSM_D_INNER
    nb = SSM_N_GROUPS * SSM_D_STATE
    y_ssm = _ssd(proj, small,
                 cw[:, :nx], cw[:, nx:nx + nb], cw[:, nx + nb:],
                 cbias[:, :nx], cbias[:, nx:nx + nb], cbias[:, nx + nb:],
                 _row(ssm_dt_bias, LANES), _row(ssm_a_log, LANES),
                 _row(jnp.repeat(ssm_d, SSM_HEAD_DIM)), _row(ssm_norm_w))

    gw = gdn_conv_w.astype(F32)
    zeros_b = jnp.zeros((SMALL_ALPHA,), F32)
    y_gdn = _gdn(proj, small,
                 gw[:, :GDN_KEY_DIM], gw[:, GDN_KEY_DIM:2 * GDN_KEY_DIM], gw[:, 2 * GDN_KEY_DIM:],
                 _row(jnp.concatenate([zeros_b, gdn_dt_bias.astype(F32)]), LANES),
                 _row(jnp.concatenate([zeros_b, gdn_a_log.astype(F32)]), LANES),
                 _row(gdn_norm_w))

    mix = _mix(y_ssm, y_gdn, proj, gate_b.astype(F32),
               w_branch_ssm.astype(BF16), w_branch_gdn.astype(BF16))

    sk = peer_sub_keys.reshape(2 * PEER_HEADS, PEER_N_KEYS, LANES).astype(BF16)
    h1, hnt, s2, e2, th, e1 = _post(h, mix, w_out.astype(BF16), _row(ffn_norm_w),
                                    peer_w_q.astype(BF16), sk)

    peer_out_t = _peer(hnt, peer_u.astype(BF16), peer_v.astype(BF16), s2, e2, th, e1)
    return _final(h1, peer_out_t, _row(out_norm_w))


def kernel(x, mix_norm_w, w_in, gate_b, ssm_conv_w, ssm_conv_b, ssm_dt_bias, ssm_a_log, ssm_d,
           ssm_norm_w, gdn_conv_w, gdn_dt_bias, gdn_a_log, gdn_norm_w, w_branch_ssm, w_branch_gdn,
           w_out, ffn_norm_w, peer_w_q, peer_sub_keys, peer_u, peer_v, final_norm_w):
    b, t, d = x.shape
    assert b == 1 and mix_norm_w.shape[0] == 1, "single sequence, single layer"
    out = _layer(x[0], mix_norm_w[0], w_in[0], gate_b[0], ssm_conv_w[0], ssm_conv_b[0],
                 ssm_dt_bias[0], ssm_a_log[0], ssm_d[0], ssm_norm_w[0], gdn_conv_w[0],
                 gdn_dt_bias[0], gdn_a_log[0], gdn_norm_w[0], w_branch_ssm[0], w_branch_gdn[0],
                 w_out[0], ffn_norm_w[0], peer_w_q[0], peer_sub_keys[0], peer_u[0], peer_v[0],
                 final_norm_w)
    return out[None]
```

```python
import functools

import jax
import jax.numpy as jnp
from jax import lax
from jax.experimental import pallas as pl
from jax.experimental.pallas import tpu as pltpu

F32 = jnp.float32
BF16 = jnp.bfloat16
HIGHEST = lax.Precision.HIGHEST

EPS = 1e-6
D_MODEL = 2048
LANES = 128
SSM_D_INNER = 4096
SSM_HEAD_DIM = 64
SSM_N_HEADS = 64
SSM_N_GROUPS = 8
SSM_HPG = 8
SSM_D_STATE = 128
SSM_CHUNK = 128
SSM_GROUP_W = SSM_HPG * SSM_HEAD_DIM
GDN_N_HEADS = 16
GDN_HEAD_K = 128
GDN_HEAD_V = 256
GDN_CHUNK = 64
GDN_KEY_DIM = GDN_N_HEADS * GDN_HEAD_K
GDN_VAL_DIM = GDN_N_HEADS * GDN_HEAD_V
CONV_K = 4
PEER_HEADS = 8
PEER_N_KEYS = 128
PEER_TOPK = 16
PEER_N_EXPERTS = PEER_N_KEYS * PEER_N_KEYS

COL_SSM_Z = 0
COL_SSM_XBC = 4096
COL_GDN_QKV = 10240
COL_GDN_Z = 18432
COL_GATE = 22528
N_MAIN = 26624
SMALL_BETA = 64
SMALL_ALPHA = 80

VMEM_LIMIT = 56 * 1024 * 1024


def _softplus(x):
    return jnp.maximum(x, 0.0) + jnp.log1p(jnp.exp(-jnp.abs(x)))


def _silu(x):
    return x * jax.nn.sigmoid(x)


def _dot(a, b, **kw):
    return jnp.dot(a, b, preferred_element_type=F32, **kw)


def _dot_nt(a, b, **kw):
    return lax.dot_general(a, b, (((1,), (1,)), ((), ())), preferred_element_type=F32, **kw)


def _dot_tn(a, b, **kw):
    return lax.dot_general(a, b, (((0,), (0,)), ((), ())), preferred_element_type=F32, **kw)


def _params(n_grid):
    return pltpu.CompilerParams(dimension_semantics=("arbitrary",) * n_grid,
                                vmem_limit_bytes=VMEM_LIMIT)


def _inproj_kernel(x_ref, nw_ref, w_ref, ws_ref, o_ref, os_ref, xn_ref):
    @pl.when(pl.program_id(1) == 0)
    def _():
        x = x_ref[...]
        ms = jnp.mean(x * x, axis=-1, keepdims=True)
        xn = (x * lax.rsqrt(ms + EPS) * nw_ref[...]).astype(BF16)
        xn_ref[...] = xn
        os_ref[...] = _dot(xn, ws_ref[...])

    o_ref[...] = _dot(xn_ref[...], w_ref[...]).astype(BF16)


def _inproj(x, norm_w, w_main, w_small, tm=1024, tn=1024):
    t, d = x.shape
    n = w_main.shape[1]
    tm = min(tm, t)
    return pl.pallas_call(
        _inproj_kernel,
        grid=(t // tm, n // tn),
        in_specs=[
            pl.BlockSpec((tm, d), lambda i, j: (i, 0)),
            pl.BlockSpec((1, d), lambda i, j: (0, 0)),
            pl.BlockSpec((d, tn), lambda i, j: (0, j)),
            pl.BlockSpec((d, LANES), lambda i, j: (0, 0)),
        ],
        out_specs=[
            pl.BlockSpec((tm, tn), lambda i, j: (i, j)),
            pl.BlockSpec((tm, LANES), lambda i, j: (i, 0)),
        ],
        out_shape=[jax.ShapeDtypeStruct((t, n), BF16), jax.ShapeDtypeStruct((t, LANES), F32)],
        scratch_shapes=[pltpu.VMEM((tm, d), BF16)],
        compiler_params=_params(2),
        name="inproj",
    )(x, norm_w, w_main, w_small)


def _shift_mats(tc):
    return jnp.concatenate([jnp.eye(tc, k=-(CONV_K - 1 - k), dtype=BF16) for k in range(CONV_K - 1)], axis=0)


def _conv_block(raw_ref, carry_ref, w_ref, shift_ref, tc):
    x16 = raw_ref[...]
    xf = x16.astype(F32)
    y = xf * w_ref[CONV_K - 1:CONV_K, :]
    c8 = carry_ref[...]
    row8 = lax.broadcasted_iota(jnp.int32, c8.shape, 0)
    head = jnp.zeros_like(c8)
    shifted = _dot(shift_ref[...], x16)
    for k in range(CONV_K - 1):
        lag = CONV_K - 1 - k
        y = y + shifted[k * tc:(k + 1) * tc] * w_ref[k:k + 1, :]
        head = head + jnp.where(row8 < lag, pltpu.roll(c8, shift=lag, axis=0), 0.0) * w_ref[k:k + 1, :]
    carry_ref[...] = xf[tc - 8:tc]
    return jnp.concatenate([y[0:8] + head, y[8:]], axis=0)


def _ssd_kernel(xs_ref, b_ref, c_ref, z_ref, sm_ref, cwx_ref, cwb_ref, cwc_ref, cbx_ref, cbb_ref,
                cbc_ref, dtb_ref, alog_ref, dsk_ref, nw_ref, sh_ref, o_ref,
                xbuf, bbuf, cbuf, state_ref, acst_ref, *, tc, gps):
    g0 = pl.program_id(0) * gps
    L, GW, NS = SSM_CHUNK, SSM_GROUP_W, SSM_D_STATE

    @pl.when(pl.program_id(1) == 0)
    def _():
        xbuf[...] = jnp.zeros_like(xbuf)
        bbuf[...] = jnp.zeros_like(bbuf)
        cbuf[...] = jnp.zeros_like(cbuf)
        state_ref[...] = jnp.zeros_like(state_ref)

    xs_all = _silu(_conv_block(xs_ref, xbuf, cwx_ref, sh_ref, tc) + cbx_ref[...])
    b_all = _silu(_conv_block(b_ref, bbuf, cwb_ref, sh_ref, tc) + cbb_ref[...])
    c_all = _silu(_conv_block(c_ref, cbuf, cwc_ref, sh_ref, tc) + cbc_ref[...])

    row = lax.broadcasted_iota(jnp.int32, (L, L), 0)
    col = lax.broadcasted_iota(jnp.int32, (L, L), 1)
    causal = row >= col
    tri = causal.astype(F32)
    lo_half = col < SSM_HEAD_DIM
    a_row = -jnp.exp(alog_ref[...])

    for c in range(tc // L):
        sl = slice(c * L, (c + 1) * L)
        dt_full = _softplus(sm_ref[sl, :] + dtb_ref[...])
        acs_full = _dot(tri, dt_full * a_row, precision=HIGHEST)
        acst_ref[c] = acs_full.T
        for gg in range(gps):
            xs = xs_all[sl, gg * GW:(gg + 1) * GW]
            bm16 = b_all[sl, gg * NS:(gg + 1) * NS].astype(BF16)
            cm16 = c_all[sl, gg * NS:(gg + 1) * NS].astype(BF16)
            cb = _dot_nt(cm16, bm16)
            ys = []
            for p in range(SSM_HPG // 2):
                acs_bc, dt_bc, scores = [], [], []
                for r in (2 * p, 2 * p + 1):
                    j = (g0 + gg) * SSM_HPG + r
                    sel = col == j
                    a_c = jnp.sum(jnp.where(sel, acs_full, 0.0), axis=1, keepdims=True)
                    d_c = jnp.sum(jnp.where(sel, dt_full, 0.0), axis=1, keepdims=True)
                    a_b = jnp.broadcast_to(a_c, (L, L))
                    acs_bc.append(a_b)
                    dt_bc.append(jnp.broadcast_to(d_c, (L, L)))
                    a_r = acst_ref[c, pl.ds(j, 1), :]
                    seg = jnp.where(causal, a_b - a_r, -jnp.inf)
                    scores.append((cb * jnp.exp(seg)).astype(BF16))
                acs_e = jnp.where(lo_half, acs_bc[0], acs_bc[1])
                dt_e = jnp.where(lo_half, dt_bc[0], dt_bc[1])
                x_p = xs[:, p * LANES:(p + 1) * LANES]
                lanes = slice(gg * GW + p * LANES, gg * GW + (p + 1) * LANES)
                xdt = x_p * dt_e
                last = acs_e[L - 1:L, :]
                xdtw = (xdt * jnp.exp(last - acs_e)).astype(BF16)
                s2 = jnp.concatenate(scores, axis=1)
                x2 = jnp.concatenate([jnp.where(lo_half, xdt, 0.0), jnp.where(lo_half, 0.0, xdt)],
                                     axis=0).astype(BF16)
                y_diag = _dot(s2, x2)
                st = state_ref[:, lanes]
                y_off = _dot(cm16, st.astype(BF16)) * jnp.exp(acs_e)
                state_ref[:, lanes] = st * jnp.exp(last) + _dot_tn(bm16, xdtw)
                ys.append(y_diag + y_off + x_p * dsk_ref[:, lanes])
            gl = slice(gg * GW, (gg + 1) * GW)
            y = jnp.concatenate(ys, axis=1) * _silu(z_ref[sl, gl].astype(F32))
            y = y * lax.rsqrt(jnp.mean(y * y, axis=-1, keepdims=True) + EPS)
            o_ref[sl, gl] = (y * nw_ref[:, gl]).astype(BF16)


def _ssd(proj, small, cwx, cwb, cwc, cbx, cbb, cbc, dtb_row, alog_row, dsk_row, nw_row, tc=256, gps=4):
    t = proj.shape[0]
    tc = min(tc, t)
    gw, ns = gps * SSM_GROUP_W, gps * SSM_D_STATE
    xs_blk = COL_SSM_XBC // gw
    b_blk = (COL_SSM_XBC + SSM_D_INNER) // ns
    c_blk = b_blk + SSM_N_GROUPS // gps
    z_blk = COL_SSM_Z // gw
    kern = functools.partial(_ssd_kernel, tc=tc, gps=gps)
    return pl.pallas_call(
        kern,
        grid=(SSM_N_GROUPS // gps, t // tc),
        in_specs=[
            pl.BlockSpec((tc, gw), lambda g, i: (i, xs_blk + g)),
            pl.BlockSpec((tc, ns), lambda g, i: (i, b_blk + g)),
            pl.BlockSpec((tc, ns), lambda g, i: (i, c_blk + g)),
            pl.BlockSpec((tc, gw), lambda g, i: (i, z_blk + g)),
            pl.BlockSpec((tc, LANES), lambda g, i: (i, 0)),
            pl.BlockSpec((CONV_K, gw), lambda g, i: (0, g)),
            pl.BlockSpec((CONV_K, ns), lambda g, i: (0, g)),
            pl.BlockSpec((CONV_K, ns), lambda g, i: (0, g)),
            pl.BlockSpec((1, gw), lambda g, i: (0, g)),
            pl.BlockSpec((1, ns), lambda g, i: (0, g)),
            pl.BlockSpec((1, ns), lambda g, i: (0, g)),
            pl.BlockSpec((1, LANES), lambda g, i: (0, 0)),
            pl.BlockSpec((1, LANES), lambda g, i: (0, 0)),
            pl.BlockSpec((1, gw), lambda g, i: (0, g)),
            pl.BlockSpec((1, gw), lambda g, i: (0, g)),
            pl.BlockSpec(((CONV_K - 1) * tc, tc), lambda g, i: (0, 0)),
        ],
        out_specs=pl.BlockSpec((tc, gw), lambda g, i: (i, g)),
        out_shape=jax.ShapeDtypeStruct((t, SSM_D_INNER), BF16),
        scratch_shapes=[
            pltpu.VMEM((8, gw), F32),
            pltpu.VMEM((8, ns), F32),
            pltpu.VMEM((8, ns), F32),
            pltpu.VMEM((SSM_D_STATE, gw), F32),
            pltpu.VMEM((tc // SSM_CHUNK, LANES, SSM_CHUNK), F32),
        ],
        compiler_params=_params(2),
        name="ssd",
    )(proj, proj, proj, proj, small, cwx, cwb, cwc, cbx, cbb, cbc, dtb_row, alog_row, dsk_row, nw_row,
      _shift_mats(tc))


def _gdn_kernel(q_ref, k_ref, v_ref, z_ref, sm_ref, cwq_ref, cwk_ref, cwv_ref, dtb_ref, alog_ref,
                nw_ref, sh_ref, o_ref, qbuf, kbuf, vbuf, s_ref, *, tc, hps):
    h0 = pl.program_id(0) * hps
    C, DK, DV = GDN_CHUNK, GDN_HEAD_K, GDN_HEAD_V

    @pl.when(pl.program_id(1) == 0)
    def _():
        qbuf[...] = jnp.zeros_like(qbuf)
        kbuf[...] = jnp.zeros_like(kbuf)
        vbuf[...] = jnp.zeros_like(vbuf)
        s_ref[...] = jnp.zeros_like(s_ref)

    q_cv = _silu(_conv_block(q_ref, qbuf, cwq_ref, sh_ref, tc))
    k_cv = _silu(_conv_block(k_ref, kbuf, cwk_ref, sh_ref, tc))
    v_cv = _silu(_conv_block(v_ref, vbuf, cwv_ref, sh_ref, tc))

    sm = sm_ref[...]
    lane = lax.broadcasted_iota(jnp.int32, sm.shape, 1)
    beta_full = jax.nn.sigmoid(sm)
    g_full = -jnp.exp(alog_ref[...]) * _softplus(sm + dtb_ref[...])
    q_hd, k_hd, v_hd, beta_hd, g_hd = [], [], [], [], []
    for hd in range(hps):
        q = q_cv[:, hd * DK:(hd + 1) * DK]
        k = k_cv[:, hd * DK:(hd + 1) * DK]
        q_hd.append(q * lax.rsqrt(jnp.sum(q * q, axis=-1, keepdims=True) + EPS) * (DK ** -0.5))
        k_hd.append(k * lax.rsqrt(jnp.sum(k * k, axis=-1, keepdims=True) + EPS))
        v_hd.append(v_cv[:, hd * DV:(hd + 1) * DV])
        beta_hd.append(jnp.sum(jnp.where(lane == SMALL_BETA + h0 + hd, beta_full, 0.0),
                               axis=1, keepdims=True))
        g_hd.append(jnp.sum(jnp.where(lane == SMALL_ALPHA + h0 + hd, g_full, 0.0),
                            axis=1, keepdims=True))

    row = lax.broadcasted_iota(jnp.int32, (C, C), 0)
    col = lax.broadcasted_iota(jnp.int32, (C, C), 1)
    causal = row >= col
    strict = row > col
    tri = causal.astype(F32)
    eye = (row == col).astype(F32)

    nc = tc // C
    items = [(hd, slice(c * C, (c + 1) * C)) for hd in range(hps) for c in range(nc)]
    rng = range(len(items))
    strict_w = (lax.broadcasted_iota(jnp.int32, (C, LANES), 0)
                > lax.broadcasted_iota(jnp.int32, (C, LANES), 1))
    qs = [q_hd[hd][sl] for hd, sl in items]
    ks = [k_hd[hd][sl] for hd, sl in items]
    vs = [v_hd[hd][sl] for hd, sl in items]
    betas = [beta_hd[hd][sl] for hd, sl in items]
    g_bc = [jnp.broadcast_to(g_hd[hd][sl], (C, LANES)) for hd, sl in items]
    cums = [_dot(tri, jnp.concatenate([g_bc[c], jnp.where(strict_w, g_bc[c], 0.0)], axis=1),
                 precision=HIGHEST) for c in rng]
    gc_bc = [x[:, :LANES] for x in cums]
    decay = [jnp.exp(jnp.where(causal, x[:, LANES:LANES + C], -jnp.inf)) for x in cums]
    k16 = [k.astype(BF16) for k in ks]
    q16 = [q.astype(BF16) for q in qs]
    kk = [_dot_nt(k16[c], k16[c]) for c in rng]
    qk = [_dot_nt(q16[c], k16[c]) for c in rng]
    p = [-jnp.where(strict, kk[c] * betas[c] * decay[c], 0.0) for c in rng]
    x = [eye + p[c] for c in rng]
    for _ in range(C.bit_length() - 2):
        p16 = [pp.astype(BF16) for pp in p]
        p = [_dot(p16[c], p16[c]) for c in rng]
        x = [x[c] + _dot(x[c].astype(BF16), p[c].astype(BF16)) for c in rng]
    t_inv = [xx.astype(BF16) for xx in x]
    egc = [jnp.exp(gc) for gc in gc_bc]
    u = [_dot(t_inv[c], (vs[c] * betas[c]).astype(BF16)).astype(BF16) for c in rng]
    w = [_dot(t_inv[c], (ks[c] * (betas[c] * egc[c])).astype(BF16)).astype(BF16) for c in rng]
    attn = [jnp.where(causal, qk[c] * decay[c], 0.0).astype(BF16) for c in rng]
    gc_last = [gc[C - 1:C, :] for gc in gc_bc]
    k_dec = [(ks[c] * jnp.exp(gc_last[c] - gc_bc[c])).astype(BF16) for c in rng]
    c_dec = [jnp.exp(gl[:, :1]) for gl in gc_last]
    a_mat = [_dot_tn(k_dec[c], w[c]).astype(BF16) for c in rng]
    n_mat = [_dot_tn(k_dec[c], u[c]) for c in rng]
    q_eff = [(qs[c] * egc[c] - _dot(attn[c], w[c])).astype(BF16) for c in rng]
    o_u = [_dot(attn[c], u[c]) for c in rng]

    s = [s_ref[hd] for hd in range(hps)]
    for c in range(nc):
        sl = slice(c * C, (c + 1) * C)
        for hd in range(hps):
            it = hd * nc + c
            s16 = s[hd].astype(BF16)
            o = _dot(q_eff[it], s16) + o_u[it]
            s[hd] = s[hd] * c_dec[it] - _dot(a_mat[it], s16) + n_mat[it]
            o = o * lax.rsqrt(jnp.mean(o * o, axis=-1, keepdims=True) + EPS) * nw_ref[...]
            z = z_ref[sl, hd * DV:(hd + 1) * DV].astype(F32)
            o_ref[sl, hd * DV:(hd + 1) * DV] = (o * _silu(z)).astype(BF16)
    for hd in range(hps):
        s_ref[hd] = s[hd]


def _gdn(proj, small, cwq, cwk, cwv, dtb_row, alog_row, nw_row, tc=256, hps=4):
    t = proj.shape[0]
    tc = min(tc, t)
    dk, dv = hps * GDN_HEAD_K, hps * GDN_HEAD_V
    q_blk = COL_GDN_QKV // dk
    k_blk = q_blk + GDN_N_HEADS // hps
    v_blk = (COL_GDN_QKV + 2 * GDN_KEY_DIM) // dv
    z_blk = COL_GDN_Z // dv
    kern = functools.partial(_gdn_kernel, tc=tc, hps=hps)
    return pl.pallas_call(
        kern,
        grid=(GDN_N_HEADS // hps, t // tc),
        in_specs=[
            pl.BlockSpec((tc, dk), lambda h, i: (i, q_blk + h)),
            pl.BlockSpec((tc, dk), lambda h, i: (i, k_blk + h)),
            pl.BlockSpec((tc, dv), lambda h, i: (i, v_blk + h)),
            pl.BlockSpec((tc, dv), lambda h, i: (i, z_blk + h)),
            pl.BlockSpec((tc, LANES), lambda h, i: (i, 0)),
            pl.BlockSpec((CONV_K, dk), lambda h, i: (0, h)),
            pl.BlockSpec((CONV_K, dk), lambda h, i: (0, h)),
            pl.BlockSpec((CONV_K, dv), lambda h, i: (0, h)),
            pl.BlockSpec((1, LANES), lambda h, i: (0, 0)),
            pl.BlockSpec((1, LANES), lambda h, i: (0, 0)),
            pl.BlockSpec((1, GDN_HEAD_V), lambda h, i: (0, 0)),
            pl.BlockSpec(((CONV_K - 1) * tc, tc), lambda h, i: (0, 0)),
        ],
        out_specs=pl.BlockSpec((tc, dv), lambda h, i: (i, h)),
        out_shape=jax.ShapeDtypeStruct((t, GDN_VAL_DIM), BF16),
        scratch_shapes=[
            pltpu.VMEM((8, dk), F32),
            pltpu.VMEM((8, dk), F32),
            pltpu.VMEM((8, dv), F32),
            pltpu.VMEM((hps, GDN_HEAD_K, GDN_HEAD_V), F32),
        ],
        compiler_params=_params(2),
        name="gdn",
    )(proj, proj, proj, proj, small, cwq, cwk, cwv, dtb_row, alog_row, nw_row, _shift_mats(tc))


def _mix_kernel(ys_ref, yg_ref, g0_ref, g1_ref, gb_ref, ws_ref, wg_ref, o_ref):
    a = _dot(ys_ref[...], ws_ref[...])
    b = _dot(yg_ref[...], wg_ref[...])
    g0 = jax.nn.sigmoid(g0_ref[...].astype(F32) + gb_ref[0:1, :])
    g1 = jax.nn.sigmoid(g1_ref[...].astype(F32) + gb_ref[1:2, :])
    o_ref[...] = (g0 * a + g1 * b).astype(BF16)


def _mix(y_ssm, y_gdn, proj, gate_b, w_s, w_g, tm=512, tn=512):
    t = y_ssm.shape[0]
    tm = min(tm, t)
    g0_blk = COL_GATE // tn
    g1_blk = (COL_GATE + D_MODEL) // tn
    return pl.pallas_call(
        _mix_kernel,
        grid=(D_MODEL // tn, t // tm),
        in_specs=[
            pl.BlockSpec((tm, SSM_D_INNER), lambda n, i: (i, 0)),
            pl.BlockSpec((tm, GDN_VAL_DIM), lambda n, i: (i, 0)),
            pl.BlockSpec((tm, tn), lambda n, i: (i, g0_blk + n)),
            pl.BlockSpec((tm, tn), lambda n, i: (i, g1_blk + n)),
            pl.BlockSpec((2, tn), lambda n, i: (0, n)),
            pl.BlockSpec((SSM_D_INNER, tn), lambda n, i: (0, n)),
            pl.BlockSpec((GDN_VAL_DIM, tn), lambda n, i: (0, n)),
        ],
        out_specs=pl.BlockSpec((tm, tn), lambda n, i: (i, n)),
        out_shape=jax.ShapeDtypeStruct((t, D_MODEL), BF16),
        compiler_params=_params(2),
        name="mix",
    )(y_ssm, y_gdn, proj, proj, gate_b, w_s, w_g)


_N_RANK = PEER_TOPK + 1
_CAND_PAIRS = [(i, j) for i in range(_N_RANK) for j in range(_N_RANK) if (i + 1) * (j + 1) <= _N_RANK]
_CAND_ROWS = -(-len(_CAND_PAIRS) // 8) * 8


def _top_desc(s, n):
    out = []
    for _ in range(n):
        m = jnp.max(s, axis=0, keepdims=True)
        out.append(m)
        s = jnp.where(s == m, -jnp.inf, s)
    return out


def _post_kernel(x_ref, mix_ref, wo_ref, fw_ref, wq_ref, sk_ref,
                 h1_ref, hnt_ref, s2_ref, e2_ref, th_ref, e1_ref, cand_ref, *, tm):
    h1 = x_ref[...] + _dot(mix_ref[...], wo_ref[...])
    h1_ref[...] = h1
    hn = h1 * lax.rsqrt(jnp.mean(h1 * h1, axis=-1, keepdims=True) + EPS) * fw_ref[...]
    hnt_ref[...] = hn.T.astype(BF16)
    qv = _dot(hn.astype(BF16), wq_ref[...]).astype(BF16)
    nchunk = tm // LANES
    for h in range(PEER_HEADS):
        s1 = _dot_nt(sk_ref[2 * h], qv[:, (2 * h) * LANES:(2 * h + 1) * LANES])
        s2 = _dot_nt(sk_ref[2 * h + 1], qv[:, (2 * h + 1) * LANES:(2 * h + 2) * LANES])
        a1 = _top_desc(s1, _N_RANK)
        a2 = _top_desc(s2, _N_RANK)
        cand_ref[...] = jnp.full(cand_ref.shape, -jnp.inf, F32)
        for r, (i, j) in enumerate(_CAND_PAIRS):
            cand_ref[pl.ds(r, 1), :] = a1[i] + a2[j]
        cand = cand_ref[...]
        c = cand
        n_removed = jnp.zeros((1, tm), F32)
        v16 = jnp.full((1, tm), -jnp.inf, F32)
        v17 = jnp.full((1, tm), -jnp.inf, F32)
        for _ in range(_N_RANK):
            m = jnp.max(c, axis=0, keepdims=True)
            eq = c == m
            cnt = jnp.sum(eq.astype(F32), axis=0, keepdims=True)
            v16 = jnp.where(n_removed < PEER_TOPK, m, v16)
            v17 = jnp.where(n_removed < _N_RANK, m, v17)
            n_removed = n_removed + cnt
            c = jnp.where(eq, -jnp.inf, c)
        tau = 0.5 * (v16 + v17)
        tau = jnp.where(v17 == -jnp.inf, v16, tau)
        m_tot = a1[0] + a2[0]
        z = jnp.sum(jnp.where(cand >= tau, jnp.exp(cand - m_tot), 0.0), axis=0, keepdims=True)
        e2 = jnp.exp(s2 - a2[0]) / z
        e1 = jnp.exp(s1 - a1[0])
        th = tau - s1
        for cc in range(nchunk):
            ls = slice(cc * LANES, (cc + 1) * LANES)
            s2_ref[cc, h] = s2[:, ls]
            e2_ref[cc, h] = e2[:, ls]
            th_ref[cc, h] = th[:, ls]
            e1_ref[cc, h] = e1[:, ls]


def _post(x, mix, w_out, ffn_w, w_q, sub_keys, tm=256):
    t, d = x.shape
    tm = min(tm, t)
    nchunk = tm // LANES
    stat = jax.ShapeDtypeStruct((t // LANES, PEER_HEADS, PEER_N_KEYS, LANES), F32)
    stat_spec = pl.BlockSpec((nchunk, PEER_HEADS, PEER_N_KEYS, LANES), lambda i: (i, 0, 0, 0))
    kern = functools.partial(_post_kernel, tm=tm)
    return pl.pallas_call(
        kern,
        grid=(t // tm,),
        in_specs=[
            pl.BlockSpec((tm, d), lambda i: (i, 0)),
            pl.BlockSpec((tm, d), lambda i: (i, 0)),
            pl.BlockSpec((d, d), lambda i: (0, 0)),
            pl.BlockSpec((1, d), lambda i: (0, 0)),
            pl.BlockSpec((d, d), lambda i: (0, 0)),
            pl.BlockSpec((2 * PEER_HEADS, PEER_N_KEYS, LANES), lambda i: (0, 0, 0)),
        ],
        out_specs=[
            pl.BlockSpec((tm, d), lambda i: (i, 0)),
            pl.BlockSpec((d, tm), lambda i: (0, i)),
            stat_spec, stat_spec, stat_spec, stat_spec,
        ],
        out_shape=[
            jax.ShapeDtypeStruct((t, d), F32),
            jax.ShapeDtypeStruct((d, t), BF16),
            stat, stat, stat, stat,
        ],
        scratch_shapes=[pltpu.VMEM((_CAND_ROWS, tm), F32)],
        compiler_params=_params(1),
        name="post",
    )(x, mix, w_out, ffn_w, w_q, sub_keys)


_ROW_TILE = 32


def _gelu(x):
    return 0.5 * x * (1.0 + lax.erf(x * (2.0 ** -0.5)))


def _peer_kernel(hnt_ref, u0_ref, u1_ref, vt0_ref, vtp_ref, vtl_ref, s2_ref, e2_ref, th_ref, e1_ref, acc_ref,
                 at0_ref, at1_ref, pt0_ref, pt1_ref, *, tb, eb):
    n1 = eb // PEER_N_KEYS
    nchunk = tb // LANES
    nrt = PEER_N_KEYS // _ROW_TILE
    d = acc_ref.shape[0]

    @pl.when(pl.program_id(1) == 0)
    def _():
        acc_ref[...] = jnp.zeros_like(acc_ref)
        pt1_ref[...] = jnp.zeros_like(pt1_ref)

    hnt = hnt_ref[...]
    at0_ref[...] = _dot(u0_ref[...], hnt)
    at1_ref[...] = _dot(u1_ref[...], hnt)

    def tile(at_ref, pt_ref, row0, cc, rt):
        ls = slice(cc * LANES, (cc + 1) * LANES)
        rs = pl.ds(pl.multiple_of(rt * _ROW_TILE, _ROW_TILE), _ROW_TILE)
        w = [jnp.zeros((_ROW_TILE, LANES), F32) for _ in range(n1)]
        for h in range(PEER_HEADS):
            s2 = s2_ref[cc, h, rs, :]
            e2 = e2_ref[cc, h, rs, :]
            for i in range(n1):
                th = th_ref[cc, h, row0 + i:row0 + i + 1, :]
                e1 = e1_ref[cc, h, row0 + i:row0 + i + 1, :]
                w[i] = w[i] + jnp.where(s2 >= th, e2, 0.0) * e1
        for i in range(n1):
            r = pl.ds(pl.multiple_of(i * PEER_N_KEYS + rt * _ROW_TILE, _ROW_TILE), _ROW_TILE)
            pt_ref[r, ls] = (w[i] * _gelu(at_ref[r, ls])).astype(BF16)

    c_rows = d // nrt

    def body0(rt, carry):
        for cc in range(nchunk):
            tile(at0_ref, pt0_ref, 0, cc, rt)
        r = pl.ds(pl.multiple_of(rt * c_rows, c_rows), c_rows)
        acc_ref[r, :] += _dot(vtp_ref[r, :], pt1_ref[...])
        return carry
    lax.fori_loop(0, nrt, body0, 0)

    def body1(rt, carry):
        for cc in range(nchunk):
            tile(at1_ref, pt1_ref, n1, cc, rt)
        r = pl.ds(pl.multiple_of(rt * c_rows, c_rows), c_rows)
        acc_ref[r, :] += _dot(vt0_ref[r, :], pt0_ref[...])
        return carry
    lax.fori_loop(0, nrt, body1, 0)

    @pl.when(pl.program_id(1) == pl.num_programs(1) - 1)
    def _():
        acc_ref[...] += _dot(vtl_ref[...], pt1_ref[...])


def _vt_kernel(v_ref, o_ref):
    o_ref[...] = v_ref[...].T.astype(BF16)


def _transpose_slabs(v, eb):
    e, d = v.shape
    return pl.pallas_call(
        _vt_kernel,
        grid=(e // eb,),
        in_specs=[pl.BlockSpec((eb, d), lambda j: (j, 0))],
        out_specs=pl.BlockSpec((None, d, eb), lambda j: (j, 0, 0)),
        out_shape=jax.ShapeDtypeStruct((e // eb, d, eb), BF16),
        compiler_params=_params(1),
        name="vt",
    )(v)


def _peer(hnt, u16, v, s2, e2, th, e1, tb=512, eb=512):
    d, t = hnt.shape
    tb = min(tb, t)
    nchunk = tb // LANES
    n1 = eb // PEER_N_KEYS
    nk = PEER_N_EXPERTS // (2 * eb)
    vt_slabs = _transpose_slabs(v, eb)
    assert 2 * n1 == 8, "a pair of expert blocks spans one 8-row group of first-key statistics"
    stat_spec = pl.BlockSpec((nchunk, PEER_HEADS, PEER_N_KEYS, LANES), lambda i, k: (i, 0, 0, 0))
    row_spec = pl.BlockSpec((nchunk, PEER_HEADS, 8, LANES), lambda i, k: (i, 0, k, 0))
    kern = functools.partial(_peer_kernel, tb=tb, eb=eb)
    return pl.pallas_call(
        kern,
        grid=(t // tb, nk),
        in_specs=[
            pl.BlockSpec((d, tb), lambda i, k: (0, i)),
            pl.BlockSpec((eb, d), lambda i, k: (2 * k, 0)),
            pl.BlockSpec((eb, d), lambda i, k: (2 * k + 1, 0)),
            pl.BlockSpec((None, d, eb), lambda i, k: (2 * k, 0, 0)),
            pl.BlockSpec((None, d, eb), lambda i, k: (2 * jnp.maximum(k - 1, 0) + 1, 0, 0)),
            pl.BlockSpec((None, d, eb), lambda i, k: (2 * nk - 1, 0, 0)),
            stat_spec, stat_spec, row_spec, row_spec,
        ],
        out_specs=pl.BlockSpec((d, tb), lambda i, k: (0, i)),
        out_shape=jax.ShapeDtypeStruct((d, t), F32),
        scratch_shapes=[pltpu.VMEM((eb, tb), F32), pltpu.VMEM((eb, tb), F32),
                        pltpu.VMEM((eb, tb), BF16), pltpu.VMEM((eb, tb), BF16)],
        compiler_params=_params(2),
        name="peer",
    )(hnt, u16, u16, vt_slabs, vt_slabs, vt_slabs, s2, e2, th, e1)


def _final_kernel(h_ref, pt_ref, w_ref, o_ref):
    hf = h_ref[...] + pt_ref[...].T
    o_ref[...] = hf * lax.rsqrt(jnp.mean(hf * hf, axis=-1, keepdims=True) + EPS) * w_ref[...]


def _final(h1, peer_out_t, w_row, tm=512):
    t, d = h1.shape
    tm = min(tm, t)
    spec = pl.BlockSpec((tm, d), lambda i: (i, 0))
    return pl.pallas_call(
        _final_kernel,
        grid=(t // tm,),
        in_specs=[spec, pl.BlockSpec((d, tm), lambda i: (0, i)), pl.BlockSpec((1, d), lambda i: (0, 0))],
        out_specs=spec,
        out_shape=jax.ShapeDtypeStruct((t, d), F32),
        compiler_params=_params(1),
        name="final",
    )(h1, peer_out_t, w_row)


def _row(v, width=None):
    v = v.astype(F32).reshape(1, -1)
    if width is not None and v.shape[1] < width:
        v = jnp.pad(v, ((0, 0), (0, width - v.shape[1])))
    return v


def _layer(h, mix_norm_w, w_in, gate_b, ssm_conv_w, ssm_conv_b, ssm_dt_bias, ssm_a_log, ssm_d,
           ssm_norm_w, gdn_conv_w, gdn_dt_bias, gdn_a_log, gdn_norm_w, w_branch_ssm, w_branch_gdn,
           w_out, ffn_norm_w, peer_w_q, peer_sub_keys, peer_u, peer_v, out_norm_w):
    ssm_proj = SSM_D_INNER + (SSM_D_INNER + 2 * SSM_N_GROUPS * SSM_D_STATE) + SSM_N_HEADS
    gdn_conv_dim = 2 * GDN_KEY_DIM + GDN_VAL_DIM
    o_dt = ssm_proj - SSM_N_HEADS
    o_qkv = ssm_proj
    o_gz = o_qkv + gdn_conv_dim
    o_beta = o_gz + GDN_VAL_DIM
    o_gate = o_beta + 2 * GDN_N_HEADS
    w_main = jnp.concatenate(
        [w_in[:, :o_dt], w_in[:, o_qkv:o_beta], w_in[:, o_gate:]], axis=1).astype(BF16)
    w_small = jnp.concatenate(
        [w_in[:, o_dt:o_qkv], w_in[:, o_beta:o_gate],
         jnp.zeros((D_MODEL, LANES - SSM_N_HEADS - 2 * GDN_N_HEADS), w_in.dtype)], axis=1).astype(BF16)

    proj, small = _inproj(h, _row(mix_norm_w), w_main, w_small)

    cw = ssm_conv_w.astype(F32)
    cbias = _row(ssm_conv_b)
    nx = SSM_D_INNER
    nb = SSM_N_GROUPS * SSM_D_STATE
    y_ssm = _ssd(proj, small,
                 cw[:, :nx], cw[:, nx:nx + nb], cw[:, nx + nb:],
                 cbias[:, :nx], cbias[:, nx:nx + nb], cbias[:, nx + nb:],
                 _row(ssm_dt_bias, LANES), _row(ssm_a_log, LANES),
                 _row(jnp.repeat(ssm_d, SSM_HEAD_DIM)), _row(ssm_norm_w))

    gw = gdn_conv_w.astype(F32)
    zeros_b = jnp.zeros((SMALL_ALPHA,), F32)
    y_gdn = _gdn(proj, small,
                 gw[:, :GDN_KEY_DIM], gw[:, GDN_KEY_DIM:2 * GDN_KEY_DIM], gw[:, 2 * GDN_KEY_DIM:],
                 _row(jnp.concatenate([zeros_b, gdn_dt_bias.astype(F32)]), LANES),
                 _row(jnp.concatenate([zeros_b, gdn_a_log.astype(F32)]), LANES),
                 _row(gdn_norm_w))

    mix = _mix(y_ssm, y_gdn, proj, gate_b.astype(F32),
               w_branch_ssm.astype(BF16), w_branch_gdn.astype(BF16))

    sk = peer_sub_keys.reshape(2 * PEER_HEADS, PEER_N_KEYS, LANES).astype(BF16)
    h1, hnt, s2, e2, th, e1 = _post(h, mix, w_out.astype(BF16), _row(ffn_norm_w),
                                    peer_w_q.astype(BF16), sk)

    peer_out_t = _peer(hnt, peer_u.astype(BF16), peer_v, s2, e2, th, e1)
    return _final(h1, peer_out_t, _row(out_norm_w))


def kernel(x, mix_norm_w, w_in, gate_b, ssm_conv_w, ssm_conv_b, ssm_dt_bias, ssm_a_log, ssm_d,
           ssm_norm_w, gdn_conv_w, gdn_dt_bias, gdn_a_log, gdn_norm_w, w_branch_ssm, w_branch_gdn,
           w_out, ffn_norm_w, peer_w_q, peer_sub_keys, peer_u, peer_v, final_norm_w):
    b, t, d = x.shape
    assert b == 1 and mix_norm_w.shape[0] == 1, "single sequence, single layer"
    out = _layer(x[0], mix_norm_w[0], w_in[0], gate_b[0], ssm_conv_w[0], ssm_conv_b[0],
                 ssm_dt_bias[0], ssm_a_log[0], ssm_d[0], ssm_norm_w[0], gdn_conv_w[0],
                 gdn_dt_bias[0], gdn_a_log[0], gdn_norm_w[0], w_branch_ssm[0], w_branch_gdn[0],
                 w_out[0], ffn_norm_w[0], peer_w_q[0], peer_sub_keys[0], peer_u[0], peer_v[0],
                 final_norm_w)
    return out[None]
```

```python
import functools

import jax
import jax.numpy as jnp
from jax import lax
from jax.experimental import pallas as pl
from jax.experimental.pallas import tpu as pltpu

F32 = jnp.float32
BF16 = jnp.bfloat16
HIGHEST = lax.Precision.HIGHEST

EPS = 1e-6
D_MODEL = 2048
LANES = 128
SUBLANES = 8
SSM_D_INNER = 4096
SSM_HEAD_DIM = 64
SSM_N_HEADS = 64
SSM_N_GROUPS = 8
SSM_HPG = 8
SSM_D_STATE = 128
SSM_CHUNK = 128
SSM_GROUP_W = SSM_HPG * SSM_HEAD_DIM
GDN_N_HEADS = 16
GDN_HEAD_K = 128
GDN_HEAD_V = 256
GDN_CHUNK = 64
GDN_KEY_DIM = GDN_N_HEADS * GDN_HEAD_K
GDN_VAL_DIM = GDN_N_HEADS * GDN_HEAD_V
CONV_K = 4
PEER_HEADS = 8
PEER_N_KEYS = 128
PEER_TOPK = 16
PEER_N_EXPERTS = PEER_N_KEYS * PEER_N_KEYS

COL_SSM_Z = 0
COL_SSM_XBC = 4096
COL_GDN_QKV = 10240
COL_GDN_Z = 18432
COL_GATE = 22528
N_MAIN = 26624
SMALL_BETA = 64
SMALL_ALPHA = 80

VMEM_LIMIT = 56 * 1024 * 1024


def _softplus(x):
    return jnp.maximum(x, 0.0) + jnp.log1p(jnp.exp(-jnp.abs(x)))


def _silu(x):
    return x * jax.nn.sigmoid(x)


def _dot(a, b, **kw):
    return jnp.dot(a, b, preferred_element_type=F32, **kw)


def _dot_nt(a, b, **kw):
    return lax.dot_general(a, b, (((1,), (1,)), ((), ())), preferred_element_type=F32, **kw)


def _dot_tn(a, b, **kw):
    return lax.dot_general(a, b, (((0,), (0,)), ((), ())), preferred_element_type=F32, **kw)


def _params(n_grid):
    return pltpu.CompilerParams(dimension_semantics=("arbitrary",) * n_grid,
                                vmem_limit_bytes=VMEM_LIMIT)


def _inproj_kernel(x_ref, nw_ref, w_ref, ws_ref, o_ref, os_ref, xn_ref):
    @pl.when(pl.program_id(1) == 0)
    def _():
        x = x_ref[...]
        ms = jnp.mean(x * x, axis=-1, keepdims=True)
        xn = (x * lax.rsqrt(ms + EPS) * nw_ref[...]).astype(BF16)
        xn_ref[...] = xn
        os_ref[...] = _dot(xn, ws_ref[...])

    o_ref[...] = _dot(xn_ref[...], w_ref[...]).astype(BF16)


def _inproj(x, norm_w, w_main, w_small, tm=1024, tn=1024):
    t, d = x.shape
    n = w_main.shape[1]
    tm = min(tm, t)
    return pl.pallas_call(
        _inproj_kernel,
        grid=(t // tm, n // tn),
        in_specs=[
            pl.BlockSpec((tm, d), lambda i, j: (i, 0)),
            pl.BlockSpec((1, d), lambda i, j: (0, 0)),
            pl.BlockSpec((d, tn), lambda i, j: (0, j)),
            pl.BlockSpec((d, LANES), lambda i, j: (0, 0)),
        ],
        out_specs=[
            pl.BlockSpec((tm, tn), lambda i, j: (i, j)),
            pl.BlockSpec((tm, LANES), lambda i, j: (i, 0)),
        ],
        out_shape=[jax.ShapeDtypeStruct((t, n), BF16), jax.ShapeDtypeStruct((t, LANES), F32)],
        scratch_shapes=[pltpu.VMEM((tm, d), BF16)],
        compiler_params=_params(2),
        name="inproj",
    )(x, norm_w, w_main, w_small)


def _shift_mats(tc):
    return jnp.concatenate([jnp.eye(tc, k=-(CONV_K - 1 - k), dtype=BF16) for k in range(CONV_K - 1)], axis=0)


def _conv_block(raw_ref, carry_ref, w_ref, shift_ref, tc):
    x16 = raw_ref[...]
    xf = x16.astype(F32)
    y = xf * w_ref[CONV_K - 1:CONV_K, :]
    c8 = carry_ref[...]
    row8 = lax.broadcasted_iota(jnp.int32, c8.shape, 0)
    head = jnp.zeros_like(c8)
    shifted = _dot(shift_ref[...], x16)
    for k in range(CONV_K - 1):
        lag = CONV_K - 1 - k
        y = y + shifted[k * tc:(k + 1) * tc] * w_ref[k:k + 1, :]
        head = head + jnp.where(row8 < lag, pltpu.roll(c8, shift=lag, axis=0), 0.0) * w_ref[k:k + 1, :]
    carry_ref[...] = xf[tc - SUBLANES:tc]
    return jnp.concatenate([y[0:SUBLANES] + head, y[SUBLANES:]], axis=0)


def _ssd_kernel(xs_ref, b_ref, c_ref, z_ref, sm_ref, cwx_ref, cwb_ref, cwc_ref, cbx_ref, cbb_ref,
                cbc_ref, dtb_ref, alog_ref, dsk_ref, nw_ref, sh_ref, o_ref,
                xbuf, bbuf, cbuf, state_ref, acst_ref, *, tc, gps):
    g0 = pl.program_id(0) * gps
    L, GW, NS = SSM_CHUNK, SSM_GROUP_W, SSM_D_STATE

    @pl.when(pl.program_id(1) == 0)
    def _():
        xbuf[...] = jnp.zeros_like(xbuf)
        bbuf[...] = jnp.zeros_like(bbuf)
        cbuf[...] = jnp.zeros_like(cbuf)
        state_ref[...] = jnp.zeros_like(state_ref)

    xs_all = _silu(_conv_block(xs_ref, xbuf, cwx_ref, sh_ref, tc) + cbx_ref[...])
    b_all = _silu(_conv_block(b_ref, bbuf, cwb_ref, sh_ref, tc) + cbb_ref[...])
    c_all = _silu(_conv_block(c_ref, cbuf, cwc_ref, sh_ref, tc) + cbc_ref[...])

    row = lax.broadcasted_iota(jnp.int32, (L, L), 0)
    col = lax.broadcasted_iota(jnp.int32, (L, L), 1)
    causal = row >= col
    tri = causal.astype(F32)
    lo_half = col < SSM_HEAD_DIM
    a_row = -jnp.exp(alog_ref[...])

    for c in range(tc // L):
        sl = slice(c * L, (c + 1) * L)
        dt_full = _softplus(sm_ref[sl, :] + dtb_ref[...])
        acs_full = _dot(tri, dt_full * a_row, precision=HIGHEST)
        acst_ref[c] = acs_full.T
        for gg in range(gps):
            xs = xs_all[sl, gg * GW:(gg + 1) * GW]
            bm16 = b_all[sl, gg * NS:(gg + 1) * NS].astype(BF16)
            cm16 = c_all[sl, gg * NS:(gg + 1) * NS].astype(BF16)
            cb = _dot_nt(cm16, bm16)
            ys = []
            for p in range(SSM_HPG // 2):
                acs_bc, dt_bc, scores = [], [], []
                for r in (2 * p, 2 * p + 1):
                    j = (g0 + gg) * SSM_HPG + r
                    sel = col == j
                    a_c = jnp.sum(jnp.where(sel, acs_full, 0.0), axis=1, keepdims=True)
                    d_c = jnp.sum(jnp.where(sel, dt_full, 0.0), axis=1, keepdims=True)
                    a_b = jnp.broadcast_to(a_c, (L, L))
                    acs_bc.append(a_b)
                    dt_bc.append(jnp.broadcast_to(d_c, (L, L)))
                    a_r = acst_ref[c, pl.ds(j, 1), :]
                    seg = jnp.where(causal, a_b - a_r, -jnp.inf)
                    scores.append((cb * jnp.exp(seg)).astype(BF16))
                acs_e = jnp.where(lo_half, acs_bc[0], acs_bc[1])
                dt_e = jnp.where(lo_half, dt_bc[0], dt_bc[1])
                x_p = xs[:, p * LANES:(p + 1) * LANES]
                lanes = slice(gg * GW + p * LANES, gg * GW + (p + 1) * LANES)
                xdt = x_p * dt_e
                last = acs_e[L - 1:L, :]
                xdtw = (xdt * jnp.exp(last - acs_e)).astype(BF16)
                s2 = jnp.concatenate(scores, axis=1)
                x2 = jnp.concatenate([jnp.where(lo_half, xdt, 0.0), jnp.where(lo_half, 0.0, xdt)],
                                     axis=0).astype(BF16)
                y_diag = _dot(s2, x2)
                st = state_ref[:, lanes]
                y_off = _dot(cm16, st.astype(BF16)) * jnp.exp(acs_e)
                state_ref[:, lanes] = st * jnp.exp(last) + _dot_tn(bm16, xdtw)
                ys.append(y_diag + y_off + x_p * dsk_ref[:, lanes])
            gl = slice(gg * GW, (gg + 1) * GW)
            y = jnp.concatenate(ys, axis=1) * _silu(z_ref[sl, gl].astype(F32))
            y = y * lax.rsqrt(jnp.mean(y * y, axis=-1, keepdims=True) + EPS)
            o_ref[sl, gl] = (y * nw_ref[:, gl]).astype(BF16)


def _ssd(proj, small, cwx, cwb, cwc, cbx, cbb, cbc, dtb_row, alog_row, dsk_row, nw_row, tc=256, gps=4):
    t = proj.shape[0]
    tc = min(tc, t)
    gw, ns = gps * SSM_GROUP_W, gps * SSM_D_STATE
    xs_blk = COL_SSM_XBC // gw
    b_blk = (COL_SSM_XBC + SSM_D_INNER) // ns
    c_blk = b_blk + SSM_N_GROUPS // gps
    z_blk = COL_SSM_Z // gw
    kern = functools.partial(_ssd_kernel, tc=tc, gps=gps)
    return pl.pallas_call(
        kern,
        grid=(SSM_N_GROUPS // gps, t // tc),
        in_specs=[
            pl.BlockSpec((tc, gw), lambda g, i: (i, xs_blk + g)),
            pl.BlockSpec((tc, ns), lambda g, i: (i, b_blk + g)),
            pl.BlockSpec((tc, ns), lambda g, i: (i, c_blk + g)),
            pl.BlockSpec((tc, gw), lambda g, i: (i, z_blk + g)),
            pl.BlockSpec((tc, LANES), lambda g, i: (i, 0)),
            pl.BlockSpec((CONV_K, gw), lambda g, i: (0, g)),
            pl.BlockSpec((CONV_K, ns), lambda g, i: (0, g)),
            pl.BlockSpec((CONV_K, ns), lambda g, i: (0, g)),
            pl.BlockSpec((1, gw), lambda g, i: (0, g)),
            pl.BlockSpec((1, ns), lambda g, i: (0, g)),
            pl.BlockSpec((1, ns), lambda g, i: (0, g)),
            pl.BlockSpec((1, LANES), lambda g, i: (0, 0)),
            pl.BlockSpec((1, LANES), lambda g, i: (0, 0)),
            pl.BlockSpec((1, gw), lambda g, i: (0, g)),
            pl.BlockSpec((1, gw), lambda g, i: (0, g)),
            pl.BlockSpec(((CONV_K - 1) * tc, tc), lambda g, i: (0, 0)),
        ],
        out_specs=pl.BlockSpec((tc, gw), lambda g, i: (i, g)),
        out_shape=jax.ShapeDtypeStruct((t, SSM_D_INNER), BF16),
        scratch_shapes=[
            pltpu.VMEM((SUBLANES, gw), F32),
            pltpu.VMEM((SUBLANES, ns), F32),
            pltpu.VMEM((SUBLANES, ns), F32),
            pltpu.VMEM((SSM_D_STATE, gw), F32),
            pltpu.VMEM((tc // SSM_CHUNK, LANES, SSM_CHUNK), F32),
        ],
        compiler_params=_params(2),
        name="ssd",
    )(proj, proj, proj, proj, small, cwx, cwb, cwc, cbx, cbb, cbc, dtb_row, alog_row, dsk_row, nw_row,
      _shift_mats(tc))


def _gdn_kernel(q_ref, k_ref, v_ref, z_ref, sm_ref, cwq_ref, cwk_ref, cwv_ref, dtb_ref, alog_ref,
                nw_ref, sh_ref, o_ref, qbuf, kbuf, vbuf, s_ref, *, tc, hps):
    h0 = pl.program_id(0) * hps
    C, DK, DV = GDN_CHUNK, GDN_HEAD_K, GDN_HEAD_V

    @pl.when(pl.program_id(1) == 0)
    def _():
        qbuf[...] = jnp.zeros_like(qbuf)
        kbuf[...] = jnp.zeros_like(kbuf)
        vbuf[...] = jnp.zeros_like(vbuf)
        s_ref[...] = jnp.zeros_like(s_ref)

    q_cv = _silu(_conv_block(q_ref, qbuf, cwq_ref, sh_ref, tc))
    k_cv = _silu(_conv_block(k_ref, kbuf, cwk_ref, sh_ref, tc))
    v_cv = _silu(_conv_block(v_ref, vbuf, cwv_ref, sh_ref, tc))

    sm = sm_ref[...]
    lane = lax.broadcasted_iota(jnp.int32, sm.shape, 1)
    beta_full = jax.nn.sigmoid(sm)
    g_full = -jnp.exp(alog_ref[...]) * _softplus(sm + dtb_ref[...])
    q_hd, k_hd, v_hd, beta_hd, g_hd = [], [], [], [], []
    for hd in range(hps):
        q = q_cv[:, hd * DK:(hd + 1) * DK]
        k = k_cv[:, hd * DK:(hd + 1) * DK]
        q_hd.append(q * lax.rsqrt(jnp.sum(q * q, axis=-1, keepdims=True) + EPS) * (DK ** -0.5))
        k_hd.append(k * lax.rsqrt(jnp.sum(k * k, axis=-1, keepdims=True) + EPS))
        v_hd.append(v_cv[:, hd * DV:(hd + 1) * DV])
        beta_hd.append(jnp.sum(jnp.where(lane == SMALL_BETA + h0 + hd, beta_full, 0.0),
                               axis=1, keepdims=True))
        g_hd.append(jnp.sum(jnp.where(lane == SMALL_ALPHA + h0 + hd, g_full, 0.0),
                            axis=1, keepdims=True))

    row = lax.broadcasted_iota(jnp.int32, (C, C), 0)
    col = lax.broadcasted_iota(jnp.int32, (C, C), 1)
    causal = row >= col
    strict = row > col
    tri = causal.astype(F32)
    eye = (row == col).astype(F32)

    nc = tc // C
    items = [(hd, slice(c * C, (c + 1) * C)) for hd in range(hps) for c in range(nc)]
    rng = range(len(items))
    strict_w = (lax.broadcasted_iota(jnp.int32, (C, LANES), 0)
                > lax.broadcasted_iota(jnp.int32, (C, LANES), 1))
    qs = [q_hd[hd][sl] for hd, sl in items]
    ks = [k_hd[hd][sl] for hd, sl in items]
    vs = [v_hd[hd][sl] for hd, sl in items]
    betas = [beta_hd[hd][sl] for hd, sl in items]
    g_bc = [jnp.broadcast_to(g_hd[hd][sl], (C, LANES)) for hd, sl in items]
    cums = [_dot(tri, jnp.concatenate([g_bc[c], jnp.where(strict_w, g_bc[c], 0.0)], axis=1),
                 precision=HIGHEST) for c in rng]
    gc_bc = [x[:, :LANES] for x in cums]
    decay = [jnp.exp(jnp.where(causal, x[:, LANES:LANES + C], -jnp.inf)) for x in cums]
    k16 = [k.astype(BF16) for k in ks]
    q16 = [q.astype(BF16) for q in qs]
    kk = [_dot_nt(k16[c], k16[c]) for c in rng]
    qk = [_dot_nt(q16[c], k16[c]) for c in rng]
    p = [-jnp.where(strict, kk[c] * betas[c] * decay[c], 0.0) for c in rng]
    x = [eye + p[c] for c in rng]
    for _ in range(C.bit_length() - 2):
        p16 = [pp.astype(BF16) for pp in p]
        p = [_dot(p16[c], p16[c]) for c in rng]
        x = [x[c] + _dot(x[c].astype(BF16), p[c].astype(BF16)) for c in rng]
    t_inv = [xx.astype(BF16) for xx in x]
    egc = [jnp.exp(gc) for gc in gc_bc]
    u = [_dot(t_inv[c], (vs[c] * betas[c]).astype(BF16)).astype(BF16) for c in rng]
    w = [_dot(t_inv[c], (ks[c] * (betas[c] * egc[c])).astype(BF16)).astype(BF16) for c in rng]
    attn = [jnp.where(causal, qk[c] * decay[c], 0.0).astype(BF16) for c in rng]
    gc_last = [gc[C - 1:C, :] for gc in gc_bc]
    k_dec = [(ks[c] * jnp.exp(gc_last[c] - gc_bc[c])).astype(BF16) for c in rng]
    c_dec = [jnp.exp(gl[:, :1]) for gl in gc_last]
    a_mat = [_dot_tn(k_dec[c], w[c]).astype(BF16) for c in rng]
    n_mat = [_dot_tn(k_dec[c], u[c]) for c in rng]
    q_eff = [(qs[c] * egc[c] - _dot(attn[c], w[c])).astype(BF16) for c in rng]
    o_u = [_dot(attn[c], u[c]) for c in rng]

    s = [s_ref[hd] for hd in range(hps)]
    for c in range(nc):
        sl = slice(c * C, (c + 1) * C)
        for hd in range(hps):
            it = hd * nc + c
            s16 = s[hd].astype(BF16)
            o = _dot(q_eff[it], s16) + o_u[it]
            s[hd] = s[hd] * c_dec[it] - _dot(a_mat[it], s16) + n_mat[it]
            o = o * lax.rsqrt(jnp.mean(o * o, axis=-1, keepdims=True) + EPS) * nw_ref[...]
            z = z_ref[sl, hd * DV:(hd + 1) * DV].astype(F32)
            o_ref[sl, hd * DV:(hd + 1) * DV] = (o * _silu(z)).astype(BF16)
    for hd in range(hps):
        s_ref[hd] = s[hd]


def _gdn(proj, small, cwq, cwk, cwv, dtb_row, alog_row, nw_row, tc=256, hps=8):
    t = proj.shape[0]
    tc = min(tc, t)
    dk, dv = hps * GDN_HEAD_K, hps * GDN_HEAD_V
    q_blk = COL_GDN_QKV // dk
    k_blk = q_blk + GDN_N_HEADS // hps
    v_blk = (COL_GDN_QKV + 2 * GDN_KEY_DIM) // dv
    z_blk = COL_GDN_Z // dv
    kern = functools.partial(_gdn_kernel, tc=tc, hps=hps)
    return pl.pallas_call(
        kern,
        grid=(GDN_N_HEADS // hps, t // tc),
        in_specs=[
            pl.BlockSpec((tc, dk), lambda h, i: (i, q_blk + h)),
            pl.BlockSpec((tc, dk), lambda h, i: (i, k_blk + h)),
            pl.BlockSpec((tc, dv), lambda h, i: (i, v_blk + h)),
            pl.BlockSpec((tc, dv), lambda h, i: (i, z_blk + h)),
            pl.BlockSpec((tc, LANES), lambda h, i: (i, 0)),
            pl.BlockSpec((CONV_K, dk), lambda h, i: (0, h)),
            pl.BlockSpec((CONV_K, dk), lambda h, i: (0, h)),
            pl.BlockSpec((CONV_K, dv), lambda h, i: (0, h)),
            pl.BlockSpec((1, LANES), lambda h, i: (0, 0)),
            pl.BlockSpec((1, LANES), lambda h, i: (0, 0)),
            pl.BlockSpec((1, GDN_HEAD_V), lambda h, i: (0, 0)),
            pl.BlockSpec(((CONV_K - 1) * tc, tc), lambda h, i: (0, 0)),
        ],
        out_specs=pl.BlockSpec((tc, dv), lambda h, i: (i, h)),
        out_shape=jax.ShapeDtypeStruct((t, GDN_VAL_DIM), BF16),
        scratch_shapes=[
            pltpu.VMEM((SUBLANES, dk), F32),
            pltpu.VMEM((SUBLANES, dk), F32),
            pltpu.VMEM((SUBLANES, dv), F32),
            pltpu.VMEM((hps, GDN_HEAD_K, GDN_HEAD_V), F32),
        ],
        compiler_params=_params(2),
        name="gdn",
    )(proj, proj, proj, proj, small, cwq, cwk, cwv, dtb_row, alog_row, nw_row, _shift_mats(tc))


def _mix_kernel(ys_ref, yg_ref, g0_ref, g1_ref, gb_ref, ws_ref, wg_ref, o_ref):
    a = _dot(ys_ref[...], ws_ref[...])
    b = _dot(yg_ref[...], wg_ref[...])
    g0 = jax.nn.sigmoid(g0_ref[...].astype(F32) + gb_ref[0:1, :])
    g1 = jax.nn.sigmoid(g1_ref[...].astype(F32) + gb_ref[1:2, :])
    o_ref[...] = (g0 * a + g1 * b).astype(BF16)


def _mix(y_ssm, y_gdn, proj, gate_b, w_s, w_g, tm=512, tn=512):
    t = y_ssm.shape[0]
    tm = min(tm, t)
    g0_blk = COL_GATE // tn
    g1_blk = (COL_GATE + D_MODEL) // tn
    return pl.pallas_call(
        _mix_kernel,
        grid=(D_MODEL // tn, t // tm),
        in_specs=[
            pl.BlockSpec((tm, SSM_D_INNER), lambda n, i: (i, 0)),
            pl.BlockSpec((tm, GDN_VAL_DIM), lambda n, i: (i, 0)),
            pl.BlockSpec((tm, tn), lambda n, i: (i, g0_blk + n)),
            pl.BlockSpec((tm, tn), lambda n, i: (i, g1_blk + n)),
            pl.BlockSpec((2, tn), lambda n, i: (0, n)),
            pl.BlockSpec((SSM_D_INNER, tn), lambda n, i: (0, n)),
            pl.BlockSpec((GDN_VAL_DIM, tn), lambda n, i: (0, n)),
        ],
        out_specs=pl.BlockSpec((tm, tn), lambda n, i: (i, n)),
        out_shape=jax.ShapeDtypeStruct((t, D_MODEL), BF16),
        compiler_params=_params(2),
        name="mix",
    )(y_ssm, y_gdn, proj, proj, gate_b, w_s, w_g)


_N_RANK = PEER_TOPK + 1
_CAND_PAIRS = [(i, j) for i in range(_N_RANK) for j in range(_N_RANK) if (i + 1) * (j + 1) <= _N_RANK]
_CAND_ROWS = -(-len(_CAND_PAIRS) // SUBLANES) * SUBLANES


def _top_desc(s, n):
    out = []
    for _ in range(n):
        m = jnp.max(s, axis=0, keepdims=True)
        out.append(m)
        s = jnp.where(s == m, -jnp.inf, s)
    return out


def _post_kernel(x_ref, mix_ref, wo_ref, fw_ref, wq_ref, sk_ref,
                 h1_ref, hnt_ref, s2_ref, e2_ref, th_ref, e1_ref, cand_ref, *, tm):
    h1 = x_ref[...] + _dot(mix_ref[...], wo_ref[...])
    h1_ref[...] = h1
    hn = h1 * lax.rsqrt(jnp.mean(h1 * h1, axis=-1, keepdims=True) + EPS) * fw_ref[...]
    hnt_ref[...] = hn.T.astype(BF16)
    qv = _dot(hn.astype(BF16), wq_ref[...]).astype(BF16)
    nchunk = tm // LANES
    for h in range(PEER_HEADS):
        s1 = _dot_nt(sk_ref[2 * h], qv[:, (2 * h) * LANES:(2 * h + 1) * LANES])
        s2 = _dot_nt(sk_ref[2 * h + 1], qv[:, (2 * h + 1) * LANES:(2 * h + 2) * LANES])
        a1 = _top_desc(s1, _N_RANK)
        a2 = _top_desc(s2, _N_RANK)
        cand_ref[...] = jnp.full(cand_ref.shape, -jnp.inf, F32)
        for r, (i, j) in enumerate(_CAND_PAIRS):
            cand_ref[pl.ds(r, 1), :] = a1[i] + a2[j]
        cand = cand_ref[...]
        c = cand
        n_removed = jnp.zeros((1, tm), F32)
        v16 = jnp.full((1, tm), -jnp.inf, F32)
        v17 = jnp.full((1, tm), -jnp.inf, F32)
        for _ in range(_N_RANK):
            m = jnp.max(c, axis=0, keepdims=True)
            eq = c == m
            cnt = jnp.sum(eq.astype(F32), axis=0, keepdims=True)
            v16 = jnp.where(n_removed < PEER_TOPK, m, v16)
            v17 = jnp.where(n_removed < _N_RANK, m, v17)
            n_removed = n_removed + cnt
            c = jnp.where(eq, -jnp.inf, c)
        tau = 0.5 * (v16 + v17)
        tau = jnp.where(v17 == -jnp.inf, v16, tau)
        m_tot = a1[0] + a2[0]
        z = jnp.sum(jnp.where(cand >= tau, jnp.exp(cand - m_tot), 0.0), axis=0, keepdims=True)
        e2 = jnp.exp(s2 - a2[0]) / z
        e1 = jnp.exp(s1 - a1[0])
        th = tau - s1
        for cc in range(nchunk):
            ls = slice(cc * LANES, (cc + 1) * LANES)
            s2_ref[cc, h] = s2[:, ls]
            e2_ref[cc, h] = e2[:, ls]
            th_ref[cc, h] = th[:, ls]
            e1_ref[cc, h] = e1[:, ls]


def _post(x, mix, w_out, ffn_w, w_q, sub_keys, tm=256):
    t, d = x.shape
    tm = min(tm, t)
    nchunk = tm // LANES
    stat = jax.ShapeDtypeStruct((t // LANES, PEER_HEADS, PEER_N_KEYS, LANES), F32)
    stat_spec = pl.BlockSpec((nchunk, PEER_HEADS, PEER_N_KEYS, LANES), lambda i: (i, 0, 0, 0))
    kern = functools.partial(_post_kernel, tm=tm)
    return pl.pallas_call(
        kern,
        grid=(t // tm,),
        in_specs=[
            pl.BlockSpec((tm, d), lambda i: (i, 0)),
            pl.BlockSpec((tm, d), lambda i: (i, 0)),
            pl.BlockSpec((d, d), lambda i: (0, 0)),
            pl.BlockSpec((1, d), lambda i: (0, 0)),
            pl.BlockSpec((d, d), lambda i: (0, 0)),
            pl.BlockSpec((2 * PEER_HEADS, PEER_N_KEYS, LANES), lambda i: (0, 0, 0)),
        ],
        out_specs=[
            pl.BlockSpec((tm, d), lambda i: (i, 0)),
            pl.BlockSpec((d, tm), lambda i: (0, i)),
            stat_spec, stat_spec, stat_spec, stat_spec,
        ],
        out_shape=[
            jax.ShapeDtypeStruct((t, d), F32),
            jax.ShapeDtypeStruct((d, t), BF16),
            stat, stat, stat, stat,
        ],
        scratch_shapes=[pltpu.VMEM((_CAND_ROWS, tm), F32)],
        compiler_params=_params(1),
        name="post",
    )(x, mix, w_out, ffn_w, w_q, sub_keys)


_ROW_TILE = 32


def _gelu(x):
    return 0.5 * x * (1.0 + lax.erf(x * (2.0 ** -0.5)))


def _peer_kernel(hnt_ref, u0_ref, u1_ref, vt0_ref, vtp_ref, vtl_ref, s2_ref, e2_ref, th_ref, e1_ref, acc_ref,
                 at0_ref, at1_ref, pt0_ref, pt1_ref, *, tb, eb):
    n1 = eb // PEER_N_KEYS
    nchunk = tb // LANES
    nrt = PEER_N_KEYS // _ROW_TILE
    d = acc_ref.shape[0]

    @pl.when(pl.program_id(1) == 0)
    def _():
        acc_ref[...] = jnp.zeros_like(acc_ref)
        pt1_ref[...] = jnp.zeros_like(pt1_ref)

    hnt = hnt_ref[...]
    at0_ref[...] = _dot(u0_ref[...], hnt)
    at1_ref[...] = _dot(u1_ref[...], hnt)

    def tile(at_ref, pt_ref, row0, cc, rt):
        ls = slice(cc * LANES, (cc + 1) * LANES)
        rs = pl.ds(pl.multiple_of(rt * _ROW_TILE, _ROW_TILE), _ROW_TILE)
        w = [jnp.zeros((_ROW_TILE, LANES), F32) for _ in range(n1)]
        for h in range(PEER_HEADS):
            s2 = s2_ref[cc, h, rs, :]
            e2 = e2_ref[cc, h, rs, :]
            for i in range(n1):
                th = th_ref[cc, h, row0 + i:row0 + i + 1, :]
                e1 = e1_ref[cc, h, row0 + i:row0 + i + 1, :]
                w[i] = w[i] + jnp.where(s2 >= th, e2, 0.0) * e1
        for i in range(n1):
            r = pl.ds(pl.multiple_of(i * PEER_N_KEYS + rt * _ROW_TILE, _ROW_TILE), _ROW_TILE)
            pt_ref[r, ls] = (w[i] * _gelu(at_ref[r, ls])).astype(BF16)

    c_rows = d // nrt

    def body0(rt, carry):
        for cc in range(nchunk):
            tile(at0_ref, pt0_ref, 0, cc, rt)
        r = pl.ds(pl.multiple_of(rt * c_rows, c_rows), c_rows)
        acc_ref[r, :] += _dot(vtp_ref[r, :], pt1_ref[...])
        return carry
    lax.fori_loop(0, nrt, body0, 0)

    def body1(rt, carry):
        for cc in range(nchunk):
            tile(at1_ref, pt1_ref, n1, cc, rt)
        r = pl.ds(pl.multiple_of(rt * c_rows, c_rows), c_rows)
        acc_ref[r, :] += _dot(vt0_ref[r, :], pt0_ref[...])
        return carry
    lax.fori_loop(0, nrt, body1, 0)

    @pl.when(pl.program_id(1) == pl.num_programs(1) - 1)
    def _():
        acc_ref[...] += _dot(vtl_ref[...], pt1_ref[...])


def _peer(hnt, u16, v16, s2, e2, th, e1, tb=512, eb=512):
    d, t = hnt.shape
    tb = min(tb, t)
    nchunk = tb // LANES
    n1 = eb // PEER_N_KEYS
    nk = PEER_N_EXPERTS // (2 * eb)
    vt_slabs = jnp.swapaxes(v16.reshape(2 * nk, eb, d), 1, 2)
    assert 2 * n1 == SUBLANES, "a pair of expert blocks spans one sublane group of first-key statistics"
    stat_spec = pl.BlockSpec((nchunk, PEER_HEADS, PEER_N_KEYS, LANES), lambda i, k: (i, 0, 0, 0))
    row_spec = pl.BlockSpec((nchunk, PEER_HEADS, SUBLANES, LANES), lambda i, k: (i, 0, k, 0))
    kern = functools.partial(_peer_kernel, tb=tb, eb=eb)
    return pl.pallas_call(
        kern,
        grid=(t // tb, nk),
        in_specs=[
            pl.BlockSpec((d, tb), lambda i, k: (0, i)),
            pl.BlockSpec((eb, d), lambda i, k: (2 * k, 0)),
            pl.BlockSpec((eb, d), lambda i, k: (2 * k + 1, 0)),
            pl.BlockSpec((None, d, eb), lambda i, k: (2 * k, 0, 0)),
            pl.BlockSpec((None, d, eb), lambda i, k: (2 * jnp.maximum(k - 1, 0) + 1, 0, 0)),
            pl.BlockSpec((None, d, eb), lambda i, k: (2 * nk - 1, 0, 0)),
            stat_spec, stat_spec, row_spec, row_spec,
        ],
        out_specs=pl.BlockSpec((d, tb), lambda i, k: (0, i)),
        out_shape=jax.ShapeDtypeStruct((d, t), F32),
        scratch_shapes=[pltpu.VMEM((eb, tb), F32), pltpu.VMEM((eb, tb), F32),
                        pltpu.VMEM((eb, tb), BF16), pltpu.VMEM((eb, tb), BF16)],
        compiler_params=_params(2),
        name="peer",
    )(hnt, u16, u16, vt_slabs, vt_slabs, vt_slabs, s2, e2, th, e1)


def _final_kernel(h_ref, pt_ref, w_ref, o_ref):
    hf = h_ref[...] + pt_ref[...].T
    o_ref[...] = hf * lax.rsqrt(jnp.mean(hf * hf, axis=-1, keepdims=True) + EPS) * w_ref[...]


def _final(h1, peer_out_t, w_row, tm=512):
    t, d = h1.shape
    tm = min(tm, t)
    spec = pl.BlockSpec((tm, d), lambda i: (i, 0))
    return pl.pallas_call(
        _final_kernel,
        grid=(t // tm,),
        in_specs=[spec, pl.BlockSpec((d, tm), lambda i: (0, i)), pl.BlockSpec((1, d), lambda i: (0, 0))],
        out_specs=spec,
        out_shape=jax.ShapeDtypeStruct((t, d), F32),
        compiler_params=_params(1),
        name="final",
    )(h1, peer_out_t, w_row)


def _row(v, width=None):
    v = v.astype(F32).reshape(1, -1)
    if width is not None and v.shape[1] < width:
        v = jnp.pad(v, ((0, 0), (0, width - v.shape[1])))
    return v


def _layer(h, mix_norm_w, w_in, gate_b, ssm_conv_w, ssm_conv_b, ssm_dt_bias, ssm_a_log, ssm_d,
           ssm_norm_w, gdn_conv_w, gdn_dt_bias, gdn_a_log, gdn_norm_w, w_branch_ssm, w_branch_gdn,
           w_out, ffn_norm_w, peer_w_q, peer_sub_keys, peer_u, peer_v, out_norm_w):
    ssm_proj = SSM_D_INNER + (SSM_D_INNER + 2 * SSM_N_GROUPS * SSM_D_STATE) + SSM_N_HEADS
    gdn_conv_dim = 2 * GDN_KEY_DIM + GDN_VAL_DIM
    o_dt = ssm_proj - SSM_N_HEADS
    o_qkv = ssm_proj
    o_gz = o_qkv + gdn_conv_dim
    o_beta = o_gz + GDN_VAL_DIM
    o_gate = o_beta + 2 * GDN_N_HEADS
    w_main = jnp.concatenate(
        [w_in[:, :o_dt], w_in[:, o_qkv:o_beta], w_in[:, o_gate:]], axis=1).astype(BF16)
    w_small = jnp.concatenate(
        [w_in[:, o_dt:o_qkv], w_in[:, o_beta:o_gate],
         jnp.zeros((D_MODEL, LANES - SSM_N_HEADS - 2 * GDN_N_HEADS), w_in.dtype)], axis=1).astype(BF16)

    proj, small = _inproj(h, _row(mix_norm_w), w_main, w_small)

    cw = ssm_conv_w.astype(F32)
    cbias = _row(ssm_conv_b)
    nx = SSM_D_INNER
    nb = SSM_N_GROUPS * SSM_D_STATE
    y_ssm = _ssd(proj, small,
                 cw[:, :nx], cw[:, nx:nx + nb], cw[:, nx + nb:],
                 cbias[:, :nx], cbias[:, nx:nx + nb], cbias[:, nx + nb:],
                 _row(ssm_dt_bias, LANES), _row(ssm_a_log, LANES),
                 _row(jnp.repeat(ssm_d, SSM_HEAD_DIM)), _row(ssm_norm_w))

    gw = gdn_conv_w.astype(F32)
    zeros_b = jnp.zeros((SMALL_ALPHA,), F32)
    y_gdn = _gdn(proj, small,
                 gw[:, :GDN_KEY_DIM], gw[:, GDN_KEY_DIM:2 * GDN_KEY_DIM], gw[:, 2 * GDN_KEY_DIM:],
                 _row(jnp.concatenate([zeros_b, gdn_dt_bias.astype(F32)]), LANES),
                 _row(jnp.concatenate([zeros_b, gdn_a_log.astype(F32)]), LANES),
                 _row(gdn_norm_w))

    mix = _mix(y_ssm, y_gdn, proj, gate_b.astype(F32),
               w_branch_ssm.astype(BF16), w_branch_gdn.astype(BF16))

    sk = peer_sub_keys.reshape(2 * PEER_HEADS, PEER_N_KEYS, LANES).astype(BF16)
    h1, hnt, s2, e2, th, e1 = _post(h, mix, w_out.astype(BF16), _row(ffn_norm_w),
                                    peer_w_q.astype(BF16), sk)

    peer_out_t = _peer(hnt, peer_u.astype(BF16), peer_v.astype(BF16), s2, e2, th, e1)
    return _final(h1, peer_out_t, _row(out_norm_w))


def kernel(x, mix_norm_w, w_in, gate_b, ssm_conv_w, ssm_conv_b, ssm_dt_bias, ssm_a_log, ssm_d,
           ssm_norm_w, gdn_conv_w, gdn_dt_bias, gdn_a_log, gdn_norm_w, w_branch_ssm, w_branch_gdn,
           w_out, ffn_norm_w, peer_w_q, peer_sub_keys, peer_u, peer_v, final_norm_w):
    b, t, d = x.shape
    assert b == 1 and mix_norm_w.shape[0] == 1, "single sequence, single layer"
    out = _layer(x[0], mix_norm_w[0], w_in[0], gate_b[0], ssm_conv_w[0], ssm_conv_b[0],
                 ssm_dt_bias[0], ssm_a_log[0], ssm_d[0], ssm_norm_w[0], gdn_conv_w[0],
                 gdn_dt_bias[0], gdn_a_log[0], gdn_norm_w[0], w_branch_ssm[0], w_branch_gdn[0],
                 w_out[0], ffn_norm_w[0], peer_w_q[0], peer_sub_keys[0], peer_u[0], peer_v[0],
                 final_norm_w)
    return out[None]
```

```python
import functools

import jax
import jax.numpy as jnp
from jax import lax
from jax.experimental import pallas as pl
from jax.experimental.pallas import tpu as pltpu

F32 = jnp.float32
BF16 = jnp.bfloat16
HIGHEST = lax.Precision.HIGHEST

EPS = 1e-6
D_MODEL = 2048
LANES = 128
SUBLANES = 8
SSM_D_INNER = 4096
SSM_HEAD_DIM = 64
SSM_N_HEADS = 64
SSM_N_GROUPS = 8
SSM_HPG = 8
SSM_D_STATE = 128
SSM_CHUNK = 128
SSM_GROUP_W = SSM_HPG * SSM_HEAD_DIM
GDN_N_HEADS = 16
GDN_HEAD_K = 128
GDN_HEAD_V = 256
GDN_CHUNK = 64
GDN_KEY_DIM = GDN_N_HEADS * GDN_HEAD_K
GDN_VAL_DIM = GDN_N_HEADS * GDN_HEAD_V
CONV_K = 4
PEER_HEADS = 8
PEER_N_KEYS = 128
PEER_TOPK = 16
PEER_N_EXPERTS = PEER_N_KEYS * PEER_N_KEYS

COL_SSM_Z = 0
COL_SSM_XBC = 4096
COL_GDN_QKV = 10240
COL_GDN_Z = 18432
COL_GATE = 22528
N_MAIN = 26624
SMALL_BETA = 64
SMALL_ALPHA = 80

VMEM_LIMIT = 56 * 1024 * 1024


def _softplus(x):
    return jnp.maximum(x, 0.0) + jnp.log1p(jnp.exp(-jnp.abs(x)))


def _silu(x):
    return x * jax.nn.sigmoid(x)


def _dot(a, b, **kw):
    return jnp.dot(a, b, preferred_element_type=F32, **kw)


def _dot_nt(a, b, **kw):
    return lax.dot_general(a, b, (((1,), (1,)), ((), ())), preferred_element_type=F32, **kw)


def _dot_tn(a, b, **kw):
    return lax.dot_general(a, b, (((0,), (0,)), ((), ())), preferred_element_type=F32, **kw)


def _params(n_grid):
    return pltpu.CompilerParams(dimension_semantics=("arbitrary",) * n_grid,
                                vmem_limit_bytes=VMEM_LIMIT)


def _inproj_kernel(x_ref, nw_ref, w_ref, ws_ref, o_ref, os_ref, xn_ref):
    @pl.when(pl.program_id(1) == 0)
    def _():
        x = x_ref[...]
        ms = jnp.mean(x * x, axis=-1, keepdims=True)
        xn = (x * lax.rsqrt(ms + EPS) * nw_ref[...]).astype(BF16)
        xn_ref[...] = xn
        os_ref[...] = _dot(xn, ws_ref[...])

    o_ref[...] = _dot(xn_ref[...], w_ref[...]).astype(BF16)


def _inproj(x, norm_w, w_main, w_small, tm=1024, tn=1024):
    t, d = x.shape
    n = w_main.shape[1]
    tm = min(tm, t)
    return pl.pallas_call(
        _inproj_kernel,
        grid=(t // tm, n // tn),
        in_specs=[
            pl.BlockSpec((tm, d), lambda i, j: (i, 0)),
            pl.BlockSpec((1, d), lambda i, j: (0, 0)),
            pl.BlockSpec((d, tn), lambda i, j: (0, j)),
            pl.BlockSpec((d, LANES), lambda i, j: (0, 0)),
        ],
        out_specs=[
            pl.BlockSpec((tm, tn), lambda i, j: (i, j)),
            pl.BlockSpec((tm, LANES), lambda i, j: (i, 0)),
        ],
        out_shape=[jax.ShapeDtypeStruct((t, n), BF16), jax.ShapeDtypeStruct((t, LANES), F32)],
        scratch_shapes=[pltpu.VMEM((tm, d), BF16)],
        compiler_params=_params(2),
        name="inproj",
    )(x, norm_w, w_main, w_small)


def _shift_mats(tc):
    return jnp.concatenate([jnp.eye(tc, k=-(CONV_K - 1 - k), dtype=BF16) for k in range(CONV_K - 1)], axis=0)


def _conv_block(raw_ref, carry_ref, w_ref, shift_ref, tc):
    x16 = raw_ref[...]
    xf = x16.astype(F32)
    y = xf * w_ref[CONV_K - 1:CONV_K, :]
    c8 = carry_ref[...]
    row8 = lax.broadcasted_iota(jnp.int32, c8.shape, 0)
    head = jnp.zeros_like(c8)
    shifted = _dot(shift_ref[...], x16)
    for k in range(CONV_K - 1):
        lag = CONV_K - 1 - k
        y = y + shifted[k * tc:(k + 1) * tc] * w_ref[k:k + 1, :]
        head = head + jnp.where(row8 < lag, pltpu.roll(c8, shift=lag, axis=0), 0.0) * w_ref[k:k + 1, :]
    carry_ref[...] = xf[tc - SUBLANES:tc]
    return jnp.concatenate([y[0:SUBLANES] + head, y[SUBLANES:]], axis=0)


def _ssd_kernel(xs_ref, b_ref, c_ref, z_ref, sm_ref, cwx_ref, cwb_ref, cwc_ref, cbx_ref, cbb_ref,
                cbc_ref, dtb_ref, alog_ref, dsk_ref, nw_ref, sh_ref, o_ref,
                xbuf, bbuf, cbuf, state_ref, acst_ref, *, tc, gps):
    g0 = pl.program_id(0) * gps
    L, GW, NS = SSM_CHUNK, SSM_GROUP_W, SSM_D_STATE

    @pl.when(pl.program_id(1) == 0)
    def _():
        xbuf[...] = jnp.zeros_like(xbuf)
        bbuf[...] = jnp.zeros_like(bbuf)
        cbuf[...] = jnp.zeros_like(cbuf)
        state_ref[...] = jnp.zeros_like(state_ref)

    xs_all = _silu(_conv_block(xs_ref, xbuf, cwx_ref, sh_ref, tc) + cbx_ref[...])
    b_all = _silu(_conv_block(b_ref, bbuf, cwb_ref, sh_ref, tc) + cbb_ref[...])
    c_all = _silu(_conv_block(c_ref, cbuf, cwc_ref, sh_ref, tc) + cbc_ref[...])

    row = lax.broadcasted_iota(jnp.int32, (L, L), 0)
    col = lax.broadcasted_iota(jnp.int32, (L, L), 1)
    causal = row >= col
    tri = causal.astype(F32)
    lo_half = col < SSM_HEAD_DIM
    a_row = -jnp.exp(alog_ref[...])

    for c in range(tc // L):
        sl = slice(c * L, (c + 1) * L)
        dt_full = _softplus(sm_ref[sl, :] + dtb_ref[...])
        acs_full = _dot(tri, dt_full * a_row, precision=HIGHEST)
        acst_ref[c] = acs_full.T
        for gg in range(gps):
            xs = xs_all[sl, gg * GW:(gg + 1) * GW]
            bm16 = b_all[sl, gg * NS:(gg + 1) * NS].astype(BF16)
            cm16 = c_all[sl, gg * NS:(gg + 1) * NS].astype(BF16)
            cb = _dot_nt(cm16, bm16)
            ys = []
            for p in range(SSM_HPG // 2):
                acs_bc, dt_bc, scores = [], [], []
                for r in (2 * p, 2 * p + 1):
                    j = (g0 + gg) * SSM_HPG + r
                    sel = col == j
                    a_c = jnp.sum(jnp.where(sel, acs_full, 0.0), axis=1, keepdims=True)
                    d_c = jnp.sum(jnp.where(sel, dt_full, 0.0), axis=1, keepdims=True)
                    a_b = jnp.broadcast_to(a_c, (L, L))
                    acs_bc.append(a_b)
                    dt_bc.append(jnp.broadcast_to(d_c, (L, L)))
                    a_r = acst_ref[c, pl.ds(j, 1), :]
                    seg = jnp.where(causal, a_b - a_r, -jnp.inf)
                    scores.append((cb * jnp.exp(seg)).astype(BF16))
                acs_e = jnp.where(lo_half, acs_bc[0], acs_bc[1])
                dt_e = jnp.where(lo_half, dt_bc[0], dt_bc[1])
                x_p = xs[:, p * LANES:(p + 1) * LANES]
                lanes = slice(gg * GW + p * LANES, gg * GW + (p + 1) * LANES)
                xdt = x_p * dt_e
                last = acs_e[L - 1:L, :]
                xdtw = (xdt * jnp.exp(last - acs_e)).astype(BF16)
                s2 = jnp.concatenate(scores, axis=1)
                x2 = jnp.concatenate([jnp.where(lo_half, xdt, 0.0), jnp.where(lo_half, 0.0, xdt)],
                                     axis=0).astype(BF16)
                y_diag = _dot(s2, x2)
                st = state_ref[:, lanes]
                y_off = _dot(cm16, st.astype(BF16)) * jnp.exp(acs_e)
                state_ref[:, lanes] = st * jnp.exp(last) + _dot_tn(bm16, xdtw)
                ys.append(y_diag + y_off + x_p * dsk_ref[:, lanes])
            gl = slice(gg * GW, (gg + 1) * GW)
            y = jnp.concatenate(ys, axis=1) * _silu(z_ref[sl, gl].astype(F32))
            y = y * lax.rsqrt(jnp.mean(y * y, axis=-1, keepdims=True) + EPS)
            o_ref[sl, gl] = (y * nw_ref[:, gl]).astype(BF16)


def _ssd(proj, small, cwx, cwb, cwc, cbx, cbb, cbc, dtb_row, alog_row, dsk_row, nw_row, tc=256, gps=4):
    t = proj.shape[0]
    tc = min(tc, t)
    gw, ns = gps * SSM_GROUP_W, gps * SSM_D_STATE
    xs_blk = COL_SSM_XBC // gw
    b_blk = (COL_SSM_XBC + SSM_D_INNER) // ns
    c_blk = b_blk + SSM_N_GROUPS // gps
    z_blk = COL_SSM_Z // gw
    kern = functools.partial(_ssd_kernel, tc=tc, gps=gps)
    return pl.pallas_call(
        kern,
        grid=(SSM_N_GROUPS // gps, t // tc),
        in_specs=[
            pl.BlockSpec((tc, gw), lambda g, i: (i, xs_blk + g)),
            pl.BlockSpec((tc, ns), lambda g, i: (i, b_blk + g)),
            pl.BlockSpec((tc, ns), lambda g, i: (i, c_blk + g)),
            pl.BlockSpec((tc, gw), lambda g, i: (i, z_blk + g)),
            pl.BlockSpec((tc, LANES), lambda g, i: (i, 0)),
            pl.BlockSpec((CONV_K, gw), lambda g, i: (0, g)),
            pl.BlockSpec((CONV_K, ns), lambda g, i: (0, g)),
            pl.BlockSpec((CONV_K, ns), lambda g, i: (0, g)),
            pl.BlockSpec((1, gw), lambda g, i: (0, g)),
            pl.BlockSpec((1, ns), lambda g, i: (0, g)),
            pl.BlockSpec((1, ns), lambda g, i: (0, g)),
            pl.BlockSpec((1, LANES), lambda g, i: (0, 0)),
            pl.BlockSpec((1, LANES), lambda g, i: (0, 0)),
            pl.BlockSpec((1, gw), lambda g, i: (0, g)),
            pl.BlockSpec((1, gw), lambda g, i: (0, g)),
            pl.BlockSpec(((CONV_K - 1) * tc, tc), lambda g, i: (0, 0)),
        ],
        out_specs=pl.BlockSpec((tc, gw), lambda g, i: (i, g)),
        out_shape=jax.ShapeDtypeStruct((t, SSM_D_INNER), BF16),
        scratch_shapes=[
            pltpu.VMEM((SUBLANES, gw), F32),
            pltpu.VMEM((SUBLANES, ns), F32),
            pltpu.VMEM((SUBLANES, ns), F32),
            pltpu.VMEM((SSM_D_STATE, gw), F32),
            pltpu.VMEM((tc // SSM_CHUNK, LANES, SSM_CHUNK), F32),
        ],
        compiler_params=_params(2),
        name="ssd",
    )(proj, proj, proj, proj, small, cwx, cwb, cwc, cbx, cbb, cbc, dtb_row, alog_row, dsk_row, nw_row,
      _shift_mats(tc))


def _gdn_kernel(q_ref, k_ref, v_ref, z_ref, sm_ref, cwq_ref, cwk_ref, cwv_ref, dtb_ref, alog_ref,
                nw_ref, sh_ref, o_ref, qbuf, kbuf, vbuf, s_ref, *, tc, hps):
    h0 = pl.program_id(0) * hps
    C, DK, DV = GDN_CHUNK, GDN_HEAD_K, GDN_HEAD_V

    @pl.when(pl.program_id(1) == 0)
    def _():
        qbuf[...] = jnp.zeros_like(qbuf)
        kbuf[...] = jnp.zeros_like(kbuf)
        vbuf[...] = jnp.zeros_like(vbuf)
        s_ref[...] = jnp.zeros_like(s_ref)

    q_cv = _silu(_conv_block(q_ref, qbuf, cwq_ref, sh_ref, tc))
    k_cv = _silu(_conv_block(k_ref, kbuf, cwk_ref, sh_ref, tc))
    v_cv = _silu(_conv_block(v_ref, vbuf, cwv_ref, sh_ref, tc))

    sm = sm_ref[...]
    lane = lax.broadcasted_iota(jnp.int32, sm.shape, 1)
    beta_full = jax.nn.sigmoid(sm)
    g_full = -jnp.exp(alog_ref[...]) * _softplus(sm + dtb_ref[...])
    q_hd, k_hd, v_hd, beta_hd, g_hd = [], [], [], [], []
    for hd in range(hps):
        q = q_cv[:, hd * DK:(hd + 1) * DK]
        k = k_cv[:, hd * DK:(hd + 1) * DK]
        q_hd.append(q * lax.rsqrt(jnp.sum(q * q, axis=-1, keepdims=True) + EPS) * (DK ** -0.5))
        k_hd.append(k * lax.rsqrt(jnp.sum(k * k, axis=-1, keepdims=True) + EPS))
        v_hd.append(v_cv[:, hd * DV:(hd + 1) * DV])
        beta_hd.append(jnp.sum(jnp.where(lane == SMALL_BETA + h0 + hd, beta_full, 0.0),
                               axis=1, keepdims=True))
        g_hd.append(jnp.sum(jnp.where(lane == SMALL_ALPHA + h0 + hd, g_full, 0.0),
                            axis=1, keepdims=True))

    row = lax.broadcasted_iota(jnp.int32, (C, C), 0)
    col = lax.broadcasted_iota(jnp.int32, (C, C), 1)
    causal = row >= col
    strict = row > col
    tri = causal.astype(F32)
    eye = (row == col).astype(F32)

    nc = tc // C
    items = [(hd, slice(c * C, (c + 1) * C)) for hd in range(hps) for c in range(nc)]
    rng = range(len(items))
    strict_w = (lax.broadcasted_iota(jnp.int32, (C, LANES), 0)
                > lax.broadcasted_iota(jnp.int32, (C, LANES), 1))
    qs = [q_hd[hd][sl] for hd, sl in items]
    ks = [k_hd[hd][sl] for hd, sl in items]
    vs = [v_hd[hd][sl] for hd, sl in items]
    betas = [beta_hd[hd][sl] for hd, sl in items]
    g_bc = [jnp.broadcast_to(g_hd[hd][sl], (C, LANES)) for hd, sl in items]
    cums = [_dot(tri, jnp.concatenate([g_bc[c], jnp.where(strict_w, g_bc[c], 0.0)], axis=1),
                 precision=HIGHEST) for c in rng]
    gc_bc = [x[:, :LANES] for x in cums]
    decay = [jnp.exp(jnp.where(causal, x[:, LANES:LANES + C], -jnp.inf)) for x in cums]
    k16 = [k.astype(BF16) for k in ks]
    q16 = [q.astype(BF16) for q in qs]
    kk = [_dot_nt(k16[c], k16[c]) for c in rng]
    qk = [_dot_nt(q16[c], k16[c]) for c in rng]
    p = [-jnp.where(strict, kk[c] * betas[c] * decay[c], 0.0) for c in rng]
    x = [eye + p[c] for c in rng]
    for _ in range(C.bit_length() - 2):
        p16 = [pp.astype(BF16) for pp in p]
        p = [_dot(p16[c], p16[c]) for c in rng]
        x = [x[c] + _dot(x[c].astype(BF16), p[c].astype(BF16)) for c in rng]
    t_inv = [xx.astype(BF16) for xx in x]
    egc = [jnp.exp(gc) for gc in gc_bc]
    u = [_dot(t_inv[c], (vs[c] * betas[c]).astype(BF16)).astype(BF16) for c in rng]
    w = [_dot(t_inv[c], (ks[c] * (betas[c] * egc[c])).astype(BF16)).astype(BF16) for c in rng]
    attn = [jnp.where(causal, qk[c] * decay[c], 0.0).astype(BF16) for c in rng]
    gc_last = [gc[C - 1:C, :] for gc in gc_bc]
    k_dec = [(ks[c] * jnp.exp(gc_last[c] - gc_bc[c])).astype(BF16) for c in rng]
    c_dec = [jnp.exp(gl[:, :1]) for gl in gc_last]
    a_mat = [_dot_tn(k_dec[c], w[c]).astype(BF16) for c in rng]
    n_mat = [_dot_tn(k_dec[c], u[c]) for c in rng]
    q_eff = [(qs[c] * egc[c] - _dot(attn[c], w[c])).astype(BF16) for c in rng]
    o_u = [_dot(attn[c], u[c]) for c in rng]

    s = [s_ref[hd] for hd in range(hps)]
    for c in range(nc):
        sl = slice(c * C, (c + 1) * C)
        for hd in range(hps):
            it = hd * nc + c
            s16 = s[hd].astype(BF16)
            o = _dot(q_eff[it], s16) + o_u[it]
            s[hd] = s[hd] * c_dec[it] - _dot(a_mat[it], s16) + n_mat[it]
            o = o * lax.rsqrt(jnp.mean(o * o, axis=-1, keepdims=True) + EPS) * nw_ref[...]
            z = z_ref[sl, hd * DV:(hd + 1) * DV].astype(F32)
            o_ref[sl, hd * DV:(hd + 1) * DV] = (o * _silu(z)).astype(BF16)
    for hd in range(hps):
        s_ref[hd] = s[hd]


def _gdn(proj, small, cwq, cwk, cwv, dtb_row, alog_row, nw_row, tc=256, hps=8):
    t = proj.shape[0]
    tc = min(tc, t)
    dk, dv = hps * GDN_HEAD_K, hps * GDN_HEAD_V
    q_blk = COL_GDN_QKV // dk
    k_blk = q_blk + GDN_N_HEADS // hps
    v_blk = (COL_GDN_QKV + 2 * GDN_KEY_DIM) // dv
    z_blk = COL_GDN_Z // dv
    kern = functools.partial(_gdn_kernel, tc=tc, hps=hps)
    return pl.pallas_call(
        kern,
        grid=(GDN_N_HEADS // hps, t // tc),
        in_specs=[
            pl.BlockSpec((tc, dk), lambda h, i: (i, q_blk + h)),
            pl.BlockSpec((tc, dk), lambda h, i: (i, k_blk + h)),
            pl.BlockSpec((tc, dv), lambda h, i: (i, v_blk + h)),
            pl.BlockSpec((tc, dv), lambda h, i: (i, z_blk + h)),
            pl.BlockSpec((tc, LANES), lambda h, i: (i, 0)),
            pl.BlockSpec((CONV_K, dk), lambda h, i: (0, h)),
            pl.BlockSpec((CONV_K, dk), lambda h, i: (0, h)),
            pl.BlockSpec((CONV_K, dv), lambda h, i: (0, h)),
            pl.BlockSpec((1, LANES), lambda h, i: (0, 0)),
            pl.BlockSpec((1, LANES), lambda h, i: (0, 0)),
            pl.BlockSpec((1, GDN_HEAD_V), lambda h, i: (0, 0)),
            pl.BlockSpec(((CONV_K - 1) * tc, tc), lambda h, i: (0, 0)),
        ],
        out_specs=pl.BlockSpec((tc, dv), lambda h, i: (i, h)),
        out_shape=jax.ShapeDtypeStruct((t, GDN_VAL_DIM), BF16),
        scratch_shapes=[
            pltpu.VMEM((SUBLANES, dk), F32),
            pltpu.VMEM((SUBLANES, dk), F32),
            pltpu.VMEM((SUBLANES, dv), F32),
            pltpu.VMEM((hps, GDN_HEAD_K, GDN_HEAD_V), F32),
        ],
        compiler_params=_params(2),
        name="gdn",
    )(proj, proj, proj, proj, small, cwq, cwk, cwv, dtb_row, alog_row, nw_row, _shift_mats(tc))


def _mix_kernel(ys_ref, yg_ref, g0_ref, g1_ref, gb_ref, ws_ref, wg_ref, o_ref):
    a = _dot(ys_ref[...], ws_ref[...])
    b = _dot(yg_ref[...], wg_ref[...])
    g0 = jax.nn.sigmoid(g0_ref[...].astype(F32) + gb_ref[0:1, :])
    g1 = jax.nn.sigmoid(g1_ref[...].astype(F32) + gb_ref[1:2, :])
    o_ref[...] = (g0 * a + g1 * b).astype(BF16)


def _mix(y_ssm, y_gdn, proj, gate_b, w_s, w_g, tm=512, tn=512):
    t = y_ssm.shape[0]
    tm = min(tm, t)
    g0_blk = COL_GATE // tn
    g1_blk = (COL_GATE + D_MODEL) // tn
    return pl.pallas_call(
        _mix_kernel,
        grid=(D_MODEL // tn, t // tm),
        in_specs=[
            pl.BlockSpec((tm, SSM_D_INNER), lambda n, i: (i, 0)),
            pl.BlockSpec((tm, GDN_VAL_DIM), lambda n, i: (i, 0)),
            pl.BlockSpec((tm, tn), lambda n, i: (i, g0_blk + n)),
            pl.BlockSpec((tm, tn), lambda n, i: (i, g1_blk + n)),
            pl.BlockSpec((2, tn), lambda n, i: (0, n)),
            pl.BlockSpec((SSM_D_INNER, tn), lambda n, i: (0, n)),
            pl.BlockSpec((GDN_VAL_DIM, tn), lambda n, i: (0, n)),
        ],
        out_specs=pl.BlockSpec((tm, tn), lambda n, i: (i, n)),
        out_shape=jax.ShapeDtypeStruct((t, D_MODEL), BF16),
        compiler_params=_params(2),
        name="mix",
    )(y_ssm, y_gdn, proj, proj, gate_b, w_s, w_g)


_N_RANK = PEER_TOPK + 1
_CAND_PAIRS = [(i, j) for i in range(_N_RANK) for j in range(_N_RANK) if (i + 1) * (j + 1) <= _N_RANK]
_CAND_ROWS = -(-len(_CAND_PAIRS) // SUBLANES) * SUBLANES


def _top_desc(s, n):
    nv = s.shape[0] // SUBLANES
    v = [s[j * SUBLANES:(j + 1) * SUBLANES, :] for j in range(nv)]
    k = 2
    while k <= nv:
        j = k // 2
        while j >= 1:
            for i in range(nv):
                l = i ^ j
                if l > i:
                    hi, lo = jnp.maximum(v[i], v[l]), jnp.minimum(v[i], v[l])
                    v[i], v[l] = (hi, lo) if (i & k) == 0 else (lo, hi)
            j //= 2
        k *= 2
    out = []
    for t in range(n):
        m = jnp.max(v[0], axis=0, keepdims=True)
        out.append(m)
        pop = v[0] == m
        live = min(nv, n - t)
        for j in range(live - 1):
            v[j] = jnp.where(pop, v[j + 1], v[j])
        v[live - 1] = jnp.where(pop, -jnp.inf, v[live - 1])
    return out


def _post_kernel(x_ref, mix_ref, wo_ref, fw_ref, wq_ref, sk_ref,
                 h1_ref, hnt_ref, s2_ref, e2_ref, th_ref, e1_ref, cand_ref, *, tm):
    h1 = x_ref[...] + _dot(mix_ref[...], wo_ref[...])
    h1_ref[...] = h1
    hn = h1 * lax.rsqrt(jnp.mean(h1 * h1, axis=-1, keepdims=True) + EPS) * fw_ref[...]
    hnt_ref[...] = hn.T.astype(BF16)
    qv = _dot(hn.astype(BF16), wq_ref[...]).astype(BF16)
    nchunk = tm // LANES
    for h in range(PEER_HEADS):
        s1 = _dot_nt(sk_ref[2 * h], qv[:, (2 * h) * LANES:(2 * h + 1) * LANES])
        s2 = _dot_nt(sk_ref[2 * h + 1], qv[:, (2 * h + 1) * LANES:(2 * h + 2) * LANES])
        a1 = _top_desc(s1, _N_RANK)
        a2 = _top_desc(s2, _N_RANK)
        cand_ref[...] = jnp.full(cand_ref.shape, -jnp.inf, F32)
        for r, (i, j) in enumerate(_CAND_PAIRS):
            cand_ref[pl.ds(r, 1), :] = a1[i] + a2[j]
        cand = cand_ref[...]
        c = cand
        n_removed = jnp.zeros((1, tm), F32)
        v16 = jnp.full((1, tm), -jnp.inf, F32)
        v17 = jnp.full((1, tm), -jnp.inf, F32)
        for _ in range(_N_RANK):
            m = jnp.max(c, axis=0, keepdims=True)
            eq = c == m
            cnt = jnp.sum(eq.astype(F32), axis=0, keepdims=True)
            v16 = jnp.where(n_removed < PEER_TOPK, m, v16)
            v17 = jnp.where(n_removed < _N_RANK, m, v17)
            n_removed = n_removed + cnt
            c = jnp.where(eq, -jnp.inf, c)
        tau = 0.5 * (v16 + v17)
        tau = jnp.where(v17 == -jnp.inf, v16, tau)
        m_tot = a1[0] + a2[0]
        z = jnp.sum(jnp.where(cand >= tau, jnp.exp(cand - m_tot), 0.0), axis=0, keepdims=True)
        e2 = jnp.exp(s2 - a2[0]) / z
        e1 = jnp.exp(s1 - a1[0])
        th = tau - s1
        for cc in range(nchunk):
            ls = slice(cc * LANES, (cc + 1) * LANES)
            s2_ref[cc, h] = s2[:, ls]
            e2_ref[cc, h] = e2[:, ls]
            th_ref[cc, h] = th[:, ls]
            e1_ref[cc, h] = e1[:, ls]


def _post(x, mix, w_out, ffn_w, w_q, sub_keys, tm=256):
    t, d = x.shape
    tm = min(tm, t)
    nchunk = tm // LANES
    stat = jax.ShapeDtypeStruct((t // LANES, PEER_HEADS, PEER_N_KEYS, LANES), F32)
    stat_spec = pl.BlockSpec((nchunk, PEER_HEADS, PEER_N_KEYS, LANES), lambda i: (i, 0, 0, 0))
    kern = functools.partial(_post_kernel, tm=tm)
    return pl.pallas_call(
        kern,
        grid=(t // tm,),
        in_specs=[
            pl.BlockSpec((tm, d), lambda i: (i, 0)),
            pl.BlockSpec((tm, d), lambda i: (i, 0)),
            pl.BlockSpec((d, d), lambda i: (0, 0)),
            pl.BlockSpec((1, d), lambda i: (0, 0)),
            pl.BlockSpec((d, d), lambda i: (0, 0)),
            pl.BlockSpec((2 * PEER_HEADS, PEER_N_KEYS, LANES), lambda i: (0, 0, 0)),
        ],
        out_specs=[
            pl.BlockSpec((tm, d), lambda i: (i, 0)),
            pl.BlockSpec((d, tm), lambda i: (0, i)),
            stat_spec, stat_spec, stat_spec, stat_spec,
        ],
        out_shape=[
            jax.ShapeDtypeStruct((t, d), F32),
            jax.ShapeDtypeStruct((d, t), BF16),
            stat, stat, stat, stat,
        ],
        scratch_shapes=[pltpu.VMEM((_CAND_ROWS, tm), F32)],
        compiler_params=_params(1),
        name="post",
    )(x, mix, w_out, ffn_w, w_q, sub_keys)


_ROW_TILE = 32


def _gelu(x):
    return 0.5 * x * (1.0 + lax.erf(x * (2.0 ** -0.5)))


def _peer_kernel(hnt_ref, u0_ref, u1_ref, vt0_ref, vtp_ref, vtl_ref, s2_ref, e2_ref, th_ref, e1_ref, acc_ref,
                 at0_ref, at1_ref, pt0_ref, pt1_ref, *, tb, eb):
    n1 = eb // PEER_N_KEYS
    nchunk = tb // LANES
    nrt = PEER_N_KEYS // _ROW_TILE
    d = acc_ref.shape[0]

    @pl.when(pl.program_id(1) == 0)
    def _():
        acc_ref[...] = jnp.zeros_like(acc_ref)
        pt1_ref[...] = jnp.zeros_like(pt1_ref)

    hnt = hnt_ref[...]
    at0_ref[...] = _dot(u0_ref[...], hnt)
    at1_ref[...] = _dot(u1_ref[...], hnt)

    def tile(at_ref, pt_ref, row0, cc, rt):
        ls = slice(cc * LANES, (cc + 1) * LANES)
        rs = pl.ds(pl.multiple_of(rt * _ROW_TILE, _ROW_TILE), _ROW_TILE)
        w = [jnp.zeros((_ROW_TILE, LANES), F32) for _ in range(n1)]
        for h in range(PEER_HEADS):
            s2 = s2_ref[cc, h, rs, :]
            e2 = e2_ref[cc, h, rs, :]
            for i in range(n1):
                th = th_ref[cc, h, row0 + i:row0 + i + 1, :]
                e1 = e1_ref[cc, h, row0 + i:row0 + i + 1, :]
                w[i] = w[i] + jnp.where(s2 >= th, e2, 0.0) * e1
        for i in range(n1):
            r = pl.ds(pl.multiple_of(i * PEER_N_KEYS + rt * _ROW_TILE, _ROW_TILE), _ROW_TILE)
            pt_ref[r, ls] = (w[i] * _gelu(at_ref[r, ls])).astype(BF16)

    c_rows = d // nrt

    def body0(rt, carry):
        for cc in range(nchunk):
            tile(at0_ref, pt0_ref, 0, cc, rt)
        r = pl.ds(pl.multiple_of(rt * c_rows, c_rows), c_rows)
        acc_ref[r, :] += _dot(vtp_ref[r, :], pt1_ref[...])
        return carry
    lax.fori_loop(0, nrt, body0, 0)

    def body1(rt, carry):
        for cc in range(nchunk):
            tile(at1_ref, pt1_ref, n1, cc, rt)
        r = pl.ds(pl.multiple_of(rt * c_rows, c_rows), c_rows)
        acc_ref[r, :] += _dot(vt0_ref[r, :], pt0_ref[...])
        return carry
    lax.fori_loop(0, nrt, body1, 0)

    @pl.when(pl.program_id(1) == pl.num_programs(1) - 1)
    def _():
        acc_ref[...] += _dot(vtl_ref[...], pt1_ref[...])


def _peer(hnt, u16, v16, s2, e2, th, e1, tb=512, eb=512):
    d, t = hnt.shape
    tb = min(tb, t)
    nchunk = tb // LANES
    n1 = eb // PEER_N_KEYS
    nk = PEER_N_EXPERTS // (2 * eb)
    vt_slabs = jnp.swapaxes(v16.reshape(2 * nk, eb, d), 1, 2)
    assert 2 * n1 == SUBLANES, "a pair of expert blocks spans one sublane group of first-key statistics"
    stat_spec = pl.BlockSpec((nchunk, PEER_HEADS, PEER_N_KEYS, LANES), lambda i, k: (i, 0, 0, 0))
    row_spec = pl.BlockSpec((nchunk, PEER_HEADS, SUBLANES, LANES), lambda i, k: (i, 0, k, 0))
    kern = functools.partial(_peer_kernel, tb=tb, eb=eb)
    return pl.pallas_call(
        kern,
        grid=(t // tb, nk),
        in_specs=[
            pl.BlockSpec((d, tb), lambda i, k: (0, i)),
            pl.BlockSpec((eb, d), lambda i, k: (2 * k, 0)),
            pl.BlockSpec((eb, d), lambda i, k: (2 * k + 1, 0)),
            pl.BlockSpec((None, d, eb), lambda i, k: (2 * k, 0, 0)),
            pl.BlockSpec((None, d, eb), lambda i, k: (2 * jnp.maximum(k - 1, 0) + 1, 0, 0)),
            pl.BlockSpec((None, d, eb), lambda i, k: (2 * nk - 1, 0, 0)),
            stat_spec, stat_spec, row_spec, row_spec,
        ],
        out_specs=pl.BlockSpec((d, tb), lambda i, k: (0, i)),
        out_shape=jax.ShapeDtypeStruct((d, t), F32),
        scratch_shapes=[pltpu.VMEM((eb, tb), F32), pltpu.VMEM((eb, tb), F32),
                        pltpu.VMEM((eb, tb), BF16), pltpu.VMEM((eb, tb), BF16)],
        compiler_params=_params(2),
        name="peer",
    )(hnt, u16, u16, vt_slabs, vt_slabs, vt_slabs, s2, e2, th, e1)


def _final_kernel(h_ref, pt_ref, w_ref, o_ref):
    hf = h_ref[...] + pt_ref[...].T
    o_ref[...] = hf * lax.rsqrt(jnp.mean(hf * hf, axis=-1, keepdims=True) + EPS) * w_ref[...]


def _final(h1, peer_out_t, w_row, tm=512):
    t, d = h1.shape
    tm = min(tm, t)
    spec = pl.BlockSpec((tm, d), lambda i: (i, 0))
    return pl.pallas_call(
        _final_kernel,
        grid=(t // tm,),
        in_specs=[spec, pl.BlockSpec((d, tm), lambda i: (0, i)), pl.BlockSpec((1, d), lambda i: (0, 0))],
        out_specs=spec,
        out_shape=jax.ShapeDtypeStruct((t, d), F32),
        compiler_params=_params(1),
        name="final",
    )(h1, peer_out_t, w_row)


def _row(v, width=None):
    v = v.astype(F32).reshape(1, -1)
    if width is not None and v.shape[1] < width:
        v = jnp.pad(v, ((0, 0), (0, width - v.shape[1])))
    return v


def _layer(h, mix_norm_w, w_in, gate_b, ssm_conv_w, ssm_conv_b, ssm_dt_bias, ssm_a_log, ssm_d,
           ssm_norm_w, gdn_conv_w, gdn_dt_bias, gdn_a_log, gdn_norm_w, w_branch_ssm, w_branch_gdn,
           w_out, ffn_norm_w, peer_w_q, peer_sub_keys, peer_u, peer_v, out_norm_w):
    ssm_proj = SSM_D_INNER + (SSM_D_INNER + 2 * SSM_N_GROUPS * SSM_D_STATE) + SSM_N_HEADS
    gdn_conv_dim = 2 * GDN_KEY_DIM + GDN_VAL_DIM
    o_dt = ssm_proj - SSM_N_HEADS
    o_qkv = ssm_proj
    o_gz = o_qkv + gdn_conv_dim
    o_beta = o_gz + GDN_VAL_DIM
    o_gate = o_beta + 2 * GDN_N_HEADS
    w_main = jnp.concatenate(
        [w_in[:, :o_dt], w_in[:, o_qkv:o_beta], w_in[:, o_gate:]], axis=1).astype(BF16)
    w_small = jnp.concatenate(
        [w_in[:, o_dt:o_qkv], w_in[:, o_beta:o_gate],
         jnp.zeros((D_MODEL, LANES - SSM_N_HEADS - 2 * GDN_N_HEADS), w_in.dtype)], axis=1).astype(BF16)

    proj, small = _inproj(h, _row(mix_norm_w), w_main, w_small)

    cw = ssm_conv_w.astype(F32)
    cbias = _row(ssm_conv_b)
    nx = SSM_D_INNER
    nb = SSM_N_GROUPS * SSM_D_STATE
    y_ssm = _ssd(proj, small,
                 cw[:, :nx], cw[:, nx:nx + nb], cw[:, nx + nb:],
                 cbias[:, :nx], cbias[:, nx:nx + nb], cbias[:, nx + nb:],
                 _row(ssm_dt_bias, LANES), _row(ssm_a_log, LANES),
                 _row(jnp.repeat(ssm_d, SSM_HEAD_DIM)), _row(ssm_norm_w))

    gw = gdn_conv_w.astype(F32)
    zeros_b = jnp.zeros((SMALL_ALPHA,), F32)
    y_gdn = _gdn(proj, small,
                 gw[:, :GDN_KEY_DIM], gw[:, GDN_KEY_DIM:2 * GDN_KEY_DIM], gw[:, 2 * GDN_KEY_DIM:],
                 _row(jnp.concatenate([zeros_b, gdn_dt_bias.astype(F32)]), LANES),
                 _row(jnp.concatenate([zeros_b, gdn_a_log.astype(F32)]), LANES),
                 _row(gdn_norm_w))

    mix = _mix(y_ssm, y_gdn, proj, gate_b.astype(F32),
               w_branch_ssm.astype(BF16), w_branch_gdn.astype(BF16))

    sk = peer_sub_keys.reshape(2 * PEER_HEADS, PEER_N_KEYS, LANES).astype(BF16)
    h1, hnt, s2, e2, th, e1 = _post(h, mix, w_out.astype(BF16), _row(ffn_norm_w),
                                    peer_w_q.astype(BF16), sk)

    peer_out_t = _peer(hnt, peer_u.astype(BF16), peer_v.astype(BF16), s2, e2, th, e1)
    return _final(h1, peer_out_t, _row(out_norm_w))


def kernel(x, mix_norm_w, w_in, gate_b, ssm_conv_w, ssm_conv_b, ssm_dt_bias, ssm_a_log, ssm_d,
           ssm_norm_w, gdn_conv_w, gdn_dt_bias, gdn_a_log, gdn_norm_w, w_branch_ssm, w_branch_gdn,
           w_out, ffn_norm_w, peer_w_q, peer_sub_keys, peer_u, peer_v, final_norm_w):
    b, t, d = x.shape
    assert b == 1 and mix_norm_w.shape[0] == 1, "single sequence, single layer"
    out = _layer(x[0], mix_norm_w[0], w_in[0], gate_b[0], ssm_conv_w[0], ssm_conv_b[0],
                 ssm_dt_bias[0], ssm_a_log[0], ssm_d[0], ssm_norm_w[0], gdn_conv_w[0],
                 gdn_dt_bias[0], gdn_a_log[0], gdn_norm_w[0], w_branch_ssm[0], w_branch_gdn[0],
                 w_out[0], ffn_norm_w[0], peer_w_q[0], peer_sub_keys[0], peer_u[0], peer_v[0],
                 final_norm_w)
    return out[None]
```

```python
import functools

import jax
import jax.numpy as jnp
from jax import lax
from jax.experimental import pallas as pl
from jax.experimental.pallas import tpu as pltpu

F32 = jnp.float32
BF16 = jnp.bfloat16
HIGHEST = lax.Precision.HIGHEST

EPS = 1e-6
D_MODEL = 2048
LANES = 128
SUBLANES = 8
SSM_D_INNER = 4096
SSM_HEAD_DIM = 64
SSM_N_HEADS = 64
SSM_N_GROUPS = 8
SSM_HPG = 8
SSM_D_STATE = 128
SSM_CHUNK = 128
SSM_GROUP_W = SSM_HPG * SSM_HEAD_DIM
GDN_N_HEADS = 16
GDN_HEAD_K = 128
GDN_HEAD_V = 256
GDN_CHUNK = 64
GDN_KEY_DIM = GDN_N_HEADS * GDN_HEAD_K
GDN_VAL_DIM = GDN_N_HEADS * GDN_HEAD_V
CONV_K = 4
PEER_HEADS = 8
PEER_N_KEYS = 128
PEER_TOPK = 16
PEER_N_EXPERTS = PEER_N_KEYS * PEER_N_KEYS

COL_SSM_Z = 0
COL_SSM_XBC = 4096
COL_GDN_QKV = 10240
COL_GDN_Z = 18432
COL_GATE = 22528
N_MAIN = 26624
SMALL_BETA = 64
SMALL_ALPHA = 80

VMEM_LIMIT = 56 * 1024 * 1024


def _softplus(x):
    return jnp.maximum(x, 0.0) + jnp.log1p(jnp.exp(-jnp.abs(x)))


def _silu(x):
    return x * jax.nn.sigmoid(x)


def _dot(a, b, **kw):
    return jnp.dot(a, b, preferred_element_type=F32, **kw)


def _dot_nt(a, b, **kw):
    return lax.dot_general(a, b, (((1,), (1,)), ((), ())), preferred_element_type=F32, **kw)


def _dot_tn(a, b, **kw):
    return lax.dot_general(a, b, (((0,), (0,)), ((), ())), preferred_element_type=F32, **kw)


def _params(n_grid):
    return pltpu.CompilerParams(dimension_semantics=("arbitrary",) * n_grid,
                                vmem_limit_bytes=VMEM_LIMIT)


def _inproj_kernel(x_ref, nw_ref, w_ref, ws_ref, o_ref, os_ref, xn_ref):
    @pl.when(pl.program_id(1) == 0)
    def _():
        x = x_ref[...]
        ms = jnp.mean(x * x, axis=-1, keepdims=True)
        xn = (x * lax.rsqrt(ms + EPS) * nw_ref[...]).astype(BF16)
        xn_ref[...] = xn
        os_ref[...] = _dot(xn, ws_ref[...])

    o_ref[...] = _dot(xn_ref[...], w_ref[...]).astype(BF16)


def _inproj(x, norm_w, w_main, w_small, tm=1024, tn=1024):
    t, d = x.shape
    n = w_main.shape[1]
    tm = min(tm, t)
    return pl.pallas_call(
        _inproj_kernel,
        grid=(t // tm, n // tn),
        in_specs=[
            pl.BlockSpec((tm, d), lambda i, j: (i, 0)),
            pl.BlockSpec((1, d), lambda i, j: (0, 0)),
            pl.BlockSpec((d, tn), lambda i, j: (0, j)),
            pl.BlockSpec((d, LANES), lambda i, j: (0, 0)),
        ],
        out_specs=[
            pl.BlockSpec((tm, tn), lambda i, j: (i, j)),
            pl.BlockSpec((tm, LANES), lambda i, j: (i, 0)),
        ],
        out_shape=[jax.ShapeDtypeStruct((t, n), BF16), jax.ShapeDtypeStruct((t, LANES), F32)],
        scratch_shapes=[pltpu.VMEM((tm, d), BF16)],
        compiler_params=_params(2),
        name="inproj",
    )(x, norm_w, w_main, w_small)


def _shift_mats(tc):
    return jnp.concatenate([jnp.eye(tc, k=-(CONV_K - 1 - k), dtype=BF16) for k in range(CONV_K - 1)], axis=0)


def _conv_block(raw_ref, carry_ref, w_ref, shift_ref, tc):
    x16 = raw_ref[...]
    xf = x16.astype(F32)
    y = xf * w_ref[CONV_K - 1:CONV_K, :]
    c8 = carry_ref[...]
    row8 = lax.broadcasted_iota(jnp.int32, c8.shape, 0)
    head = jnp.zeros_like(c8)
    shifted = _dot(shift_ref[...], x16)
    for k in range(CONV_K - 1):
        lag = CONV_K - 1 - k
        y = y + shifted[k * tc:(k + 1) * tc] * w_ref[k:k + 1, :]
        head = head + jnp.where(row8 < lag, pltpu.roll(c8, shift=lag, axis=0), 0.0) * w_ref[k:k + 1, :]
    carry_ref[...] = xf[tc - SUBLANES:tc]
    return jnp.concatenate([y[0:SUBLANES] + head, y[SUBLANES:]], axis=0)


def _ssd_kernel(xs_ref, b_ref, c_ref, z_ref, sm_ref, cwx_ref, cwb_ref, cwc_ref, cbx_ref, cbb_ref,
                cbc_ref, dtb_ref, alog_ref, dsk_ref, nw_ref, sh_ref, o_ref,
                xbuf, bbuf, cbuf, state_ref, acst_ref, *, tc, gps):
    g0 = pl.program_id(0) * gps
    L, GW, NS = SSM_CHUNK, SSM_GROUP_W, SSM_D_STATE

    @pl.when(pl.program_id(1) == 0)
    def _():
        xbuf[...] = jnp.zeros_like(xbuf)
        bbuf[...] = jnp.zeros_like(bbuf)
        cbuf[...] = jnp.zeros_like(cbuf)
        state_ref[...] = jnp.zeros_like(state_ref)

    xs_all = _silu(_conv_block(xs_ref, xbuf, cwx_ref, sh_ref, tc) + cbx_ref[...])
    b_all = _silu(_conv_block(b_ref, bbuf, cwb_ref, sh_ref, tc) + cbb_ref[...])
    c_all = _silu(_conv_block(c_ref, cbuf, cwc_ref, sh_ref, tc) + cbc_ref[...])

    row = lax.broadcasted_iota(jnp.int32, (L, L), 0)
    col = lax.broadcasted_iota(jnp.int32, (L, L), 1)
    causal = row >= col
    tri = causal.astype(F32)
    lo_half = col < SSM_HEAD_DIM
    a_row = -jnp.exp(alog_ref[...])

    for c in range(tc // L):
        sl = slice(c * L, (c + 1) * L)
        dt_full = _softplus(sm_ref[sl, :] + dtb_ref[...])
        acs_full = _dot(tri, dt_full * a_row, precision=HIGHEST)
        acst_ref[c] = acs_full.T
        for gg in range(gps):
            xs = xs_all[sl, gg * GW:(gg + 1) * GW]
            bm16 = b_all[sl, gg * NS:(gg + 1) * NS].astype(BF16)
            cm16 = c_all[sl, gg * NS:(gg + 1) * NS].astype(BF16)
            cb = _dot_nt(cm16, bm16)
            ys = []
            for p in range(SSM_HPG // 2):
                acs_bc, dt_bc, scores = [], [], []
                for r in (2 * p, 2 * p + 1):
                    j = (g0 + gg) * SSM_HPG + r
                    sel = col == j
                    a_c = jnp.sum(jnp.where(sel, acs_full, 0.0), axis=1, keepdims=True)
                    d_c = jnp.sum(jnp.where(sel, dt_full, 0.0), axis=1, keepdims=True)
                    a_b = jnp.broadcast_to(a_c, (L, L))
                    acs_bc.append(a_b)
                    dt_bc.append(jnp.broadcast_to(d_c, (L, L)))
                    a_r = acst_ref[c, pl.ds(j, 1), :]
                    seg = jnp.where(causal, a_b - a_r, -jnp.inf)
                    scores.append((cb * jnp.exp(seg)).astype(BF16))
                acs_e = jnp.where(lo_half, acs_bc[0], acs_bc[1])
                dt_e = jnp.where(lo_half, dt_bc[0], dt_bc[1])
                x_p = xs[:, p * LANES:(p + 1) * LANES]
                lanes = slice(gg * GW + p * LANES, gg * GW + (p + 1) * LANES)
                xdt = x_p * dt_e
                last = acs_e[L - 1:L, :]
                xdtw = (xdt * jnp.exp(last - acs_e)).astype(BF16)
                s2 = jnp.concatenate(scores, axis=1)
                x2 = jnp.concatenate([jnp.where(lo_half, xdt, 0.0), jnp.where(lo_half, 0.0, xdt)],
                                     axis=0).astype(BF16)
                y_diag = _dot(s2, x2)
                st = state_ref[:, lanes]
                y_off = _dot(cm16, st.astype(BF16)) * jnp.exp(acs_e)
                state_ref[:, lanes] = st * jnp.exp(last) + _dot_tn(bm16, xdtw)
                ys.append(y_diag + y_off + x_p * dsk_ref[:, lanes])
            gl = slice(gg * GW, (gg + 1) * GW)
            y = jnp.concatenate(ys, axis=1) * _silu(z_ref[sl, gl].astype(F32))
            y = y * lax.rsqrt(jnp.mean(y * y, axis=-1, keepdims=True) + EPS)
            o_ref[sl, gl] = (y * nw_ref[:, gl]).astype(BF16)


def _ssd(proj, small, cwx, cwb, cwc, cbx, cbb, cbc, dtb_row, alog_row, dsk_row, nw_row, tc=256, gps=4):
    t = proj.shape[0]
    tc = min(tc, t)
    gw, ns = gps * SSM_GROUP_W, gps * SSM_D_STATE
    xs_blk = COL_SSM_XBC // gw
    b_blk = (COL_SSM_XBC + SSM_D_INNER) // ns
    c_blk = b_blk + SSM_N_GROUPS // gps
    z_blk = COL_SSM_Z // gw
    kern = functools.partial(_ssd_kernel, tc=tc, gps=gps)
    return pl.pallas_call(
        kern,
        grid=(SSM_N_GROUPS // gps, t // tc),
        in_specs=[
            pl.BlockSpec((tc, gw), lambda g, i: (i, xs_blk + g)),
            pl.BlockSpec((tc, ns), lambda g, i: (i, b_blk + g)),
            pl.BlockSpec((tc, ns), lambda g, i: (i, c_blk + g)),
            pl.BlockSpec((tc, gw), lambda g, i: (i, z_blk + g)),
            pl.BlockSpec((tc, LANES), lambda g, i: (i, 0)),
            pl.BlockSpec((CONV_K, gw), lambda g, i: (0, g)),
            pl.BlockSpec((CONV_K, ns), lambda g, i: (0, g)),
            pl.BlockSpec((CONV_K, ns), lambda g, i: (0, g)),
            pl.BlockSpec((1, gw), lambda g, i: (0, g)),
            pl.BlockSpec((1, ns), lambda g, i: (0, g)),
            pl.BlockSpec((1, ns), lambda g, i: (0, g)),
            pl.BlockSpec((1, LANES), lambda g, i: (0, 0)),
            pl.BlockSpec((1, LANES), lambda g, i: (0, 0)),
            pl.BlockSpec((1, gw), lambda g, i: (0, g)),
            pl.BlockSpec((1, gw), lambda g, i: (0, g)),
            pl.BlockSpec(((CONV_K - 1) * tc, tc), lambda g, i: (0, 0)),
        ],
        out_specs=pl.BlockSpec((tc, gw), lambda g, i: (i, g)),
        out_shape=jax.ShapeDtypeStruct((t, SSM_D_INNER), BF16),
        scratch_shapes=[
            pltpu.VMEM((SUBLANES, gw), F32),
            pltpu.VMEM((SUBLANES, ns), F32),
            pltpu.VMEM((SUBLANES, ns), F32),
            pltpu.VMEM((SSM_D_STATE, gw), F32),
            pltpu.VMEM((tc // SSM_CHUNK, LANES, SSM_CHUNK), F32),
        ],
        compiler_params=_params(2),
        name="ssd",
    )(proj, proj, proj, proj, small, cwx, cwb, cwc, cbx, cbb, cbc, dtb_row, alog_row, dsk_row, nw_row,
      _shift_mats(tc))


def _gdn_kernel(q_ref, k_ref, v_ref, z_ref, sm_ref, cwq_ref, cwk_ref, cwv_ref, dtb_ref, alog_ref,
                nw_ref, sh_ref, o_ref, qbuf, kbuf, vbuf, s_ref, *, tc, hps):
    h0 = pl.program_id(0) * hps
    C, DK, DV = GDN_CHUNK, GDN_HEAD_K, GDN_HEAD_V

    @pl.when(pl.program_id(1) == 0)
    def _():
        qbuf[...] = jnp.zeros_like(qbuf)
        kbuf[...] = jnp.zeros_like(kbuf)
        vbuf[...] = jnp.zeros_like(vbuf)
        s_ref[...] = jnp.zeros_like(s_ref)

    q_cv = _silu(_conv_block(q_ref, qbuf, cwq_ref, sh_ref, tc))
    k_cv = _silu(_conv_block(k_ref, kbuf, cwk_ref, sh_ref, tc))
    v_cv = _silu(_conv_block(v_ref, vbuf, cwv_ref, sh_ref, tc))

    sm = sm_ref[...]
    lane = lax.broadcasted_iota(jnp.int32, sm.shape, 1)
    beta_full = jax.nn.sigmoid(sm)
    g_full = -jnp.exp(alog_ref[...]) * _softplus(sm + dtb_ref[...])
    q_hd, k_hd, v_hd, beta_hd, g_hd = [], [], [], [], []
    for hd in range(hps):
        q = q_cv[:, hd * DK:(hd + 1) * DK]
        k = k_cv[:, hd * DK:(hd + 1) * DK]
        q_hd.append(q * lax.rsqrt(jnp.sum(q * q, axis=-1, keepdims=True) + EPS) * (DK ** -0.5))
        k_hd.append(k * lax.rsqrt(jnp.sum(k * k, axis=-1, keepdims=True) + EPS))
        v_hd.append(v_cv[:, hd * DV:(hd + 1) * DV])
        beta_hd.append(jnp.sum(jnp.where(lane == SMALL_BETA + h0 + hd, beta_full, 0.0),
                               axis=1, keepdims=True))
        g_hd.append(jnp.sum(jnp.where(lane == SMALL_ALPHA + h0 + hd, g_full, 0.0),
                            axis=1, keepdims=True))

    row = lax.broadcasted_iota(jnp.int32, (C, C), 0)
    col = lax.broadcasted_iota(jnp.int32, (C, C), 1)
    causal = row >= col
    strict = row > col
    tri = causal.astype(F32)
    eye = (row == col).astype(F32)

    nc = tc // C
    items = [(hd, slice(c * C, (c + 1) * C)) for hd in range(hps) for c in range(nc)]
    rng = range(len(items))
    strict_w = (lax.broadcasted_iota(jnp.int32, (C, LANES), 0)
                > lax.broadcasted_iota(jnp.int32, (C, LANES), 1))
    qs = [q_hd[hd][sl] for hd, sl in items]
    ks = [k_hd[hd][sl] for hd, sl in items]
    vs = [v_hd[hd][sl] for hd, sl in items]
    betas = [beta_hd[hd][sl] for hd, sl in items]
    g_bc = [jnp.broadcast_to(g_hd[hd][sl], (C, LANES)) for hd, sl in items]
    cums = [_dot(tri, jnp.concatenate([g_bc[c], jnp.where(strict_w, g_bc[c], 0.0)], axis=1),
                 precision=HIGHEST) for c in rng]
    gc_bc = [x[:, :LANES] for x in cums]
    decay = [jnp.exp(jnp.where(causal, x[:, LANES:LANES + C], -jnp.inf)) for x in cums]
    k16 = [k.astype(BF16) for k in ks]
    q16 = [q.astype(BF16) for q in qs]
    kk = [_dot_nt(k16[c], k16[c]) for c in rng]
    qk = [_dot_nt(q16[c], k16[c]) for c in rng]
    p = [-jnp.where(strict, kk[c] * betas[c] * decay[c], 0.0) for c in rng]
    x = [eye + p[c] for c in rng]
    for _ in range(C.bit_length() - 2):
        p16 = [pp.astype(BF16) for pp in p]
        p = [_dot(p16[c], p16[c]) for c in rng]
        x = [x[c] + _dot(x[c].astype(BF16), p[c].astype(BF16)) for c in rng]
    t_inv = [xx.astype(BF16) for xx in x]
    egc = [jnp.exp(gc) for gc in gc_bc]
    u = [_dot(t_inv[c], (vs[c] * betas[c]).astype(BF16)).astype(BF16) for c in rng]
    w = [_dot(t_inv[c], (ks[c] * (betas[c] * egc[c])).astype(BF16)).astype(BF16) for c in rng]
    attn = [jnp.where(causal, qk[c] * decay[c], 0.0).astype(BF16) for c in rng]
    gc_last = [gc[C - 1:C, :] for gc in gc_bc]
    k_dec = [(ks[c] * jnp.exp(gc_last[c] - gc_bc[c])).astype(BF16) for c in rng]
    c_dec = [jnp.exp(gl[:, :1]) for gl in gc_last]
    a_mat = [_dot_tn(k_dec[c], w[c]).astype(BF16) for c in rng]
    n_mat = [_dot_tn(k_dec[c], u[c]) for c in rng]
    q_eff = [(qs[c] * egc[c] - _dot(attn[c], w[c])).astype(BF16) for c in rng]
    o_u = [_dot(attn[c], u[c]) for c in rng]

    s = [s_ref[hd] for hd in range(hps)]
    for c in range(nc):
        sl = slice(c * C, (c + 1) * C)
        for hd in range(hps):
            it = hd * nc + c
            s16 = s[hd].astype(BF16)
            o = _dot(q_eff[it], s16) + o_u[it]
            s[hd] = s[hd] * c_dec[it] - _dot(a_mat[it], s16) + n_mat[it]
            o = o * lax.rsqrt(jnp.mean(o * o, axis=-1, keepdims=True) + EPS) * nw_ref[...]
            z = z_ref[sl, hd * DV:(hd + 1) * DV].astype(F32)
            o_ref[sl, hd * DV:(hd + 1) * DV] = (o * _silu(z)).astype(BF16)
    for hd in range(hps):
        s_ref[hd] = s[hd]


def _gdn(proj, small, cwq, cwk, cwv, dtb_row, alog_row, nw_row, tc=256, hps=8):
    t = proj.shape[0]
    tc = min(tc, t)
    dk, dv = hps * GDN_HEAD_K, hps * GDN_HEAD_V
    q_blk = COL_GDN_QKV // dk
    k_blk = q_blk + GDN_N_HEADS // hps
    v_blk = (COL_GDN_QKV + 2 * GDN_KEY_DIM) // dv
    z_blk = COL_GDN_Z // dv
    kern = functools.partial(_gdn_kernel, tc=tc, hps=hps)
    return pl.pallas_call(
        kern,
        grid=(GDN_N_HEADS // hps, t // tc),
        in_specs=[
            pl.BlockSpec((tc, dk), lambda h, i: (i, q_blk + h)),
            pl.BlockSpec((tc, dk), lambda h, i: (i, k_blk + h)),
            pl.BlockSpec((tc, dv), lambda h, i: (i, v_blk + h)),
            pl.BlockSpec((tc, dv), lambda h, i: (i, z_blk + h)),
            pl.BlockSpec((tc, LANES), lambda h, i: (i, 0)),
            pl.BlockSpec((CONV_K, dk), lambda h, i: (0, h)),
            pl.BlockSpec((CONV_K, dk), lambda h, i: (0, h)),
            pl.BlockSpec((CONV_K, dv), lambda h, i: (0, h)),
            pl.BlockSpec((1, LANES), lambda h, i: (0, 0)),
            pl.BlockSpec((1, LANES), lambda h, i: (0, 0)),
            pl.BlockSpec((1, GDN_HEAD_V), lambda h, i: (0, 0)),
            pl.BlockSpec(((CONV_K - 1) * tc, tc), lambda h, i: (0, 0)),
        ],
        out_specs=pl.BlockSpec((tc, dv), lambda h, i: (i, h)),
        out_shape=jax.ShapeDtypeStruct((t, GDN_VAL_DIM), BF16),
        scratch_shapes=[
            pltpu.VMEM((SUBLANES, dk), F32),
            pltpu.VMEM((SUBLANES, dk), F32),
            pltpu.VMEM((SUBLANES, dv), F32),
            pltpu.VMEM((hps, GDN_HEAD_K, GDN_HEAD_V), F32),
        ],
        compiler_params=_params(2),
        name="gdn",
    )(proj, proj, proj, proj, small, cwq, cwk, cwv, dtb_row, alog_row, nw_row, _shift_mats(tc))


def _mix_kernel(ys_ref, yg_ref, g0_ref, g1_ref, gb_ref, ws_ref, wg_ref, o_ref):
    a = _dot(ys_ref[...], ws_ref[...])
    b = _dot(yg_ref[...], wg_ref[...])
    g0 = jax.nn.sigmoid(g0_ref[...].astype(F32) + gb_ref[0:1, :])
    g1 = jax.nn.sigmoid(g1_ref[...].astype(F32) + gb_ref[1:2, :])
    o_ref[...] = (g0 * a + g1 * b).astype(BF16)


def _mix(y_ssm, y_gdn, proj, gate_b, w_s, w_g, tm=512, tn=512):
    t = y_ssm.shape[0]
    tm = min(tm, t)
    g0_blk = COL_GATE // tn
    g1_blk = (COL_GATE + D_MODEL) // tn
    return pl.pallas_call(
        _mix_kernel,
        grid=(D_MODEL // tn, t // tm),
        in_specs=[
            pl.BlockSpec((tm, SSM_D_INNER), lambda n, i: (i, 0)),
            pl.BlockSpec((tm, GDN_VAL_DIM), lambda n, i: (i, 0)),
            pl.BlockSpec((tm, tn), lambda n, i: (i, g0_blk + n)),
            pl.BlockSpec((tm, tn), lambda n, i: (i, g1_blk + n)),
            pl.BlockSpec((2, tn), lambda n, i: (0, n)),
            pl.BlockSpec((SSM_D_INNER, tn), lambda n, i: (0, n)),
            pl.BlockSpec((GDN_VAL_DIM, tn), lambda n, i: (0, n)),
        ],
        out_specs=pl.BlockSpec((tm, tn), lambda n, i: (i, n)),
        out_shape=jax.ShapeDtypeStruct((t, D_MODEL), BF16),
        compiler_params=_params(2),
        name="mix",
    )(y_ssm, y_gdn, proj, proj, gate_b, w_s, w_g)


_N_RANK = PEER_TOPK + 1
_CAND_PAIRS = [(i, j) for i in range(_N_RANK) for j in range(_N_RANK) if (i + 1) * (j + 1) <= _N_RANK]
_CAND_ROWS = SUBLANES * (1 << (-(-len(_CAND_PAIRS) // SUBLANES) - 1).bit_length())


def _top_desc(s, n):
    nv = s.shape[0] // SUBLANES
    v = [s[j * SUBLANES:(j + 1) * SUBLANES, :] for j in range(nv)]
    k = 2
    while k <= nv:
        j = k // 2
        while j >= 1:
            for i in range(nv):
                l = i ^ j
                if l > i:
                    hi, lo = jnp.maximum(v[i], v[l]), jnp.minimum(v[i], v[l])
                    v[i], v[l] = (hi, lo) if (i & k) == 0 else (lo, hi)
            j //= 2
        k *= 2
    out = []
    for t in range(n):
        m = jnp.max(v[0], axis=0, keepdims=True)
        out.append(m)
        pop = v[0] == m
        live = min(nv, n - t)
        for j in range(live - 1):
            v[j] = jnp.where(pop, v[j + 1], v[j])
        v[live - 1] = jnp.where(pop, -jnp.inf, v[live - 1])
    return out


def _post_kernel(x_ref, mix_ref, wo_ref, fw_ref, wq_ref, sk_ref,
                 h1_ref, hnt_ref, s2_ref, e2_ref, th_ref, e1_ref, cand_ref, *, tm):
    h1 = x_ref[...] + _dot(mix_ref[...], wo_ref[...])
    h1_ref[...] = h1
    hn = h1 * lax.rsqrt(jnp.mean(h1 * h1, axis=-1, keepdims=True) + EPS) * fw_ref[...]
    hnt_ref[...] = hn.T.astype(BF16)
    qv = _dot(hn.astype(BF16), wq_ref[...]).astype(BF16)
    nchunk = tm // LANES
    for h in range(PEER_HEADS):
        s1 = _dot_nt(sk_ref[2 * h], qv[:, (2 * h) * LANES:(2 * h + 1) * LANES])
        s2 = _dot_nt(sk_ref[2 * h + 1], qv[:, (2 * h + 1) * LANES:(2 * h + 2) * LANES])
        a1 = _top_desc(s1, _N_RANK)
        a2 = _top_desc(s2, _N_RANK)
        cand_ref[...] = jnp.full(cand_ref.shape, -jnp.inf, F32)
        for r, (i, j) in enumerate(_CAND_PAIRS):
            cand_ref[pl.ds(r, 1), :] = a1[i] + a2[j]
        cand = cand_ref[...]
        tops = _top_desc(cand, _N_RANK)
        v16, v17 = tops[PEER_TOPK - 1], tops[PEER_TOPK]
        tau = 0.5 * (v16 + v17)
        tau = jnp.where(v17 == -jnp.inf, v16, tau)
        m_tot = a1[0] + a2[0]
        z = jnp.sum(jnp.where(cand >= tau, jnp.exp(cand - m_tot), 0.0), axis=0, keepdims=True)
        e2 = jnp.exp(s2 - a2[0]) / z
        e1 = jnp.exp(s1 - a1[0])
        th = tau - s1
        for cc in range(nchunk):
            ls = slice(cc * LANES, (cc + 1) * LANES)
            s2_ref[cc, h] = s2[:, ls]
            e2_ref[cc, h] = e2[:, ls]
            th_ref[cc, h] = th[:, ls]
            e1_ref[cc, h] = e1[:, ls]


def _post(x, mix, w_out, ffn_w, w_q, sub_keys, tm=256):
    t, d = x.shape
    tm = min(tm, t)
    nchunk = tm // LANES
    stat = jax.ShapeDtypeStruct((t // LANES, PEER_HEADS, PEER_N_KEYS, LANES), F32)
    stat_spec = pl.BlockSpec((nchunk, PEER_HEADS, PEER_N_KEYS, LANES), lambda i: (i, 0, 0, 0))
    kern = functools.partial(_post_kernel, tm=tm)
    return pl.pallas_call(
        kern,
        grid=(t // tm,),
        in_specs=[
            pl.BlockSpec((tm, d), lambda i: (i, 0)),
            pl.BlockSpec((tm, d), lambda i: (i, 0)),
            pl.BlockSpec((d, d), lambda i: (0, 0)),
            pl.BlockSpec((1, d), lambda i: (0, 0)),
            pl.BlockSpec((d, d), lambda i: (0, 0)),
            pl.BlockSpec((2 * PEER_HEADS, PEER_N_KEYS, LANES), lambda i: (0, 0, 0)),
        ],
        out_specs=[
            pl.BlockSpec((tm, d), lambda i: (i, 0)),
            pl.BlockSpec((d, tm), lambda i: (0, i)),
            stat_spec, stat_spec, stat_spec, stat_spec,
        ],
        out_shape=[
            jax.ShapeDtypeStruct((t, d), F32),
            jax.ShapeDtypeStruct((d, t), BF16),
            stat, stat, stat, stat,
        ],
        scratch_shapes=[pltpu.VMEM((_CAND_ROWS, tm), F32)],
        compiler_params=_params(1),
        name="post",
    )(x, mix, w_out, ffn_w, w_q, sub_keys)


_ROW_TILE = 32


def _gelu(x):
    return 0.5 * x * (1.0 + lax.erf(x * (2.0 ** -0.5)))


def _peer_kernel(hnt_ref, u0_ref, u1_ref, vt0_ref, vtp_ref, vtl_ref, s2_ref, e2_ref, th_ref, e1_ref, acc_ref,
                 at0_ref, at1_ref, pt0_ref, pt1_ref, *, tb, eb):
    n1 = eb // PEER_N_KEYS
    nchunk = tb // LANES
    nrt = PEER_N_KEYS // _ROW_TILE
    d = acc_ref.shape[0]

    @pl.when(pl.program_id(1) == 0)
    def _():
        acc_ref[...] = jnp.zeros_like(acc_ref)
        pt1_ref[...] = jnp.zeros_like(pt1_ref)

    hnt = hnt_ref[...]
    at0_ref[...] = _dot(u0_ref[...], hnt)
    at1_ref[...] = _dot(u1_ref[...], hnt)

    def tile(at_ref, pt_ref, row0, cc, rt):
        ls = slice(cc * LANES, (cc + 1) * LANES)
        rs = pl.ds(pl.multiple_of(rt * _ROW_TILE, _ROW_TILE), _ROW_TILE)
        w = [jnp.zeros((_ROW_TILE, LANES), F32) for _ in range(n1)]
        for h in range(PEER_HEADS):
            s2 = s2_ref[cc, h, rs, :]
            e2 = e2_ref[cc, h, rs, :]
            for i in range(n1):
                th = th_ref[cc, h, row0 + i:row0 + i + 1, :]
                e1 = e1_ref[cc, h, row0 + i:row0 + i + 1, :]
                w[i] = w[i] + jnp.where(s2 >= th, e2, 0.0) * e1
        for i in range(n1):
            r = pl.ds(pl.multiple_of(i * PEER_N_KEYS + rt * _ROW_TILE, _ROW_TILE), _ROW_TILE)
            pt_ref[r, ls] = (w[i] * _gelu(at_ref[r, ls])).astype(BF16)

    c_rows = d // nrt

    def body0(rt, carry):
        for cc in range(nchunk):
            tile(at0_ref, pt0_ref, 0, cc, rt)
        r = pl.ds(pl.multiple_of(rt * c_rows, c_rows), c_rows)
        acc_ref[r, :] += _dot(vtp_ref[r, :], pt1_ref[...])
        return carry
    lax.fori_loop(0, nrt, body0, 0)

    def body1(rt, carry):
        for cc in range(nchunk):
            tile(at1_ref, pt1_ref, n1, cc, rt)
        r = pl.ds(pl.multiple_of(rt * c_rows, c_rows), c_rows)
        acc_ref[r, :] += _dot(vt0_ref[r, :], pt0_ref[...])
        return carry
    lax.fori_loop(0, nrt, body1, 0)

    @pl.when(pl.program_id(1) == pl.num_programs(1) - 1)
    def _():
        acc_ref[...] += _dot(vtl_ref[...], pt1_ref[...])


def _peer(hnt, u16, v16, s2, e2, th, e1, tb=512, eb=512):
    d, t = hnt.shape
    tb = min(tb, t)
    nchunk = tb // LANES
    n1 = eb // PEER_N_KEYS
    nk = PEER_N_EXPERTS // (2 * eb)
    vt_slabs = jnp.swapaxes(v16.reshape(2 * nk, eb, d), 1, 2)
    assert 2 * n1 == SUBLANES, "a pair of expert blocks spans one sublane group of first-key statistics"
    stat_spec = pl.BlockSpec((nchunk, PEER_HEADS, PEER_N_KEYS, LANES), lambda i, k: (i, 0, 0, 0))
    row_spec = pl.BlockSpec((nchunk, PEER_HEADS, SUBLANES, LANES), lambda i, k: (i, 0, k, 0))
    kern = functools.partial(_peer_kernel, tb=tb, eb=eb)
    return pl.pallas_call(
        kern,
        grid=(t // tb, nk),
        in_specs=[
            pl.BlockSpec((d, tb), lambda i, k: (0, i)),
            pl.BlockSpec((eb, d), lambda i, k: (2 * k, 0)),
            pl.BlockSpec((eb, d), lambda i, k: (2 * k + 1, 0)),
            pl.BlockSpec((None, d, eb), lambda i, k: (2 * k, 0, 0)),
            pl.BlockSpec((None, d, eb), lambda i, k: (2 * jnp.maximum(k - 1, 0) + 1, 0, 0)),
            pl.BlockSpec((None, d, eb), lambda i, k: (2 * nk - 1, 0, 0)),
            stat_spec, stat_spec, row_spec, row_spec,
        ],
        out_specs=pl.BlockSpec((d, tb), lambda i, k: (0, i)),
        out_shape=jax.ShapeDtypeStruct((d, t), F32),
        scratch_shapes=[pltpu.VMEM((eb, tb), F32), pltpu.VMEM((eb, tb), F32),
                        pltpu.VMEM((eb, tb), BF16), pltpu.VMEM((eb, tb), BF16)],
        compiler_params=_params(2),
        name="peer",
    )(hnt, u16, u16, vt_slabs, vt_slabs, vt_slabs, s2, e2, th, e1)


def _final_kernel(h_ref, pt_ref, w_ref, o_ref):
    hf = h_ref[...] + pt_ref[...].T
    o_ref[...] = hf * lax.rsqrt(jnp.mean(hf * hf, axis=-1, keepdims=True) + EPS) * w_ref[...]


def _final(h1, peer_out_t, w_row, tm=512):
    t, d = h1.shape
    tm = min(tm, t)
    spec = pl.BlockSpec((tm, d), lambda i: (i, 0))
    return pl.pallas_call(
        _final_kernel,
        grid=(t // tm,),
        in_specs=[spec, pl.BlockSpec((d, tm), lambda i: (0, i)), pl.BlockSpec((1, d), lambda i: (0, 0))],
        out_specs=spec,
        out_shape=jax.ShapeDtypeStruct((t, d), F32),
        compiler_params=_params(1),
        name="final",
    )(h1, peer_out_t, w_row)


def _row(v, width=None):
    v = v.astype(F32).reshape(1, -1)
    if width is not None and v.shape[1] < width:
        v = jnp.pad(v, ((0, 0), (0, width - v.shape[1])))
    return v


def _layer(h, mix_norm_w, w_in, gate_b, ssm_conv_w, ssm_conv_b, ssm_dt_bias, ssm_a_log, ssm_d,
           ssm_norm_w, gdn_conv_w, gdn_dt_bias, gdn_a_log, gdn_norm_w, w_branch_ssm, w_branch_gdn,
           w_out, ffn_norm_w, peer_w_q, peer_sub_keys, peer_u, peer_v, out_norm_w):
    ssm_proj = SSM_D_INNER + (SSM_D_INNER + 2 * SSM_N_GROUPS * SSM_D_STATE) + SSM_N_HEADS
    gdn_conv_dim = 2 * GDN_KEY_DIM + GDN_VAL_DIM
    o_dt = ssm_proj - SSM_N_HEADS
    o_qkv = ssm_proj
    o_gz = o_qkv + gdn_conv_dim
    o_beta = o_gz + GDN_VAL_DIM
    o_gate = o_beta + 2 * GDN_N_HEADS
    w_main = jnp.concatenate(
        [w_in[:, :o_dt], w_in[:, o_qkv:o_beta], w_in[:, o_gate:]], axis=1).astype(BF16)
    w_small = jnp.concatenate(
        [w_in[:, o_dt:o_qkv], w_in[:, o_beta:o_gate],
         jnp.zeros((D_MODEL, LANES - SSM_N_HEADS - 2 * GDN_N_HEADS), w_in.dtype)], axis=1).astype(BF16)

    proj, small = _inproj(h, _row(mix_norm_w), w_main, w_small)

    cw = ssm_conv_w.astype(F32)
    cbias = _row(ssm_conv_b)
    nx = SSM_D_INNER
    nb = SSM_N_GROUPS * SSM_D_STATE
    y_ssm = _ssd(proj, small,
                 cw[:, :nx], cw[:, nx:nx + nb], cw[:, nx + nb:],
                 cbias[:, :nx], cbias[:, nx:nx + nb], cbias[:, nx + nb:],
                 _row(ssm_dt_bias, LANES), _row(ssm_a_log, LANES),
                 _row(jnp.repeat(ssm_d, SSM_HEAD_DIM)), _row(ssm_norm_w))

    gw = gdn_conv_w.astype(F32)
    zeros_b = jnp.zeros((SMALL_ALPHA,), F32)
    y_gdn = _gdn(proj, small,
                 gw[:, :GDN_KEY_DIM], gw[:, GDN_KEY_DIM:2 * GDN_KEY_DIM], gw[:, 2 * GDN_KEY_DIM:],
                 _row(jnp.concatenate([zeros_b, gdn_dt_bias.astype(F32)]), LANES),
                 _row(jnp.concatenate([zeros_b, gdn_a_log.astype(F32)]), LANES),
                 _row(gdn_norm_w))

    mix = _mix(y_ssm, y_gdn, proj, gate_b.astype(F32),
               w_branch_ssm.astype(BF16), w_branch_gdn.astype(BF16))

    sk = peer_sub_keys.reshape(2 * PEER_HEADS, PEER_N_KEYS, LANES).astype(BF16)
    h1, hnt, s2, e2, th, e1 = _post(h, mix, w_out.astype(BF16), _row(ffn_norm_w),
                                    peer_w_q.astype(BF16), sk)

    peer_out_t = _peer(hnt, peer_u.astype(BF16), peer_v.astype(BF16), s2, e2, th, e1)
    return _final(h1, peer_out_t, _row(out_norm_w))


def kernel(x, mix_norm_w, w_in, gate_b, ssm_conv_w, ssm_conv_b, ssm_dt_bias, ssm_a_log, ssm_d,
           ssm_norm_w, gdn_conv_w, gdn_dt_bias, gdn_a_log, gdn_norm_w, w_branch_ssm, w_branch_gdn,
           w_out, ffn_norm_w, peer_w_q, peer_sub_keys, peer_u, peer_v, final_norm_w):
    b, t, d = x.shape
    assert b == 1 and mix_norm_w.shape[0] == 1, "single sequence, single layer"
    out = _layer(x[0], mix_norm_w[0], w_in[0], gate_b[0], ssm_conv_w[0], ssm_conv_b[0],
                 ssm_dt_bias[0], ssm_a_log[0], ssm_d[0], ssm_norm_w[0], gdn_conv_w[0],
                 gdn_dt_bias[0], gdn_a_log[0], gdn_norm_w[0], w_branch_ssm[0], w_branch_gdn[0],
                 w_out[0], ffn_norm_w[0], peer_w_q[0], peer_sub_keys[0], peer_u[0], peer_v[0],
                 final_norm_w)
    return out[None]
```

```python
import functools

import jax
import jax.numpy as jnp
from jax import lax
from jax.experimental import pallas as pl
from jax.experimental.pallas import tpu as pltpu

F32 = jnp.float32
BF16 = jnp.bfloat16
HIGHEST = lax.Precision.HIGHEST

EPS = 1e-6
D_MODEL = 2048
LANES = 128
SUBLANES = 8
SSM_D_INNER = 4096
SSM_HEAD_DIM = 64
SSM_N_HEADS = 64
SSM_N_GROUPS = 8
SSM_HPG = 8
SSM_D_STATE = 128
SSM_CHUNK = 128
SSM_GROUP_W = SSM_HPG * SSM_HEAD_DIM
GDN_N_HEADS = 16
GDN_HEAD_K = 128
GDN_HEAD_V = 256
GDN_CHUNK = 64
GDN_KEY_DIM = GDN_N_HEADS * GDN_HEAD_K
GDN_VAL_DIM = GDN_N_HEADS * GDN_HEAD_V
CONV_K = 4
PEER_HEADS = 8
PEER_N_KEYS = 128
PEER_TOPK = 16
PEER_N_EXPERTS = PEER_N_KEYS * PEER_N_KEYS

COL_SSM_Z = 0
COL_SSM_XBC = 4096
COL_GDN_QKV = 10240
COL_GDN_Z = 18432
COL_GATE = 22528
N_MAIN = 26624
SMALL_BETA = 64
SMALL_ALPHA = 80

VMEM_LIMIT = 56 * 1024 * 1024


def _softplus(x):
    return jnp.maximum(x, 0.0) + jnp.log1p(jnp.exp(-jnp.abs(x)))


def _silu(x):
    return (0.5 * x) * (1.0 + jnp.tanh(0.5 * x))


def _dot(a, b, **kw):
    return jnp.dot(a, b, preferred_element_type=F32, **kw)


def _dot_nt(a, b, **kw):
    return lax.dot_general(a, b, (((1,), (1,)), ((), ())), preferred_element_type=F32, **kw)


def _dot_tn(a, b, **kw):
    return lax.dot_general(a, b, (((0,), (0,)), ((), ())), preferred_element_type=F32, **kw)


def _params(n_grid):
    return pltpu.CompilerParams(dimension_semantics=("arbitrary",) * n_grid,
                                vmem_limit_bytes=VMEM_LIMIT)


def _inproj_kernel(x_ref, nw_ref, w_ref, ws_ref, o_ref, os_ref, xn_ref):
    @pl.when(pl.program_id(1) == 0)
    def _():
        x = x_ref[...]
        ms = jnp.mean(x * x, axis=-1, keepdims=True)
        xn = (x * lax.rsqrt(ms + EPS) * nw_ref[...]).astype(BF16)
        xn_ref[...] = xn
        os_ref[...] = _dot(xn, ws_ref[...])

    o_ref[...] = _dot(xn_ref[...], w_ref[...]).astype(BF16)


def _inproj(x, norm_w, w_main, w_small, tm=1024, tn=1024):
    t, d = x.shape
    n = w_main.shape[1]
    tm = min(tm, t)
    return pl.pallas_call(
        _inproj_kernel,
        grid=(t // tm, n // tn),
        in_specs=[
            pl.BlockSpec((tm, d), lambda i, j: (i, 0)),
            pl.BlockSpec((1, d), lambda i, j: (0, 0)),
            pl.BlockSpec((d, tn), lambda i, j: (0, j)),
            pl.BlockSpec((d, LANES), lambda i, j: (0, 0)),
        ],
        out_specs=[
            pl.BlockSpec((tm, tn), lambda i, j: (i, j)),
            pl.BlockSpec((tm, LANES), lambda i, j: (i, 0)),
        ],
        out_shape=[jax.ShapeDtypeStruct((t, n), BF16), jax.ShapeDtypeStruct((t, LANES), F32)],
        scratch_shapes=[pltpu.VMEM((tm, d), BF16)],
        compiler_params=_params(2),
        name="inproj",
    )(x, norm_w, w_main, w_small)


def _shift_mats(tc):
    return jnp.concatenate([jnp.eye(tc, k=-(CONV_K - 1 - k), dtype=BF16) for k in range(CONV_K - 1)], axis=0)


def _conv_block(raw_ref, carry_ref, w_ref, shift_ref, tc):
    x16 = raw_ref[...]
    xf = x16.astype(F32)
    y = xf * w_ref[CONV_K - 1:CONV_K, :]
    c8 = carry_ref[...]
    row8 = lax.broadcasted_iota(jnp.int32, c8.shape, 0)
    head = jnp.zeros_like(c8)
    shifted = _dot(shift_ref[...], x16)
    for k in range(CONV_K - 1):
        lag = CONV_K - 1 - k
        y = y + shifted[k * tc:(k + 1) * tc] * w_ref[k:k + 1, :]
        head = head + jnp.where(row8 < lag, pltpu.roll(c8, shift=lag, axis=0), 0.0) * w_ref[k:k + 1, :]
    carry_ref[...] = xf[tc - SUBLANES:tc]
    return jnp.concatenate([y[0:SUBLANES] + head, y[SUBLANES:]], axis=0)


def _ssd_kernel(xs_ref, b_ref, c_ref, z_ref, sm_ref, cwx_ref, cwb_ref, cwc_ref, cbx_ref, cbb_ref,
                cbc_ref, dtb_ref, alog_ref, dsk_ref, nw_ref, sh_ref, o_ref,
                xbuf, bbuf, cbuf, state_ref, acst_ref, *, tc, gps):
    g0 = pl.program_id(0) * gps
    L, GW, NS = SSM_CHUNK, SSM_GROUP_W, SSM_D_STATE

    @pl.when(pl.program_id(1) == 0)
    def _():
        xbuf[...] = jnp.zeros_like(xbuf)
        bbuf[...] = jnp.zeros_like(bbuf)
        cbuf[...] = jnp.zeros_like(cbuf)
        state_ref[...] = jnp.zeros_like(state_ref)

    xs_all = _silu(_conv_block(xs_ref, xbuf, cwx_ref, sh_ref, tc) + cbx_ref[...])
    b_all = _silu(_conv_block(b_ref, bbuf, cwb_ref, sh_ref, tc) + cbb_ref[...])
    c_all = _silu(_conv_block(c_ref, cbuf, cwc_ref, sh_ref, tc) + cbc_ref[...])

    row = lax.broadcasted_iota(jnp.int32, (L, L), 0)
    col = lax.broadcasted_iota(jnp.int32, (L, L), 1)
    causal = row >= col
    tri = causal.astype(F32)
    lo_half = col < SSM_HEAD_DIM
    a_row = -jnp.exp(alog_ref[...])

    for c in range(tc // L):
        sl = slice(c * L, (c + 1) * L)
        dt_full = _softplus(sm_ref[sl, :] + dtb_ref[...])
        acs_full = _dot(tri, dt_full * a_row, precision=HIGHEST)
        acst_ref[c] = acs_full.T
        for gg in range(gps):
            xs = xs_all[sl, gg * GW:(gg + 1) * GW]
            bm16 = b_all[sl, gg * NS:(gg + 1) * NS].astype(BF16)
            cm16 = c_all[sl, gg * NS:(gg + 1) * NS].astype(BF16)
            cb = _dot_nt(cm16, bm16)
            ys = []
            for p in range(SSM_HPG // 2):
                acs_bc, dt_bc, scores = [], [], []
                for r in (2 * p, 2 * p + 1):
                    j = (g0 + gg) * SSM_HPG + r
                    sel = col == j
                    a_c = jnp.sum(jnp.where(sel, acs_full, 0.0), axis=1, keepdims=True)
                    d_c = jnp.sum(jnp.where(sel, dt_full, 0.0), axis=1, keepdims=True)
                    a_b = jnp.broadcast_to(a_c, (L, L))
                    acs_bc.append(a_b)
                    dt_bc.append(jnp.broadcast_to(d_c, (L, L)))
                    a_r = acst_ref[c, pl.ds(j, 1), :]
                    seg = jnp.where(causal, a_b - a_r, -jnp.inf)
                    scores.append((cb * jnp.exp(seg)).astype(BF16))
                acs_e = jnp.where(lo_half, acs_bc[0], acs_bc[1])
                dt_e = jnp.where(lo_half, dt_bc[0], dt_bc[1])
                x_p = xs[:, p * LANES:(p + 1) * LANES]
                lanes = slice(gg * GW + p * LANES, gg * GW + (p + 1) * LANES)
                xdt = x_p * dt_e
                last = acs_e[L - 1:L, :]
                xdtw = (xdt * jnp.exp(last - acs_e)).astype(BF16)
                s2 = jnp.concatenate(scores, axis=1)
                x2 = jnp.concatenate([jnp.where(lo_half, xdt, 0.0), jnp.where(lo_half, 0.0, xdt)],
                                     axis=0).astype(BF16)
                y_diag = _dot(s2, x2)
                st = state_ref[:, lanes]
                y_off = _dot(cm16, st.astype(BF16)) * jnp.exp(acs_e)
                state_ref[:, lanes] = st * jnp.exp(last) + _dot_tn(bm16, xdtw)
                ys.append(y_diag + y_off + x_p * dsk_ref[:, lanes])
            gl = slice(gg * GW, (gg + 1) * GW)
            y = jnp.concatenate(ys, axis=1) * _silu(z_ref[sl, gl].astype(F32))
            y = y * lax.rsqrt(jnp.mean(y * y, axis=-1, keepdims=True) + EPS)
            o_ref[sl, gl] = (y * nw_ref[:, gl]).astype(BF16)


def _ssd(proj, small, cwx, cwb, cwc, cbx, cbb, cbc, dtb_row, alog_row, dsk_row, nw_row, tc=256, gps=4):
    t = proj.shape[0]
    tc = min(tc, t)
    gw, ns = gps * SSM_GROUP_W, gps * SSM_D_STATE
    xs_blk = COL_SSM_XBC // gw
    b_blk = (COL_SSM_XBC + SSM_D_INNER) // ns
    c_blk = b_blk + SSM_N_GROUPS // gps
    z_blk = COL_SSM_Z // gw
    kern = functools.partial(_ssd_kernel, tc=tc, gps=gps)
    return pl.pallas_call(
        kern,
        grid=(SSM_N_GROUPS // gps, t // tc),
        in_specs=[
            pl.BlockSpec((tc, gw), lambda g, i: (i, xs_blk + g)),
            pl.BlockSpec((tc, ns), lambda g, i: (i, b_blk + g)),
            pl.BlockSpec((tc, ns), lambda g, i: (i, c_blk + g)),
            pl.BlockSpec((tc, gw), lambda g, i: (i, z_blk + g)),
            pl.BlockSpec((tc, LANES), lambda g, i: (i, 0)),
            pl.BlockSpec((CONV_K, gw), lambda g, i: (0, g)),
            pl.BlockSpec((CONV_K, ns), lambda g, i: (0, g)),
            pl.BlockSpec((CONV_K, ns), lambda g, i: (0, g)),
            pl.BlockSpec((1, gw), lambda g, i: (0, g)),
            pl.BlockSpec((1, ns), lambda g, i: (0, g)),
            pl.BlockSpec((1, ns), lambda g, i: (0, g)),
            pl.BlockSpec((1, LANES), lambda g, i: (0, 0)),
            pl.BlockSpec((1, LANES), lambda g, i: (0, 0)),
            pl.BlockSpec((1, gw), lambda g, i: (0, g)),
            pl.BlockSpec((1, gw), lambda g, i: (0, g)),
            pl.BlockSpec(((CONV_K - 1) * tc, tc), lambda g, i: (0, 0)),
        ],
        out_specs=pl.BlockSpec((tc, gw), lambda g, i: (i, g)),
        out_shape=jax.ShapeDtypeStruct((t, SSM_D_INNER), BF16),
        scratch_shapes=[
            pltpu.VMEM((SUBLANES, gw), F32),
            pltpu.VMEM((SUBLANES, ns), F32),
            pltpu.VMEM((SUBLANES, ns), F32),
            pltpu.VMEM((SSM_D_STATE, gw), F32),
            pltpu.VMEM((tc // SSM_CHUNK, LANES, SSM_CHUNK), F32),
        ],
        compiler_params=_params(2),
        name="ssd",
    )(proj, proj, proj, proj, small, cwx, cwb, cwc, cbx, cbb, cbc, dtb_row, alog_row, dsk_row, nw_row,
      _shift_mats(tc))


def _gdn_kernel(q_ref, k_ref, v_ref, z_ref, sm_ref, cwq_ref, cwk_ref, cwv_ref, dtb_ref, alog_ref,
                nw_ref, sh_ref, o_ref, qbuf, kbuf, vbuf, s_ref, *, tc, hps):
    h0 = pl.program_id(0) * hps
    C, DK, DV = GDN_CHUNK, GDN_HEAD_K, GDN_HEAD_V

    @pl.when(pl.program_id(1) == 0)
    def _():
        qbuf[...] = jnp.zeros_like(qbuf)
        kbuf[...] = jnp.zeros_like(kbuf)
        vbuf[...] = jnp.zeros_like(vbuf)
        s_ref[...] = jnp.zeros_like(s_ref)

    q_cv = _silu(_conv_block(q_ref, qbuf, cwq_ref, sh_ref, tc))
    k_cv = _silu(_conv_block(k_ref, kbuf, cwk_ref, sh_ref, tc))
    v_cv = _silu(_conv_block(v_ref, vbuf, cwv_ref, sh_ref, tc))

    sm = sm_ref[...]
    lane = lax.broadcasted_iota(jnp.int32, sm.shape, 1)
    beta_full = jax.nn.sigmoid(sm)
    g_full = -jnp.exp(alog_ref[...]) * _softplus(sm + dtb_ref[...])
    q_hd, k_hd, v_hd, beta_hd, g_hd = [], [], [], [], []
    for hd in range(hps):
        q = q_cv[:, hd * DK:(hd + 1) * DK]
        k = k_cv[:, hd * DK:(hd + 1) * DK]
        q_hd.append(q * lax.rsqrt(jnp.sum(q * q, axis=-1, keepdims=True) + EPS) * (DK ** -0.5))
        k_hd.append(k * lax.rsqrt(jnp.sum(k * k, axis=-1, keepdims=True) + EPS))
        v_hd.append(v_cv[:, hd * DV:(hd + 1) * DV])
        beta_hd.append(jnp.sum(jnp.where(lane == SMALL_BETA + h0 + hd, beta_full, 0.0),
                               axis=1, keepdims=True))
        g_hd.append(jnp.sum(jnp.where(lane == SMALL_ALPHA + h0 + hd, g_full, 0.0),
                            axis=1, keepdims=True))

    row = lax.broadcasted_iota(jnp.int32, (C, C), 0)
    col = lax.broadcasted_iota(jnp.int32, (C, C), 1)
    causal = row >= col
    strict = row > col
    tri = causal.astype(F32)
    eye = (row == col).astype(F32)

    nc = tc // C
    items = [(hd, slice(c * C, (c + 1) * C)) for hd in range(hps) for c in range(nc)]
    rng = range(len(items))
    strict_w = (lax.broadcasted_iota(jnp.int32, (C, LANES), 0)
                > lax.broadcasted_iota(jnp.int32, (C, LANES), 1))
    qs = [q_hd[hd][sl] for hd, sl in items]
    ks = [k_hd[hd][sl] for hd, sl in items]
    vs = [v_hd[hd][sl] for hd, sl in items]
    betas = [beta_hd[hd][sl] for hd, sl in items]
    g_bc = [jnp.broadcast_to(g_hd[hd][sl], (C, LANES)) for hd, sl in items]
    cums = [_dot(tri, jnp.concatenate([g_bc[c], jnp.where(strict_w, g_bc[c], 0.0)], axis=1),
                 precision=HIGHEST) for c in rng]
    gc_bc = [x[:, :LANES] for x in cums]
    decay = [jnp.exp(jnp.where(causal, x[:, LANES:LANES + C], -jnp.inf)) for x in cums]
    k16 = [k.astype(BF16) for k in ks]
    q16 = [q.astype(BF16) for q in qs]
    kk = [_dot_nt(k16[c], k16[c]) for c in rng]
    qk = [_dot_nt(q16[c], k16[c]) for c in rng]
    p = [-jnp.where(strict, kk[c] * betas[c] * decay[c], 0.0) for c in rng]
    x = [eye + p[c] for c in rng]
    for _ in range(C.bit_length() - 2):
        p16 = [pp.astype(BF16) for pp in p]
        p = [_dot(p16[c], p16[c]) for c in rng]
        x = [x[c] + _dot(x[c].astype(BF16), p[c].astype(BF16)) for c in rng]
    t_inv = [xx.astype(BF16) for xx in x]
    egc = [jnp.exp(gc) for gc in gc_bc]
    u = [_dot(t_inv[c], (vs[c] * betas[c]).astype(BF16)).astype(BF16) for c in rng]
    w = [_dot(t_inv[c], (ks[c] * (betas[c] * egc[c])).astype(BF16)).astype(BF16) for c in rng]
    attn = [jnp.where(causal, qk[c] * decay[c], 0.0).astype(BF16) for c in rng]
    gc_last = [gc[C - 1:C, :] for gc in gc_bc]
    k_dec = [(ks[c] * jnp.exp(gc_last[c] - gc_bc[c])).astype(BF16) for c in rng]
    c_dec = [jnp.exp(gl[:, :1]) for gl in gc_last]
    a_mat = [_dot_tn(k_dec[c], w[c]).astype(BF16) for c in rng]
    n_mat = [_dot_tn(k_dec[c], u[c]) for c in rng]
    q_eff = [(qs[c] * egc[c] - _dot(attn[c], w[c])).astype(BF16) for c in rng]
    o_u = [_dot(attn[c], u[c]) for c in rng]

    s = [s_ref[hd] for hd in range(hps)]
    for c in range(nc):
        sl = slice(c * C, (c + 1) * C)
        for hd in range(hps):
            it = hd * nc + c
            s16 = s[hd].astype(BF16)
            o = _dot(q_eff[it], s16) + o_u[it]
            s[hd] = s[hd] * c_dec[it] - _dot(a_mat[it], s16) + n_mat[it]
            o = o * lax.rsqrt(jnp.mean(o * o, axis=-1, keepdims=True) + EPS) * nw_ref[...]
            z = z_ref[sl, hd * DV:(hd + 1) * DV].astype(F32)
            o_ref[sl, hd * DV:(hd + 1) * DV] = (o * _silu(z)).astype(BF16)
    for hd in range(hps):
        s_ref[hd] = s[hd]


def _gdn(proj, small, cwq, cwk, cwv, dtb_row, alog_row, nw_row, tc=256, hps=8):
    t = proj.shape[0]
    tc = min(tc, t)
    dk, dv = hps * GDN_HEAD_K, hps * GDN_HEAD_V
    q_blk = COL_GDN_QKV // dk
    k_blk = q_blk + GDN_N_HEADS // hps
    v_blk = (COL_GDN_QKV + 2 * GDN_KEY_DIM) // dv
    z_blk = COL_GDN_Z // dv
    kern = functools.partial(_gdn_kernel, tc=tc, hps=hps)
    return pl.pallas_call(
        kern,
        grid=(GDN_N_HEADS // hps, t // tc),
        in_specs=[
            pl.BlockSpec((tc, dk), lambda h, i: (i, q_blk + h)),
            pl.BlockSpec((tc, dk), lambda h, i: (i, k_blk + h)),
            pl.BlockSpec((tc, dv), lambda h, i: (i, v_blk + h)),
            pl.BlockSpec((tc, dv), lambda h, i: (i, z_blk + h)),
            pl.BlockSpec((tc, LANES), lambda h, i: (i, 0)),
            pl.BlockSpec((CONV_K, dk), lambda h, i: (0, h)),
            pl.BlockSpec((CONV_K, dk), lambda h, i: (0, h)),
            pl.BlockSpec((CONV_K, dv), lambda h, i: (0, h)),
            pl.BlockSpec((1, LANES), lambda h, i: (0, 0)),
            pl.BlockSpec((1, LANES), lambda h, i: (0, 0)),
            pl.BlockSpec((1, GDN_HEAD_V), lambda h, i: (0, 0)),
            pl.BlockSpec(((CONV_K - 1) * tc, tc), lambda h, i: (0, 0)),
        ],
        out_specs=pl.BlockSpec((tc, dv), lambda h, i: (i, h)),
        out_shape=jax.ShapeDtypeStruct((t, GDN_VAL_DIM), BF16),
        scratch_shapes=[
            pltpu.VMEM((SUBLANES, dk), F32),
            pltpu.VMEM((SUBLANES, dk), F32),
            pltpu.VMEM((SUBLANES, dv), F32),
            pltpu.VMEM((hps, GDN_HEAD_K, GDN_HEAD_V), F32),
        ],
        compiler_params=_params(2),
        name="gdn",
    )(proj, proj, proj, proj, small, cwq, cwk, cwv, dtb_row, alog_row, nw_row, _shift_mats(tc))


def _mix_kernel(ys_ref, yg_ref, g0_ref, g1_ref, gb_ref, ws_ref, wg_ref, o_ref):
    a = _dot(ys_ref[...], ws_ref[...])
    b = _dot(yg_ref[...], wg_ref[...])
    g0 = jax.nn.sigmoid(g0_ref[...].astype(F32) + gb_ref[0:1, :])
    g1 = jax.nn.sigmoid(g1_ref[...].astype(F32) + gb_ref[1:2, :])
    o_ref[...] = (g0 * a + g1 * b).astype(BF16)


def _mix(y_ssm, y_gdn, proj, gate_b, w_s, w_g, tm=512, tn=512):
    t = y_ssm.shape[0]
    tm = min(tm, t)
    g0_blk = COL_GATE // tn
    g1_blk = (COL_GATE + D_MODEL) // tn
    return pl.pallas_call(
        _mix_kernel,
        grid=(D_MODEL // tn, t // tm),
        in_specs=[
            pl.BlockSpec((tm, SSM_D_INNER), lambda n, i: (i, 0)),
            pl.BlockSpec((tm, GDN_VAL_DIM), lambda n, i: (i, 0)),
            pl.BlockSpec((tm, tn), lambda n, i: (i, g0_blk + n)),
            pl.BlockSpec((tm, tn), lambda n, i: (i, g1_blk + n)),
            pl.BlockSpec((2, tn), lambda n, i: (0, n)),
            pl.BlockSpec((SSM_D_INNER, tn), lambda n, i: (0, n)),
            pl.BlockSpec((GDN_VAL_DIM, tn), lambda n, i: (0, n)),
        ],
        out_specs=pl.BlockSpec((tm, tn), lambda n, i: (i, n)),
        out_shape=jax.ShapeDtypeStruct((t, D_MODEL), BF16),
        compiler_params=_params(2),
        name="mix",
    )(y_ssm, y_gdn, proj, proj, gate_b, w_s, w_g)


_N_RANK = PEER_TOPK + 1
_CAND_PAIRS = [(i, j) for i in range(_N_RANK) for j in range(_N_RANK) if (i + 1) * (j + 1) <= _N_RANK]
_CAND_ROWS = SUBLANES * (1 << (-(-len(_CAND_PAIRS) // SUBLANES) - 1).bit_length())


def _top_desc(s, n):
    nv = s.shape[0] // SUBLANES
    v = [s[j * SUBLANES:(j + 1) * SUBLANES, :] for j in range(nv)]
    k = 2
    while k <= nv:
        j = k // 2
        while j >= 1:
            for i in range(nv):
                l = i ^ j
                if l > i:
                    hi, lo = jnp.maximum(v[i], v[l]), jnp.minimum(v[i], v[l])
                    v[i], v[l] = (hi, lo) if (i & k) == 0 else (lo, hi)
            j //= 2
        k *= 2
    out = []
    for t in range(n):
        m = jnp.max(v[0], axis=0, keepdims=True)
        out.append(m)
        pop = v[0] == m
        live = min(nv, n - t)
        for j in range(live - 1):
            v[j] = jnp.where(pop, v[j + 1], v[j])
        v[live - 1] = jnp.where(pop, -jnp.inf, v[live - 1])
    return out


def _post_kernel(x_ref, mix_ref, wo_ref, fw_ref, wq_ref, sk_ref,
                 h1_ref, hnt_ref, s2_ref, e2_ref, th_ref, e1_ref, cand_ref, *, tm):
    h1 = x_ref[...] + _dot(mix_ref[...], wo_ref[...])
    h1_ref[...] = h1
    hn = h1 * lax.rsqrt(jnp.mean(h1 * h1, axis=-1, keepdims=True) + EPS) * fw_ref[...]
    hnt_ref[...] = hn.T.astype(BF16)
    qv = _dot(hn.astype(BF16), wq_ref[...]).astype(BF16)
    nchunk = tm // LANES
    for h in range(PEER_HEADS):
        s1 = _dot_nt(sk_ref[2 * h], qv[:, (2 * h) * LANES:(2 * h + 1) * LANES])
        s2 = _dot_nt(sk_ref[2 * h + 1], qv[:, (2 * h + 1) * LANES:(2 * h + 2) * LANES])
        a1 = _top_desc(s1, _N_RANK)
        a2 = _top_desc(s2, _N_RANK)
        cand_ref[...] = jnp.full(cand_ref.shape, -jnp.inf, F32)
        for r, (i, j) in enumerate(_CAND_PAIRS):
            cand_ref[pl.ds(r, 1), :] = a1[i] + a2[j]
        cand = cand_ref[...]
        tops = _top_desc(cand, _N_RANK)
        v16, v17 = tops[PEER_TOPK - 1], tops[PEER_TOPK]
        tau = 0.5 * (v16 + v17)
        tau = jnp.where(v17 == -jnp.inf, v16, tau)
        m_tot = a1[0] + a2[0]
        z = jnp.sum(jnp.where(cand >= tau, jnp.exp(cand - m_tot), 0.0), axis=0, keepdims=True)
        e2 = jnp.exp(s2 - a2[0]) / z
        e1 = jnp.exp(s1 - a1[0])
        th = tau - s1
        for cc in range(nchunk):
            ls = slice(cc * LANES, (cc + 1) * LANES)
            s2_ref[cc, h] = s2[:, ls]
            e2_ref[cc, h] = e2[:, ls]
            th_ref[cc, h] = th[:, ls]
            e1_ref[cc, h] = e1[:, ls]


def _post(x, mix, w_out, ffn_w, w_q, sub_keys, tm=256):
    t, d = x.shape
    tm = min(tm, t)
    nchunk = tm // LANES
    stat = jax.ShapeDtypeStruct((t // LANES, PEER_HEADS, PEER_N_KEYS, LANES), F32)
    stat_spec = pl.BlockSpec((nchunk, PEER_HEADS, PEER_N_KEYS, LANES), lambda i: (i, 0, 0, 0))
    kern = functools.partial(_post_kernel, tm=tm)
    return pl.pallas_call(
        kern,
        grid=(t // tm,),
        in_specs=[
            pl.BlockSpec((tm, d), lambda i: (i, 0)),
            pl.BlockSpec((tm, d), lambda i: (i, 0)),
            pl.BlockSpec((d, d), lambda i: (0, 0)),
            pl.BlockSpec((1, d), lambda i: (0, 0)),
            pl.BlockSpec((d, d), lambda i: (0, 0)),
            pl.BlockSpec((2 * PEER_HEADS, PEER_N_KEYS, LANES), lambda i: (0, 0, 0)),
        ],
        out_specs=[
            pl.BlockSpec((tm, d), lambda i: (i, 0)),
            pl.BlockSpec((d, tm), lambda i: (0, i)),
            stat_spec, stat_spec, stat_spec, stat_spec,
        ],
        out_shape=[
            jax.ShapeDtypeStruct((t, d), F32),
            jax.ShapeDtypeStruct((d, t), BF16),
            stat, stat, stat, stat,
        ],
        scratch_shapes=[pltpu.VMEM((_CAND_ROWS, tm), F32)],
        compiler_params=_params(1),
        name="post",
    )(x, mix, w_out, ffn_w, w_q, sub_keys)


_ROW_TILE = 32


def _gelu(x):
    return 0.5 * x * (1.0 + lax.erf(x * (2.0 ** -0.5)))


def _peer_kernel(hnt_ref, u0_ref, u1_ref, vt0_ref, vtp_ref, vtl_ref, s2_ref, e2_ref, th_ref, e1_ref, acc_ref,
                 at0_ref, at1_ref, pt0_ref, pt1_ref, *, tb, eb):
    n1 = eb // PEER_N_KEYS
    nchunk = tb // LANES
    nrt = PEER_N_KEYS // _ROW_TILE
    d = acc_ref.shape[0]

    @pl.when(pl.program_id(1) == 0)
    def _():
        acc_ref[...] = jnp.zeros_like(acc_ref)
        pt1_ref[...] = jnp.zeros_like(pt1_ref)

    hnt = hnt_ref[...]
    at0_ref[...] = _dot(u0_ref[...], hnt)
    at1_ref[...] = _dot(u1_ref[...], hnt)

    def tile(at_ref, pt_ref, row0, cc, rt):
        ls = slice(cc * LANES, (cc + 1) * LANES)
        rs = pl.ds(pl.multiple_of(rt * _ROW_TILE, _ROW_TILE), _ROW_TILE)
        w = [jnp.zeros((_ROW_TILE, LANES), F32) for _ in range(n1)]
        for h in range(PEER_HEADS):
            s2 = s2_ref[cc, h, rs, :]
            e2 = e2_ref[cc, h, rs, :]
            for i in range(n1):
                th = th_ref[cc, h, row0 + i:row0 + i + 1, :]
                e1 = e1_ref[cc, h, row0 + i:row0 + i + 1, :]
                w[i] = w[i] + jnp.where(s2 >= th, e2, 0.0) * e1
        for i in range(n1):
            r = pl.ds(pl.multiple_of(i * PEER_N_KEYS + rt * _ROW_TILE, _ROW_TILE), _ROW_TILE)
            pt_ref[r, ls] = (w[i] * _gelu(at_ref[r, ls])).astype(BF16)

    c_rows = d // nrt

    def body0(rt, carry):
        for cc in range(nchunk):
            tile(at0_ref, pt0_ref, 0, cc, rt)
        r = pl.ds(pl.multiple_of(rt * c_rows, c_rows), c_rows)
        acc_ref[r, :] += _dot(vtp_ref[r, :], pt1_ref[...])
        return carry
    lax.fori_loop(0, nrt, body0, 0)

    def body1(rt, carry):
        for cc in range(nchunk):
            tile(at1_ref, pt1_ref, n1, cc, rt)
        r = pl.ds(pl.multiple_of(rt * c_rows, c_rows), c_rows)
        acc_ref[r, :] += _dot(vt0_ref[r, :], pt0_ref[...])
        return carry
    lax.fori_loop(0, nrt, body1, 0)

    @pl.when(pl.program_id(1) == pl.num_programs(1) - 1)
    def _():
        acc_ref[...] += _dot(vtl_ref[...], pt1_ref[...])


def _peer(hnt, u16, v16, s2, e2, th, e1, tb=512, eb=512):
    d, t = hnt.shape
    tb = min(tb, t)
    nchunk = tb // LANES
    n1 = eb // PEER_N_KEYS
    nk = PEER_N_EXPERTS // (2 * eb)
    vt_slabs = jnp.swapaxes(v16.reshape(2 * nk, eb, d), 1, 2)
    assert 2 * n1 == SUBLANES, "a pair of expert blocks spans one sublane group of first-key statistics"
    stat_spec = pl.BlockSpec((nchunk, PEER_HEADS, PEER_N_KEYS, LANES), lambda i, k: (i, 0, 0, 0))
    row_spec = pl.BlockSpec((nchunk, PEER_HEADS, SUBLANES, LANES), lambda i, k: (i, 0, k, 0))
    kern = functools.partial(_peer_kernel, tb=tb, eb=eb)
    return pl.pallas_call(
        kern,
        grid=(t // tb, nk),
        in_specs=[
            pl.BlockSpec((d, tb), lambda i, k: (0, i)),
            pl.BlockSpec((eb, d), lambda i, k: (2 * k, 0)),
            pl.BlockSpec((eb, d), lambda i, k: (2 * k + 1, 0)),
            pl.BlockSpec((None, d, eb), lambda i, k: (2 * k, 0, 0)),
            pl.BlockSpec((None, d, eb), lambda i, k: (2 * jnp.maximum(k - 1, 0) + 1, 0, 0)),
            pl.BlockSpec((None, d, eb), lambda i, k: (2 * nk - 1, 0, 0)),
            stat_spec, stat_spec, row_spec, row_spec,
        ],
        out_specs=pl.BlockSpec((d, tb), lambda i, k: (0, i)),
        out_shape=jax.ShapeDtypeStruct((d, t), F32),
        scratch_shapes=[pltpu.VMEM((eb, tb), F32), pltpu.VMEM((eb, tb), F32),
                        pltpu.VMEM((eb, tb), BF16), pltpu.VMEM((eb, tb), BF16)],
        compiler_params=_params(2),
        name="peer",
    )(hnt, u16, u16, vt_slabs, vt_slabs, vt_slabs, s2, e2, th, e1)


def _final_kernel(h_ref, pt_ref, w_ref, o_ref):
    hf = h_ref[...] + pt_ref[...].T
    o_ref[...] = hf * lax.rsqrt(jnp.mean(hf * hf, axis=-1, keepdims=True) + EPS) * w_ref[...]


def _final(h1, peer_out_t, w_row, tm=512):
    t, d = h1.shape
    tm = min(tm, t)
    spec = pl.BlockSpec((tm, d), lambda i: (i, 0))
    return pl.pallas_call(
        _final_kernel,
        grid=(t // tm,),
        in_specs=[spec, pl.BlockSpec((d, tm), lambda i: (0, i)), pl.BlockSpec((1, d), lambda i: (0, 0))],
        out_specs=spec,
        out_shape=jax.ShapeDtypeStruct((t, d), F32),
        compiler_params=_params(1),
        name="final",
    )(h1, peer_out_t, w_row)


def _row(v, width=None):
    v = v.astype(F32).reshape(1, -1)
    if width is not None and v.shape[1] < width:
        v = jnp.pad(v, ((0, 0), (0, width - v.shape[1])))
    return v


def _layer(h, mix_norm_w, w_in, gate_b, ssm_conv_w, ssm_conv_b, ssm_dt_bias, ssm_a_log, ssm_d,
           ssm_norm_w, gdn_conv_w, gdn_dt_bias, gdn_a_log, gdn_norm_w, w_branch_ssm, w_branch_gdn,
           w_out, ffn_norm_w, peer_w_q, peer_sub_keys, peer_u, peer_v, out_norm_w):
    ssm_proj = SSM_D_INNER + (SSM_D_INNER + 2 * SSM_N_GROUPS * SSM_D_STATE) + SSM_N_HEADS
    gdn_conv_dim = 2 * GDN_KEY_DIM + GDN_VAL_DIM
    o_dt = ssm_proj - SSM_N_HEADS
    o_qkv = ssm_proj
    o_gz = o_qkv + gdn_conv_dim
    o_beta = o_gz + GDN_VAL_DIM
    o_gate = o_beta + 2 * GDN_N_HEADS
    w_main = jnp.concatenate(
        [w_in[:, :o_dt], w_in[:, o_qkv:o_beta], w_in[:, o_gate:]], axis=1).astype(BF16)
    w_small = jnp.concatenate(
        [w_in[:, o_dt:o_qkv], w_in[:, o_beta:o_gate],
         jnp.zeros((D_MODEL, LANES - SSM_N_HEADS - 2 * GDN_N_HEADS), w_in.dtype)], axis=1).astype(BF16)

    proj, small = _inproj(h, _row(mix_norm_w), w_main, w_small)

    cw = ssm_conv_w.astype(F32)
    cbias = _row(ssm_conv_b)
    nx = SSM_D_INNER
    nb = SSM_N_GROUPS * SSM_D_STATE
    y_ssm = _ssd(proj, small,
                 cw[:, :nx], cw[:, nx:nx + nb], cw[:, nx + nb:],
                 cbias[:, :nx], cbias[:, nx:nx + nb], cbias[:, nx + nb:],
                 _row(ssm_dt_bias, LANES), _row(ssm_a_log, LANES),
                 _row(jnp.repeat(ssm_d, SSM_HEAD_DIM)), _row(ssm_norm_w))

    gw = gdn_conv_w.astype(F32)
    zeros_b = jnp.zeros((SMALL_ALPHA,), F32)
    y_gdn = _gdn(proj, small,
                 gw[:, :GDN_KEY_DIM], gw[:, GDN_KEY_DIM:2 * GDN_KEY_DIM], gw[:, 2 * GDN_KEY_DIM:],
                 _row(jnp.concatenate([zeros_b, gdn_dt_bias.astype(F32)]), LANES),
                 _row(jnp.concatenate([zeros_b, gdn_a_log.astype(F32)]), LANES),
                 _row(gdn_norm_w))

    mix = _mix(y_ssm, y_gdn, proj, gate_b.astype(F32),
               w_branch_ssm.astype(BF16), w_branch_gdn.astype(BF16))

    sk = peer_sub_keys.reshape(2 * PEER_HEADS, PEER_N_KEYS, LANES).astype(BF16)
    h1, hnt, s2, e2, th, e1 = _post(h, mix, w_out.astype(BF16), _row(ffn_norm_w),
                                    peer_w_q.astype(BF16), sk)

    peer_out_t = _peer(hnt, peer_u.astype(BF16), peer_v.astype(BF16), s2, e2, th, e1)
    return _final(h1, peer_out_t, _row(out_norm_w))


def kernel(x, mix_norm_w, w_in, gate_b, ssm_conv_w, ssm_conv_b, ssm_dt_bias, ssm_a_log, ssm_d,
           ssm_norm_w, gdn_conv_w, gdn_dt_bias, gdn_a_log, gdn_norm_w, w_branch_ssm, w_branch_gdn,
           w_out, ffn_norm_w, peer_w_q, peer_sub_keys, peer_u, peer_v, final_norm_w):
    b, t, d = x.shape
    assert b == 1 and mix_norm_w.shape[0] == 1, "single sequence, single layer"
    out = _layer(x[0], mix_norm_w[0], w_in[0], gate_b[0], ssm_conv_w[0], ssm_conv_b[0],
                 ssm_dt_bias[0], ssm_a_log[0], ssm_d[0], ssm_norm_w[0], gdn_conv_w[0],
                 gdn_dt_bias[0], gdn_a_log[0], gdn_norm_w[0], w_branch_ssm[0], w_branch_gdn[0],
                 w_out[0], ffn_norm_w[0], peer_w_q[0], peer_sub_keys[0], peer_u[0], peer_v[0],
                 final_norm_w)
    return out[None]
```
